```python
import jax, jax.numpy as jnp
from jax import lax
import numpy as np

D_MODEL = 1024
BATCH = 16
SEQ = 2048
DEPTH = 1

CHUNK = 64
Q_BLOCK = 128
EPS = 1e-6
MLA_HEADS = 8
QK_NOPE = 64
QK_ROPE = 32
QK_HEAD = QK_NOPE + QK_ROPE
V_HEAD = 64
Q_LORA = 256
KV_LORA = 128
ROPE_THETA = 10000.0
LRU_WIDTH = 1024
LRU_BLOCKS = 8
LRU_BW = LRU_WIDTH // LRU_BLOCKS
CONV_WIDTH = 4
LRU_C = 8.0
PEER_HEADS = 8
N_KEYS = 128
N_EXPERTS = N_KEYS * N_KEYS
PEER_DKEY = 256
PEER_HALF = PEER_DKEY // 2
PEER_TOPK = 16
TOKEN_BLOCK = 128
IN_COLS = Q_LORA + KV_LORA + QK_ROPE + 2 * LRU_WIDTH + 2 * D_MODEL
IN_SPLITS = [Q_LORA, Q_LORA + KV_LORA, Q_LORA + KV_LORA + QK_ROPE,
             Q_LORA + KV_LORA + QK_ROPE + LRU_WIDTH,
             Q_LORA + KV_LORA + QK_ROPE + 2 * LRU_WIDTH]

kernel_name = "hybrid_mla_rglru_peer_chunk_causal"


def rms_norm(x, g):
    xf = x.astype(jnp.float32)
    y = xf * lax.rsqrt(jnp.mean(xf * xf, axis=-1, keepdims=True) + EPS)
    return (y * g.astype(jnp.float32)).astype(x.dtype)


def apply_rope(t, cos, sin):
    t1, t2 = jnp.split(t, 2, axis=-1)
    return jnp.concatenate([t1 * cos - t2 * sin, t1 * sin + t2 * cos], axis=-1)


def mla_attention(c_q, c_kv, k_rope, positions, q_a_norm_g, w_uq, kv_a_norm_g, w_ukv,
                  q_norm_g, k_norm_g):
    B, S, _ = c_q.shape
    H = MLA_HEADS
    q = (rms_norm(c_q, q_a_norm_g) @ w_uq).reshape(B, S, H, QK_HEAD)
    kv = (rms_norm(c_kv, kv_a_norm_g) @ w_ukv).reshape(B, S, H, QK_NOPE + V_HEAD)
    k_nope, v = kv[..., :QK_NOPE], kv[..., QK_NOPE:]
    k = jnp.concatenate([k_nope, jnp.broadcast_to(k_rope[:, :, None, :], (B, S, H, QK_ROPE))], axis=-1)
    q = rms_norm(q, q_norm_g)
    k = rms_norm(k, k_norm_g)
    inv_freq = 1.0 / (ROPE_THETA ** (jnp.arange(0, QK_ROPE, 2, dtype=jnp.float32) / QK_ROPE))
    ang = positions.astype(jnp.float32)[..., None] * inv_freq
    cos = jnp.cos(ang)[:, :, None, :].astype(q.dtype)
    sin = jnp.sin(ang)[:, :, None, :].astype(q.dtype)
    q = jnp.concatenate([q[..., :QK_NOPE], apply_rope(q[..., QK_NOPE:], cos, sin)], axis=-1)
    k = jnp.concatenate([k[..., :QK_NOPE], apply_rope(k[..., QK_NOPE:], cos, sin)], axis=-1)
    nb = S // Q_BLOCK
    qb = q.reshape(B, nb, Q_BLOCK, H, QK_HEAD).transpose(1, 0, 2, 3, 4)
    key_chunk = jnp.arange(S) // CHUNK
    scale = QK_HEAD ** -0.5
    neg = jnp.finfo(jnp.float32).min

    def block(args):
        qi, bi = args
        s = jnp.einsum('bqhd,bkhd->bhqk', qi, k, preferred_element_type=jnp.float32) * scale
        q_chunk = (bi * Q_BLOCK + jnp.arange(Q_BLOCK)) // CHUNK
        mask = key_chunk[None, :] <= q_chunk[:, None]
        p = jax.nn.softmax(jnp.where(mask, s, neg), axis=-1).astype(v.dtype)
        return jnp.einsum('bhqk,bkhd->bqhd', p, v)

    o = lax.map(block, (qb, jnp.arange(nb)))
    return o.transpose(1, 0, 2, 3, 4).reshape(B, S, H * V_HEAD)


def rg_lru_branch(xb, gb, conv_w, conv_b, w_rg, b_rg, w_ig, b_ig, lru_lambda):
    B, S, C = xb.shape
    xc = lax.conv_general_dilated(xb, conv_w[:, None, :], window_strides=(1,),
                                  padding=[(CONV_WIDTH - 1, 0)],
                                  dimension_numbers=('NWC', 'WIO', 'NWC'),
                                  feature_group_count=C) + conv_b
    xblk = xc.reshape(B, S, LRU_BLOCKS, LRU_BW)
    r = jax.nn.sigmoid(jnp.einsum('bsni,nio->bsno', xblk, w_rg).reshape(B, S, C) + b_rg)
    i = jax.nn.sigmoid(jnp.einsum('bsni,nio->bsno', xblk, w_ig).reshape(B, S, C) + b_ig)
    log_a = (-LRU_C * r.astype(jnp.float32)) * jax.nn.softplus(-lru_lambda.astype(jnp.float32))
    a = jnp.exp(log_a)
    b = jnp.sqrt(-jnp.expm1(2.0 * log_a)) * (i * xc).astype(jnp.float32)

    def combine(e1, e2):
        a1, b1 = e1
        a2, b2 = e2
        return a1 * a2, a2 * b1 + b2

    _, h = lax.associative_scan(combine, (a, b), axis=1)
    return h.astype(xb.dtype) * jax.nn.gelu(gb)


def peer_ffn(h, w_query, sub_keys, expert_u, expert_v):
    B, S, D = h.shape
    hb = h.reshape((B * S) // TOKEN_BLOCK, TOKEN_BLOCK, D)
    K = PEER_TOPK

    def block(xt):
        q = (xt @ w_query).reshape(TOKEN_BLOCK, PEER_HEADS, 2, PEER_HALF)
        s = jnp.einsum('thpd,hpnd->thpn', q, sub_keys, preferred_element_type=jnp.float32)
        s_top, i_top = lax.top_k(s, K)
        cand = (s_top[:, :, 0, :, None] + s_top[:, :, 1, None, :]).reshape(TOKEN_BLOCK, PEER_HEADS, K * K)
        cand_idx = (i_top[:, :, 0, :, None] * N_KEYS + i_top[:, :, 1, None, :]).reshape(TOKEN_BLOCK, PEER_HEADS, K * K)
        best, pos = lax.top_k(cand, K)
        idx = jnp.take_along_axis(cand_idx, pos, axis=-1)
        g = jax.nn.softmax(best, axis=-1)
        u = expert_u[idx]
        v = expert_v[idx]
        act = jax.nn.gelu(jnp.einsum('td,thkd->thk', xt, u))
        return jnp.einsum('thk,thkd->td', (g * act).astype(xt.dtype), v)

    return lax.map(block, hb).reshape(B, S, D)


def _normal(key, shape, scale):
    return jax.random.normal(key, shape, jnp.float32) * scale


def setup_inputs(seed: int = 0) -> dict:
    key = jax.random.key(seed)
    ks = jax.random.split(key, 32)
    L, D, H = DEPTH, D_MODEL, MLA_HEADS
    x = _normal(ks[0], (BATCH, SEQ, D), 1.0)
    c = _normal(ks[1], (BATCH, D), 1.0)
    positions = (jax.random.randint(ks[2], (BATCH, 1), 0, 4096, dtype=jnp.int32)
                 + jnp.arange(SEQ, dtype=jnp.int32)[None, :]).astype(jnp.int32)
    u = jax.random.uniform(ks[21], (L, LRU_WIDTH), jnp.float32, 0.9, 0.999)
    a_base = u ** (1.0 / LRU_C)
    lru_lambda = jnp.log(a_base) - jnp.log1p(-a_base)
    return {
        'x': x,
        'c': c,
        'positions': positions,
        'w_ada': _normal(ks[3], (L, D, 6 * D), 0.5 * D ** -0.5),
        'b_ada': _normal(ks[4], (L, 6 * D), 0.02),
        'norm1_g': 1.0 + _normal(ks[5], (L, D), 0.02),
        'w_in': _normal(ks[6], (L, D, IN_COLS), D ** -0.5),
        'q_a_norm_g': 1.0 + _normal(ks[7], (L, Q_LORA), 0.02),
        'w_uq': _normal(ks[8], (L, Q_LORA, H * QK_HEAD), Q_LORA ** -0.5),
        'kv_a_norm_g': 1.0 + _normal(ks[9], (L, KV_LORA), 0.02),
        'w_ukv': _normal(ks[10], (L, KV_LORA, H * (QK_NOPE + V_HEAD)), KV_LORA ** -0.5),
        'q_norm_g': 1.0 + _normal(ks[11], (L, QK_HEAD), 0.02),
        'k_norm_g': 1.0 + _normal(ks[12], (L, QK_HEAD), 0.02),
        'w_o_attn': _normal(ks[13], (L, H * V_HEAD, D), (H * V_HEAD) ** -0.5),
        'conv_w': _normal(ks[14], (L, CONV_WIDTH, LRU_WIDTH), CONV_WIDTH ** -0.5),
        'conv_b': _normal(ks[15], (L, LRU_WIDTH), 0.02),
        'w_rg': _normal(ks[16], (L, LRU_BLOCKS, LRU_BW, LRU_BW), LRU_BW ** -0.5),
        'b_rg': _normal(ks[17], (L, LRU_WIDTH), 0.02),
        'w_ig': _normal(ks[18], (L, LRU_BLOCKS, LRU_BW, LRU_BW), LRU_BW ** -0.5),
        'b_ig': _normal(ks[19], (L, LRU_WIDTH), 0.02),
        'lru_lambda': lru_lambda,
        'w_o_lru': _normal(ks[22], (L, LRU_WIDTH, D), LRU_WIDTH ** -0.5),
        'w_out': _normal(ks[23], (L, D, D), D ** -0.5),
        'norm2_g': 1.0 + _normal(ks[24], (L, D), 0.02),
        'w_query': _normal(ks[25], (L, D, PEER_HEADS * PEER_DKEY), D ** -0.5),
        'sub_keys': _normal(ks[26], (L, PEER_HEADS, 2, N_KEYS, PEER_HALF), PEER_HALF ** -0.5),
        'expert_u': _normal(ks[27], (L, N_EXPERTS, D), D ** -0.5),
        'expert_v': _normal(ks[28], (L, N_EXPERTS, D), PEER_HEADS ** -0.5),
    }


def reference(x, c, positions, w_ada, b_ada, norm1_g, w_in, q_a_norm_g, w_uq, kv_a_norm_g,
              w_ukv, q_norm_g, k_norm_g, w_o_attn, conv_w, conv_b, w_rg, b_rg, w_ig, b_ig,
              lru_lambda, w_o_lru, w_out, norm2_g, w_query, sub_keys, expert_u, expert_v):
    cs = jax.nn.silu(c)
    for l in range(DEPTH):
        mod = cs @ w_ada[l] + b_ada[l]
        shift1, scale1, gate1, shift2, scale2, gate2 = [m[:, None, :] for m in jnp.split(mod, 6, axis=-1)]
        h = rms_norm(x, norm1_g[l]) * (1 + scale1) + shift1
        proj = h @ w_in[l]
        c_q, c_kv, k_rope, lru_x, lru_gate, merge_logits = jnp.split(proj, IN_SPLITS, axis=-1)
        attn = mla_attention(c_q, c_kv, k_rope, positions, q_a_norm_g[l], w_uq[l],
                             kv_a_norm_g[l], w_ukv[l], q_norm_g[l], k_norm_g[l])
        rec = rg_lru_branch(lru_x, lru_gate, conv_w[l], conv_b[l], w_rg[l], b_rg[l],
                            w_ig[l], b_ig[l], lru_lambda[l])
        g_attn, g_rec = jnp.split(jax.nn.sigmoid(merge_logits), 2, axis=-1)
        y = g_attn * (attn @ w_o_attn[l]) + g_rec * (rec @ w_o_lru[l])
        x = x + gate1 * (y @ w_out[l])
        h2 = rms_norm(x, norm2_g[l]) * (1 + scale2) + shift2
        x = x + gate2 * peer_ffn(h2, w_query[l], sub_keys[l], expert_u[l], expert_v[l])
    return x
```

```python
import functools

import jax
import jax.numpy as jnp
from jax import lax
from jax.experimental import pallas as pl
from jax.experimental.pallas import tpu as pltpu

F32 = jnp.float32
BF16 = jnp.bfloat16

EPS = 1e-6
LANES = 128
SUBLANES = 8
VMEM_LIMIT = 56 * 1024 * 1024

CHUNK = 64
MLA_HEADS = 8
QK_NOPE = 64
QK_ROPE = 32
QK_HEAD = QK_NOPE + QK_ROPE
V_HEAD = 64
Q_LORA = 256
KV_LORA = 128
ROPE_THETA = 10000.0
LRU_BLOCKS = 8
CONV_WIDTH = 4
LRU_C = 8.0
PEER_HEADS = 8
N_KEYS = 128
PEER_TOPK = 16

TM_PROJ = 256
TQ = 256
TS_LRU = 256
TL_SEL = 128
TM_PEER = 512
TE_PEER = 1024


def _cparams(sem):
    return pltpu.CompilerParams(dimension_semantics=sem, vmem_limit_bytes=VMEM_LIMIT)


def _dot(a, b):
    return jnp.dot(a, b, preferred_element_type=F32)


def _dot_nt(a, b):
    return lax.dot_general(a, b, (((1,), (1,)), ((), ())), preferred_element_type=F32)


def _mod_body(c_ref, w_ref, b_ref, o_ref):
    c = c_ref[...]
    cs = c * jax.nn.sigmoid(c)
    o_ref[...] = _dot(cs.astype(BF16), w_ref[...].astype(BF16)) + b_ref[...]


def _mod(c, w_ada, b_ada):
    B, D = c.shape
    N = w_ada.shape[1]
    tn = 1024
    return pl.pallas_call(
        _mod_body,
        grid=(N // tn,),
        in_specs=[pl.BlockSpec((B, D), lambda n: (0, 0)),
                  pl.BlockSpec((D, tn), lambda n: (0, n)),
                  pl.BlockSpec((1, tn), lambda n: (0, n))],
        out_specs=pl.BlockSpec((B, tn), lambda n: (0, n)),
        out_shape=jax.ShapeDtypeStruct((B, N), F32),
        compiler_params=_cparams(("parallel",)),
        name="mod",
    )(c, w_ada, b_ada.reshape(1, N))


def _rope(t, cos_l, sin_l, lane):
    rot = jnp.where(lane < QK_NOPE + QK_ROPE // 2,
                    pltpu.roll(t, LANES - QK_ROPE // 2, 1), pltpu.roll(t, QK_ROPE // 2, 1))
    return t * cos_l + rot * sin_l


def _in_proj_body(x_ref, pos_ref, shift_ref, scale_ref, g1_ref, wqkr_ref, wlru_ref, wmg_ref,
                  gqa_ref, wuq_ref, gkva_ref, wuk_ref, wuv_ref, gqn_ref, gkn_ref, freq_ref, sign_ref,
                  q_ref, k_ref, v_ref, lx_ref, lg_ref, mg_ref):
    x = x_ref[0]
    D = x.shape[-1]
    y = x * lax.rsqrt(jnp.mean(x * x, axis=-1, keepdims=True) + EPS) * g1_ref[...]
    h = y * (1.0 + scale_ref[0]) + shift_ref[0]
    hb = h.astype(BF16)

    lru = _dot(hb, wlru_ref[...])
    lx_ref[0] = lru[:, :D]
    lg_ref[0] = lru[:, D:]
    mg_ref[0] = _dot(hb, wmg_ref[...])

    qkr = _dot(hb, wqkr_ref[...])
    cq = qkr[:, :Q_LORA]
    ckv = qkr[:, Q_LORA:Q_LORA + KV_LORA]
    kr = qkr[:, Q_LORA + KV_LORA:]
    qn = cq * lax.rsqrt(jnp.mean(cq * cq, axis=-1, keepdims=True) + EPS) * gqa_ref[...]
    kvn = ckv * lax.rsqrt(jnp.mean(ckv * ckv, axis=-1, keepdims=True) + EPS) * gkva_ref[...]
    kvb = kvn.astype(BF16)
    q_all = _dot(qn.astype(BF16), wuq_ref[...])
    k_all = _dot(kvb, wuk_ref[...])
    v_ref[0] = _dot(kvb, wuv_ref[...]).astype(BF16)

    ang = pos_ref[0].astype(F32) * freq_ref[...]
    cos_l = jnp.cos(ang)
    sin_l = jnp.sin(ang) * sign_ref[...]
    lane = lax.broadcasted_iota(jnp.int32, ang.shape, 1)
    inv_n = 1.0 / QK_HEAD
    for hh in range(MLA_HEADS):
        sl = slice(hh * LANES, (hh + 1) * LANES)
        qh = q_all[:, sl]
        qh = qh * lax.rsqrt(jnp.sum(qh * qh, axis=-1, keepdims=True) * inv_n + EPS) * gqn_ref[...]
        q_ref[0, :, sl] = _rope(qh, cos_l, sin_l, lane).astype(BF16)
        kh = k_all[:, sl] + kr
        kh = kh * lax.rsqrt(jnp.sum(kh * kh, axis=-1, keepdims=True) * inv_n + EPS) * gkn_ref[...]
        k_ref[0, :, sl] = _rope(kh, cos_l, sin_l, lane).astype(BF16)


def _in_proj(x, pos3, shift1, scale1, g1, wqkr, wlru, wmg, gqa, wuq, gkva, wuk, wuv, gqn, gkn, freq, sign):
    B, S, D = x.shape
    tm = TM_PROJ
    HP = MLA_HEADS * LANES
    tok = lambda w: pl.BlockSpec((1, tm, w), lambda b, s: (b, s, 0))
    per_b = pl.BlockSpec((1, 1, D), lambda b, s: (b, 0, 0))
    full = lambda a: pl.BlockSpec(a.shape, lambda b, s: (0,) * a.ndim)
    return pl.pallas_call(
        _in_proj_body,
        grid=(B, S // tm),
        in_specs=[tok(D), tok(1), per_b, per_b, full(g1), full(wqkr), full(wlru), full(wmg),
                  full(gqa), full(wuq), full(gkva), full(wuk), full(wuv), full(gqn), full(gkn),
                  full(freq), full(sign)],
        out_specs=[tok(HP), tok(HP), tok(HP), tok(D), tok(D), tok(2 * D)],
        out_shape=[jax.ShapeDtypeStruct((B, S, HP), BF16)] * 3
        + [jax.ShapeDtypeStruct((B, S, D), F32)] * 2 + [jax.ShapeDtypeStruct((B, S, 2 * D), F32)],
        compiler_params=_cparams(("parallel", "parallel")),
        name="in_proj",
    )(x, pos3, shift1, scale1, g1, wqkr, wlru, wmg, gqa, wuq, gkva, wuk, wuv, gqn, gkn, freq, sign)


def _attn_body(q_ref, k_ref, v_ref, o_ref):
    qi = pl.program_id(2)
    q = q_ref[0]
    scale = QK_HEAD ** -0.5
    neg = float(jnp.finfo(jnp.float32).min)

    def step(j, carry, masked):
        m, l, acc = carry
        off = pl.multiple_of(j * TQ, TQ)
        kj = k_ref[0, pl.ds(off, TQ), :]
        vj = v_ref[0, pl.ds(off, TQ), :]
        s = _dot_nt(q, kj) * scale
        if masked:
            shift = CHUNK.bit_length() - 1
            qc = jnp.right_shift(lax.broadcasted_iota(jnp.int32, s.shape, 0), shift)
            kc = jnp.right_shift(lax.broadcasted_iota(jnp.int32, s.shape, 1), shift)
            s = jnp.where(kc <= qc, s, neg)
        m_new = jnp.maximum(m, jnp.max(s, axis=-1, keepdims=True))
        p = jnp.exp(s - m_new)
        alpha = jnp.exp(m - m_new)
        l = alpha * l + jnp.sum(p, axis=-1, keepdims=True)
        acc = alpha * acc + _dot(p.astype(BF16), vj)
        return m_new, l, acc

    init = (jnp.full((TQ, 1), neg, F32), jnp.zeros((TQ, 1), F32), jnp.zeros((TQ, LANES), F32))
    carry = lax.fori_loop(0, qi, lambda j, c: step(j, c, False), init)
    m, l, acc = step(qi, carry, True)
    o_ref[0] = (acc / l).astype(BF16)


def _attention(q, k, v):
    B, S, HP = q.shape
    H = HP // LANES
    return pl.pallas_call(
        _attn_body,
        grid=(B, H, S // TQ),
        in_specs=[pl.BlockSpec((1, TQ, LANES), lambda b, h, i: (b, i, h)),
                  pl.BlockSpec((1, S, LANES), lambda b, h, i: (b, 0, h)),
                  pl.BlockSpec((1, S, LANES), lambda b, h, i: (b, 0, h))],
        out_specs=pl.BlockSpec((1, TQ, LANES), lambda b, h, i: (b, i, h)),
        out_shape=jax.ShapeDtypeStruct((B, S, HP), BF16),
        compiler_params=_cparams(("parallel", "parallel", "arbitrary")),
        name="attn",
    )(q, k, v)


def _lru_body(lx_ref, lg_ref, cw_ref, cb_ref, wg_ref, brg_ref, big_ref, lam_ref, rec_ref, prev_ref, h_ref):
    si = pl.program_id(1)

    @pl.when(si == 0)
    def _():
        prev_ref[...] = jnp.zeros_like(prev_ref)
        h_ref[...] = jnp.zeros_like(h_ref)

    x = lx_ref[0]
    ts, C = x.shape
    prev8 = prev_ref[...]
    row8 = lax.broadcasted_iota(jnp.int32, prev8.shape, 0)
    xc = x * cw_ref[CONV_WIDTH - 1:CONV_WIDTH, :] + cb_ref[...]
    for d in range(1, CONV_WIDTH):
        xr = pltpu.roll(x, d, 0)
        first = jnp.where(row8 < d, pltpu.roll(prev8, d, 0), xr[:SUBLANES])
        xs = jnp.concatenate([first, xr[SUBLANES:]], axis=0)
        xc = xc + xs * cw_ref[CONV_WIDTH - 1 - d:CONV_WIDTH - d, :]
    prev_ref[...] = x[ts - SUBLANES:]

    xcb = xc.astype(BF16)
    rowmod = jnp.bitwise_and(lax.broadcasted_iota(jnp.int32, (ts, LANES), 0), SUBLANES - 1)
    for blk in range(LRU_BLOCKS):
        sl = slice(blk * LANES, (blk + 1) * LANES)
        xb = xc[:, sl]
        g = _dot(xcb[:, sl], wg_ref[blk])
        r = jax.nn.sigmoid(g[:, :LANES] + brg_ref[:, sl])
        i = jax.nn.sigmoid(g[:, LANES:] + big_ref[:, sl])
        nl = -lam_ref[:, sl]
        softplus = jnp.maximum(nl, 0.0) + jnp.log(1.0 + jnp.exp(-jnp.abs(nl)))
        log_a = (-LRU_C * r) * softplus
        a = jnp.exp(log_a)
        b = jnp.sqrt(1.0 - a * a) * (i * xb)
        for d in (1, 2, 4):
            valid = rowmod >= d
            b = jnp.where(valid, a * pltpu.roll(b, d, 0) + b, b)
            a = jnp.where(valid, a * pltpu.roll(a, d, 0), a)
        carry = h_ref[:, sl]
        hs = []
        for gi in range(ts // SUBLANES):
            rs = slice(gi * SUBLANES, (gi + 1) * SUBLANES)
            hg = a[rs] * carry + b[rs]
            carry = hg[SUBLANES - 1:SUBLANES]
            hs.append(hg)
        h_ref[:, sl] = carry
        hfull = jnp.concatenate(hs, axis=0)
        rec_ref[0, :, sl] = (hfull * jax.nn.gelu(lg_ref[0, :, sl])).astype(BF16)


def _lru(lx, lg, cw, cb, wg, brg, big, lam):
    B, S, C = lx.shape
    ts = TS_LRU
    tok = pl.BlockSpec((1, ts, C), lambda b, s: (b, s, 0))
    full = lambda a: pl.BlockSpec(a.shape, lambda b, s: (0,) * a.ndim)
    return pl.pallas_call(
        _lru_body,
        grid=(B, S // ts),
        in_specs=[tok, tok, full(cw), full(cb), full(wg), full(brg), full(big), full(lam)],
        out_specs=tok,
        out_shape=jax.ShapeDtypeStruct((B, S, C), BF16),
        scratch_shapes=[pltpu.VMEM((SUBLANES, C), F32), pltpu.VMEM((1, C), F32)],
        compiler_params=_cparams(("parallel", "arbitrary")),
        name="lru",
    )(lx, lg, cw, cb, wg, brg, big, lam)


def _merge_body(o_ref, rec_ref, mg_ref, x_ref, gate1_ref, shift2_ref, scale2_ref, g2_ref,
                woa_ref, wob_ref, wout_ref, wq_ref, keys_ref, x1_ref, h2_ref, sc_ref):
    D = x_ref.shape[-1]
    ya = _dot(o_ref[0], woa_ref[...])
    yb = _dot(rec_ref[0], wob_ref[...])
    mg = mg_ref[0]
    y = jax.nn.sigmoid(mg[:, :D]) * ya + jax.nn.sigmoid(mg[:, D:]) * yb
    x1 = x_ref[0] + gate1_ref[0] * _dot(y.astype(BF16), wout_ref[...])
    x1_ref[0] = x1
    h2 = x1 * lax.rsqrt(jnp.mean(x1 * x1, axis=-1, keepdims=True) + EPS) * g2_ref[...]
    h2 = h2 * (1.0 + scale2_ref[0]) + shift2_ref[0]
    h2b = h2.astype(BF16)
    h2_ref[0] = h2b
    qp = _dot(h2b, wq_ref[...]).astype(BF16)
    for g in range(keys_ref.shape[0]):
        sc_ref[g * N_KEYS:(g + 1) * N_KEYS, :] = _dot_nt(keys_ref[g], qp[:, g * LANES:(g + 1) * LANES])


def _merge(o, rec, mg, x, gate1, shift2, scale2, g2, woa, wob, wout, wq, keys):
    B, S, D = x.shape
    tm = TM_PROJ
    ns = S // tm
    NG = keys.shape[0]
    tok = lambda w: pl.BlockSpec((1, tm, w), lambda b, s: (b, s, 0))
    per_b = pl.BlockSpec((1, 1, D), lambda b, s: (b, 0, 0))
    full = lambda a: pl.BlockSpec(a.shape, lambda b, s: (0,) * a.ndim)
    return pl.pallas_call(
        _merge_body,
        grid=(B, ns),
        in_specs=[tok(o.shape[-1]), tok(D), tok(2 * D), tok(D), per_b, per_b, per_b, full(g2),
                  full(woa), full(wob), full(wout), full(wq), full(keys)],
        out_specs=[tok(D), tok(D), pl.BlockSpec((NG * N_KEYS, tm), lambda b, s: (0, b * ns + s))],
        out_shape=[jax.ShapeDtypeStruct((B, S, D), F32), jax.ShapeDtypeStruct((B, S, D), BF16),
                   jax.ShapeDtypeStruct((NG * N_KEYS, B * S), F32)],
        compiler_params=_cparams(("parallel", "parallel")),
        name="merge",
    )(o, rec, mg, x, gate1, shift2, scale2, g2, woa, wob, wout, wq, keys)


def _top_values(arr, n):
    neg_inf = float("-inf")
    tops = []
    for _ in range(n):
        m = jnp.max(arr, axis=0, keepdims=True)
        tops.append(m)
        arr = jnp.where(arr == m, neg_inf, arr)
    return tops


def _select_body(sc_ref, e1_ref, e2_ref, thr_ref):
    K = PEER_TOPK
    neg_inf = float("-inf")
    s1 = sc_ref[:N_KEYS, :]
    s2 = sc_ref[N_KEYS:, :]
    t1 = _top_values(s1, K)
    t2 = _top_values(s2, K)
    t2a = jnp.concatenate(t2[:SUBLANES], axis=0)
    t2b = jnp.concatenate(t2[SUBLANES:], axis=0)
    t1b = jnp.concatenate(t1[SUBLANES:], axis=0)
    row = lax.broadcasted_iota(jnp.int32, t2a.shape, 0)
    cands = [t1[0] + t2a, t1[0] + t2b]
    for a in range(1, SUBLANES):
        cands.append(jnp.where(row < K // (a + 1), t1[a] + t2a, neg_inf))
    cands.append(t1b + t2[0])
    cand = jnp.concatenate(cands, axis=0)
    best = _top_values(cand, K)
    z = jnp.ones_like(best[0])
    for r in range(1, K):
        z = z + jnp.exp(best[r] - best[0])
    thr_ref[0] = best[K - 1]
    e1_ref[...] = jnp.exp(s1 - t1[0])
    e2_ref[...] = jnp.exp(s2 - t2[0]) / z


def _select(sc):
    R, T = sc.shape
    H = R // (2 * N_KEYS)
    tl = TL_SEL
    return pl.pallas_call(
        _select_body,
        grid=(T // tl, H),
        in_specs=[pl.BlockSpec((2 * N_KEYS, tl), lambda t, h: (h, t))],
        out_specs=[pl.BlockSpec((N_KEYS, tl), lambda t, h: (h, t)),
                   pl.BlockSpec((N_KEYS, tl), lambda t, h: (h, t)),
                   pl.BlockSpec((1, 1, tl), lambda t, h: (h, 0, t))],
        out_shape=[jax.ShapeDtypeStruct((H * N_KEYS, T), F32), jax.ShapeDtypeStruct((H * N_KEYS, T), F32),
                   jax.ShapeDtypeStruct((H, 1, T), F32)],
        compiler_params=_cparams(("parallel", "parallel")),
        name="select",
    )(sc)


def _peer_body(h2_ref, u_ref, vt_ref, sc_ref, e1_ref, e2_ref, thr_ref, x1_ref, gate2_ref,
               out_ref, acc_ref, at_ref, w_ref):
    e = pl.program_id(1)
    tm = h2_ref.shape[0]
    n_i = u_ref.shape[0] // N_KEYS

    @pl.when(e == 0)
    def _():
        acc_ref[...] = jnp.zeros_like(acc_ref)

    at_ref[...] = _dot_nt(u_ref[...], h2_ref[...])

    assert n_i == SUBLANES
    i0 = pl.multiple_of(e * n_i, SUBLANES)

    def per_lane_block(lb, _):
        ls = pl.ds(pl.multiple_of(lb * LANES, LANES), LANES)
        for ii in range(n_i):
            rs = slice(ii * N_KEYS, (ii + 1) * N_KEYS)
            coef = jnp.zeros((N_KEYS, LANES), F32)
            for h in range(PEER_HEADS):
                s1row = sc_ref[pl.ds(h * 2 * N_KEYS + i0, SUBLANES), ls][ii:ii + 1]
                e1row = e1_ref[pl.ds(h * N_KEYS + i0, SUBLANES), ls][ii:ii + 1]
                s2 = sc_ref[(2 * h + 1) * N_KEYS:(2 * h + 2) * N_KEYS, ls]
                e2 = e2_ref[h * N_KEYS:(h + 1) * N_KEYS, ls]
                sel = (s1row + s2) >= thr_ref[h, :, ls]
                coef = coef + jnp.where(sel, e2, 0.0) * e1row
            act = jax.nn.gelu(at_ref[rs, ls])
            w_ref[rs, ls] = (coef * act).astype(BF16)
        return 0

    lax.fori_loop(0, tm // LANES, per_lane_block, 0)
    acc_ref[...] += _dot(vt_ref[...], w_ref[...])

    @pl.when(e == pl.num_programs(1) - 1)
    def _():
        out_ref[...] = x1_ref[...] + gate2_ref[0] * acc_ref[...].T


def _peer(h2, u, vt, sc, e1, e2, thr, x1, gate2, seq_len):
    T, D = h2.shape
    NE = u.shape[0]
    tm, te = TM_PEER, TE_PEER
    tiles_per_seq = seq_len // tm
    return pl.pallas_call(
        _peer_body,
        grid=(T // tm, NE // te),
        in_specs=[pl.BlockSpec((tm, D), lambda t, e: (t, 0)),
                  pl.BlockSpec((te, D), lambda t, e: (e, 0)),
                  pl.BlockSpec((D, te), lambda t, e: (0, e)),
                  pl.BlockSpec((sc.shape[0], tm), lambda t, e: (0, t)),
                  pl.BlockSpec((e1.shape[0], tm), lambda t, e: (0, t)),
                  pl.BlockSpec((e2.shape[0], tm), lambda t, e: (0, t)),
                  pl.BlockSpec((thr.shape[0], 1, tm), lambda t, e: (0, 0, t)),
                  pl.BlockSpec((tm, D), lambda t, e: (t, 0)),
                  pl.BlockSpec((1, 1, D), lambda t, e: (t // tiles_per_seq, 0, 0))],
        out_specs=pl.BlockSpec((tm, D), lambda t, e: (t, 0)),
        out_shape=jax.ShapeDtypeStruct((T, D), F32),
        scratch_shapes=[pltpu.VMEM((D, tm), F32), pltpu.VMEM((te, tm), F32), pltpu.VMEM((te, tm), BF16)],
        compiler_params=_cparams(("parallel", "arbitrary")),
        name="peer",
    )(h2, u, vt, sc, e1, e2, thr, x1, gate2)


def _pad_heads(w, real):
    K = w.shape[0]
    w = w.reshape(K, -1, real)
    return jnp.pad(w, ((0, 0), (0, 0), (0, LANES - real))).reshape(K, -1)


def _layer(x, cs_mod, positions, p, l):
    B, S, D = x.shape
    H = MLA_HEADS
    shift1, scale1, gate1, shift2, scale2, gate2 = [cs_mod[:, j].reshape(B, 1, D) for j in range(6)]
    w_in = p["w_in"][l]
    c0, c1, c2 = Q_LORA, Q_LORA + KV_LORA, Q_LORA + KV_LORA + QK_ROPE
    w_rope = jnp.pad(w_in[:, c1:c2], ((0, 0), (QK_NOPE, LANES - QK_HEAD)))
    wqkr = jnp.concatenate([w_in[:, :c1], w_rope], axis=1).astype(BF16)
    wlru = w_in[:, c2:c2 + 2 * D].astype(BF16)
    wmg = w_in[:, c2 + 2 * D:].astype(BF16)
    wuq = _pad_heads(p["w_uq"][l], QK_HEAD).astype(BF16)
    wukv = p["w_ukv"][l].reshape(KV_LORA, H, QK_NOPE + V_HEAD)
    wuk = jnp.pad(wukv[:, :, :QK_NOPE], ((0, 0), (0, 0), (0, LANES - QK_NOPE))).reshape(KV_LORA, H * LANES).astype(BF16)
    wuv = jnp.pad(wukv[:, :, QK_NOPE:], ((0, 0), (0, 0), (0, LANES - V_HEAD))).reshape(KV_LORA, H * LANES).astype(BF16)
    gqn = jnp.pad(p["q_norm_g"][l], (0, LANES - QK_HEAD)).reshape(1, LANES)
    gkn = jnp.pad(p["k_norm_g"][l], (0, LANES - QK_HEAD)).reshape(1, LANES)
    inv_freq = 1.0 / (ROPE_THETA ** (jnp.arange(0, QK_ROPE, 2, dtype=F32) / QK_ROPE))
    half = QK_ROPE // 2
    zeros_lo = jnp.zeros((QK_NOPE,), F32)
    zeros_hi = jnp.zeros((LANES - QK_HEAD,), F32)
    freq = jnp.concatenate([zeros_lo, inv_freq, inv_freq, zeros_hi]).reshape(1, LANES)
    sign = jnp.concatenate([zeros_lo, -jnp.ones((half,), F32), jnp.ones((half,), F32), zeros_hi]).reshape(1, LANES)

    q, k, v, lx, lg, mg = _in_proj(
        x, positions.reshape(B, S, 1), shift1, scale1, p["norm1_g"][l].reshape(1, D), wqkr, wlru, wmg,
        p["q_a_norm_g"][l].reshape(1, Q_LORA), wuq, p["kv_a_norm_g"][l].reshape(1, KV_LORA), wuk, wuv,
        gqn, gkn, freq, sign)
    o = _attention(q, k, v)

    C = p["conv_b"].shape[-1]
    wg = jnp.concatenate([p["w_rg"][l], p["w_ig"][l]], axis=-1).astype(BF16)
    rec = _lru(lx, lg, p["conv_w"][l], p["conv_b"][l].reshape(1, C), wg, p["b_rg"][l].reshape(1, C),
               p["b_ig"][l].reshape(1, C), p["lru_lambda"][l].reshape(1, C))

    woa = jnp.pad(p["w_o_attn"][l].reshape(H, V_HEAD, D), ((0, 0), (0, LANES - V_HEAD), (0, 0)))
    woa = woa.reshape(H * LANES, D).astype(BF16)
    keys = p["sub_keys"][l].reshape(PEER_HEADS * 2, N_KEYS, -1).astype(BF16)
    x1, h2, sc = _merge(o, rec, mg, x, gate1, shift2, scale2, p["norm2_g"][l].reshape(1, D), woa,
                        p["w_o_lru"][l].astype(BF16), p["w_out"][l].astype(BF16), p["w_query"][l].astype(BF16), keys)
    e1, e2, thr = _select(sc)
    out = _peer(h2.reshape(B * S, D), p["expert_u"][l].astype(BF16), p["expert_v"][l].T.astype(BF16),
                sc, e1, e2, thr, x1.reshape(B * S, D), gate2, S)
    return out.reshape(B, S, D)


def kernel(x, c, positions, w_ada, b_ada, norm1_g, w_in, q_a_norm_g, w_uq, kv_a_norm_g, w_ukv, q_norm_g, k_norm_g, w_o_attn, conv_w, conv_b, w_rg, b_rg, w_ig, b_ig, lru_lambda, w_o_lru, w_out, norm2_g, w_query, sub_keys, expert_u, expert_v):
    p = dict(w_in=w_in, norm1_g=norm1_g, q_a_norm_g=q_a_norm_g, w_uq=w_uq, kv_a_norm_g=kv_a_norm_g, w_ukv=w_ukv,
             q_norm_g=q_norm_g, k_norm_g=k_norm_g, w_o_attn=w_o_attn, conv_w=conv_w, conv_b=conv_b, w_rg=w_rg,
             b_rg=b_rg, w_ig=w_ig, b_ig=b_ig, lru_lambda=lru_lambda, w_o_lru=w_o_lru, w_out=w_out,
             norm2_g=norm2_g, w_query=w_query, sub_keys=sub_keys, expert_u=expert_u, expert_v=expert_v)
    B, D = c.shape
    for l in range(w_ada.shape[0]):
        mod = _mod(c, w_ada[l], b_ada[l]).reshape(B, 6, D)
        x = _layer(x, mod, positions, p, l)
    return x
```

```python
import functools

import jax
import jax.numpy as jnp
from jax import lax
from jax.experimental import pallas as pl
from jax.experimental.pallas import tpu as pltpu

F32 = jnp.float32
BF16 = jnp.bfloat16

EPS = 1e-6
LANES = 128
SUBLANES = 8
VMEM_LIMIT = 56 * 1024 * 1024

CHUNK = 64
MLA_HEADS = 8
QK_NOPE = 64
QK_ROPE = 32
QK_HEAD = QK_NOPE + QK_ROPE
V_HEAD = 64
Q_LORA = 256
KV_LORA = 128
ROPE_THETA = 10000.0
LRU_BLOCKS = 8
CONV_WIDTH = 4
LRU_C = 8.0
PEER_HEADS = 8
N_KEYS = 128
PEER_TOPK = 16

TM_PROJ = 256
TQ = 256
TS_LRU = 256
TL_SEL = 128
TM_PEER = 512
TE_PEER = 1024


def _cparams(sem):
    return pltpu.CompilerParams(dimension_semantics=sem, vmem_limit_bytes=VMEM_LIMIT)


def _dot(a, b):
    return jnp.dot(a, b, preferred_element_type=F32)


def _dot_nt(a, b):
    return lax.dot_general(a, b, (((1,), (1,)), ((), ())), preferred_element_type=F32)


def _mod_body(c_ref, w_ref, b_ref, o_ref):
    c = c_ref[...]
    cs = c * jax.nn.sigmoid(c)
    o_ref[...] = _dot(cs.astype(BF16), w_ref[...].astype(BF16)) + b_ref[...]


def _mod(c, w_ada, b_ada):
    B, D = c.shape
    N = w_ada.shape[1]
    tn = 1024
    return pl.pallas_call(
        _mod_body,
        grid=(N // tn,),
        in_specs=[pl.BlockSpec((B, D), lambda n: (0, 0)),
                  pl.BlockSpec((D, tn), lambda n: (0, n)),
                  pl.BlockSpec((1, tn), lambda n: (0, n))],
        out_specs=pl.BlockSpec((B, tn), lambda n: (0, n)),
        out_shape=jax.ShapeDtypeStruct((B, N), F32),
        compiler_params=_cparams(("parallel",)),
        name="mod",
    )(c, w_ada, b_ada.reshape(1, N))


def _rope(t, cos_l, sin_l, lane):
    rot = jnp.where(lane < QK_NOPE + QK_ROPE // 2,
                    pltpu.roll(t, LANES - QK_ROPE // 2, 1), pltpu.roll(t, QK_ROPE // 2, 1))
    return t * cos_l + rot * sin_l


def _in_proj_body(x_ref, pos_ref, shift_ref, scale_ref, g1_ref, wqkr_ref, wlru_ref, wmg_ref,
                  gqa_ref, wuq_ref, gkva_ref, wuk_ref, wuv_ref, gqn_ref, gkn_ref, freq_ref, sign_ref,
                  q_ref, k_ref, v_ref, lx_ref, lg_ref, mg_ref):
    x = x_ref[0]
    D = x.shape[-1]
    y = x * lax.rsqrt(jnp.mean(x * x, axis=-1, keepdims=True) + EPS) * g1_ref[...]
    h = y * (1.0 + scale_ref[0]) + shift_ref[0]
    hb = h.astype(BF16)

    lru = _dot(hb, wlru_ref[...])
    lx_ref[0] = lru[:, :D]
    lg_ref[0] = lru[:, D:]
    mg_ref[0] = _dot(hb, wmg_ref[...])

    qkr = _dot(hb, wqkr_ref[...])
    cq = qkr[:, :Q_LORA]
    ckv = qkr[:, Q_LORA:Q_LORA + KV_LORA]
    kr = qkr[:, Q_LORA + KV_LORA:]
    qn = cq * lax.rsqrt(jnp.mean(cq * cq, axis=-1, keepdims=True) + EPS) * gqa_ref[...]
    kvn = ckv * lax.rsqrt(jnp.mean(ckv * ckv, axis=-1, keepdims=True) + EPS) * gkva_ref[...]
    kvb = kvn.astype(BF16)
    q_all = _dot(qn.astype(BF16), wuq_ref[...])
    k_all = _dot(kvb, wuk_ref[...])
    v_ref[0] = _dot(kvb, wuv_ref[...]).astype(BF16)

    ang = pos_ref[0].astype(F32) * freq_ref[...]
    cos_l = jnp.cos(ang)
    sin_l = jnp.sin(ang) * sign_ref[...]
    lane = lax.broadcasted_iota(jnp.int32, ang.shape, 1)
    inv_n = 1.0 / QK_HEAD
    for hh in range(MLA_HEADS):
        sl = slice(hh * LANES, (hh + 1) * LANES)
        qh = q_all[:, sl]
        qh = qh * lax.rsqrt(jnp.sum(qh * qh, axis=-1, keepdims=True) * inv_n + EPS) * gqn_ref[...]
        q_ref[0, :, sl] = _rope(qh, cos_l, sin_l, lane).astype(BF16)
        kh = k_all[:, sl] + kr
        kh = kh * lax.rsqrt(jnp.sum(kh * kh, axis=-1, keepdims=True) * inv_n + EPS) * gkn_ref[...]
        k_ref[0, :, sl] = _rope(kh, cos_l, sin_l, lane).astype(BF16)


def _in_proj(x, pos3, shift1, scale1, g1, wqkr, wlru, wmg, gqa, wuq, gkva, wuk, wuv, gqn, gkn, freq, sign):
    B, S, D = x.shape
    tm = TM_PROJ
    HP = MLA_HEADS * LANES
    tok = lambda w: pl.BlockSpec((1, tm, w), lambda b, s: (b, s, 0))
    per_b = pl.BlockSpec((1, 1, D), lambda b, s: (b, 0, 0))
    full = lambda a: pl.BlockSpec(a.shape, lambda b, s: (0,) * a.ndim)
    return pl.pallas_call(
        _in_proj_body,
        grid=(B, S // tm),
        in_specs=[tok(D), tok(1), per_b, per_b, full(g1), full(wqkr), full(wlru), full(wmg),
                  full(gqa), full(wuq), full(gkva), full(wuk), full(wuv), full(gqn), full(gkn),
                  full(freq), full(sign)],
        out_specs=[tok(HP), tok(HP), tok(HP), tok(D), tok(D), tok(2 * D)],
        out_shape=[jax.ShapeDtypeStruct((B, S, HP), BF16)] * 3
        + [jax.ShapeDtypeStruct((B, S, D), F32)] * 2 + [jax.ShapeDtypeStruct((B, S, 2 * D), F32)],
        compiler_params=_cparams(("parallel", "parallel")),
        name="in_proj",
    )(x, pos3, shift1, scale1, g1, wqkr, wlru, wmg, gqa, wuq, gkva, wuk, wuv, gqn, gkn, freq, sign)


def _attn_body(q_ref, k_ref, v_ref, o_ref):
    qi = pl.program_id(2)
    q = q_ref[0]
    neg = float(jnp.finfo(jnp.float32).min)
    shift = CHUNK.bit_length() - 1
    qc = jnp.right_shift(lax.broadcasted_iota(jnp.int32, (TQ, TQ), 0), shift)
    kc = jnp.right_shift(lax.broadcasted_iota(jnp.int32, (TQ, TQ), 1), shift)
    allowed = kc <= qc

    for nk in range(1, k_ref.shape[1] // TQ + 1):
        @pl.when(qi == nk - 1)
        def _(nk=nk):
            n_keys = nk * TQ
            s = _dot_nt(q, k_ref[0, :n_keys, :])
            diag = jnp.where(allowed, s[:, n_keys - TQ:], neg)
            s = diag if nk == 1 else jnp.concatenate([s[:, :n_keys - TQ], diag], axis=1)
            p = jnp.exp(s - jnp.max(s, axis=-1, keepdims=True))
            l = jnp.sum(p, axis=-1, keepdims=True)
            o_ref[0] = (_dot(p.astype(BF16), v_ref[0, :n_keys, :]) / l).astype(BF16)


def _attention(q, k, v):
    B, S, HP = q.shape
    H = HP // LANES
    return pl.pallas_call(
        _attn_body,
        grid=(B, H, S // TQ),
        in_specs=[pl.BlockSpec((1, TQ, LANES), lambda b, h, i: (b, i, h)),
                  pl.BlockSpec((1, S, LANES), lambda b, h, i: (b, 0, h)),
                  pl.BlockSpec((1, S, LANES), lambda b, h, i: (b, 0, h))],
        out_specs=pl.BlockSpec((1, TQ, LANES), lambda b, h, i: (b, i, h)),
        out_shape=jax.ShapeDtypeStruct((B, S, HP), BF16),
        compiler_params=_cparams(("parallel", "parallel", "arbitrary")),
        name="attn",
    )(q, k, v)


def _lru_body(lx_ref, lg_ref, cw_ref, cb_ref, wg_ref, brg_ref, big_ref, lam_ref, rec_ref, prev_ref, h_ref):
    si = pl.program_id(1)

    @pl.when(si == 0)
    def _():
        prev_ref[...] = jnp.zeros_like(prev_ref)
        h_ref[...] = jnp.zeros_like(h_ref)

    x = lx_ref[0]
    ts, C = x.shape
    prev8 = prev_ref[...]
    row8 = lax.broadcasted_iota(jnp.int32, prev8.shape, 0)
    xc = x * cw_ref[CONV_WIDTH - 1:CONV_WIDTH, :] + cb_ref[...]
    for d in range(1, CONV_WIDTH):
        xr = pltpu.roll(x, d, 0)
        first = jnp.where(row8 < d, pltpu.roll(prev8, d, 0), xr[:SUBLANES])
        xs = jnp.concatenate([first, xr[SUBLANES:]], axis=0)
        xc = xc + xs * cw_ref[CONV_WIDTH - 1 - d:CONV_WIDTH - d, :]
    prev_ref[...] = x[ts - SUBLANES:]

    xcb = xc.astype(BF16)
    rowmod = jnp.bitwise_and(lax.broadcasted_iota(jnp.int32, (ts, LANES), 0), SUBLANES - 1)
    for blk in range(LRU_BLOCKS):
        sl = slice(blk * LANES, (blk + 1) * LANES)
        xb = xc[:, sl]
        g = _dot(xcb[:, sl], wg_ref[blk])
        r = jax.nn.sigmoid(g[:, :LANES] + brg_ref[:, sl])
        i = jax.nn.sigmoid(g[:, LANES:] + big_ref[:, sl])
        nl = -lam_ref[:, sl]
        softplus = jnp.maximum(nl, 0.0) + jnp.log(1.0 + jnp.exp(-jnp.abs(nl)))
        log_a = (-LRU_C * r) * softplus
        a = jnp.exp(log_a)
        b = jnp.sqrt(1.0 - a * a) * (i * xb)
        for d in (1, 2, 4):
            valid = rowmod >= d
            b = jnp.where(valid, a * pltpu.roll(b, d, 0) + b, b)
            a = jnp.where(valid, a * pltpu.roll(a, d, 0), a)
        carry = h_ref[:, sl]
        hs = []
        for gi in range(ts // SUBLANES):
            rs = slice(gi * SUBLANES, (gi + 1) * SUBLANES)
            hg = a[rs] * carry + b[rs]
            carry = hg[SUBLANES - 1:SUBLANES]
            hs.append(hg)
        h_ref[:, sl] = carry
        hfull = jnp.concatenate(hs, axis=0)
        rec_ref[0, :, sl] = (hfull * jax.nn.gelu(lg_ref[0, :, sl])).astype(BF16)


def _lru(lx, lg, cw, cb, wg, brg, big, lam):
    B, S, C = lx.shape
    ts = TS_LRU
    tok = pl.BlockSpec((1, ts, C), lambda b, s: (b, s, 0))
    full = lambda a: pl.BlockSpec(a.shape, lambda b, s: (0,) * a.ndim)
    return pl.pallas_call(
        _lru_body,
        grid=(B, S // ts),
        in_specs=[tok, tok, full(cw), full(cb), full(wg), full(brg), full(big), full(lam)],
        out_specs=tok,
        out_shape=jax.ShapeDtypeStruct((B, S, C), BF16),
        scratch_shapes=[pltpu.VMEM((SUBLANES, C), F32), pltpu.VMEM((1, C), F32)],
        compiler_params=_cparams(("parallel", "arbitrary")),
        name="lru",
    )(lx, lg, cw, cb, wg, brg, big, lam)


def _merge_body(o_ref, rec_ref, mg_ref, x_ref, gate1_ref, shift2_ref, scale2_ref, g2_ref,
                woa_ref, wob_ref, wout_ref, wq_ref, keys_ref, x1_ref, h2_ref, sc_ref):
    D = x_ref.shape[-1]
    ya = _dot(o_ref[0], woa_ref[...])
    yb = _dot(rec_ref[0], wob_ref[...])
    mg = mg_ref[0]
    y = jax.nn.sigmoid(mg[:, :D]) * ya + jax.nn.sigmoid(mg[:, D:]) * yb
    x1 = x_ref[0] + gate1_ref[0] * _dot(y.astype(BF16), wout_ref[...])
    x1_ref[0] = x1
    h2 = x1 * lax.rsqrt(jnp.mean(x1 * x1, axis=-1, keepdims=True) + EPS) * g2_ref[...]
    h2 = h2 * (1.0 + scale2_ref[0]) + shift2_ref[0]
    h2b = h2.astype(BF16)
    h2_ref[0] = h2b
    qp = _dot(h2b, wq_ref[...]).astype(BF16)
    for g in range(keys_ref.shape[0]):
        sc_ref[g * N_KEYS:(g + 1) * N_KEYS, :] = _dot_nt(keys_ref[g], qp[:, g * LANES:(g + 1) * LANES])


def _merge(o, rec, mg, x, gate1, shift2, scale2, g2, woa, wob, wout, wq, keys):
    B, S, D = x.shape
    tm = TM_PROJ
    ns = S // tm
    NG = keys.shape[0]
    tok = lambda w: pl.BlockSpec((1, tm, w), lambda b, s: (b, s, 0))
    per_b = pl.BlockSpec((1, 1, D), lambda b, s: (b, 0, 0))
    full = lambda a: pl.BlockSpec(a.shape, lambda b, s: (0,) * a.ndim)
    return pl.pallas_call(
        _merge_body,
        grid=(B, ns),
        in_specs=[tok(o.shape[-1]), tok(D), tok(2 * D), tok(D), per_b, per_b, per_b, full(g2),
                  full(woa), full(wob), full(wout), full(wq), full(keys)],
        out_specs=[tok(D), tok(D), pl.BlockSpec((NG * N_KEYS, tm), lambda b, s: (0, b * ns + s))],
        out_shape=[jax.ShapeDtypeStruct((B, S, D), F32), jax.ShapeDtypeStruct((B, S, D), BF16),
                   jax.ShapeDtypeStruct((NG * N_KEYS, B * S), F32)],
        compiler_params=_cparams(("parallel", "parallel")),
        name="merge",
    )(o, rec, mg, x, gate1, shift2, scale2, g2, woa, wob, wout, wq, keys)


def _sort16_network():
    def merge(lo, hi, r):
        step = r * 2
        if step < hi - lo:
            yield from merge(lo, hi, step)
            yield from merge(lo + r, hi, step)
            yield from [(i, i + r) for i in range(lo + r, hi - r, step)]
        else:
            yield (lo, lo + r)

    def sort(lo, hi):
        if hi - lo >= 1:
            mid = lo + (hi - lo) // 2
            yield from sort(lo, mid)
            yield from sort(mid + 1, hi)
            yield from merge(lo, hi, 1)

    return tuple(sort(0, PEER_TOPK - 1))


_SORT16 = _sort16_network()


def _compare_exchange(v, i, j):
    v[i], v[j] = jnp.maximum(v[i], v[j]), jnp.minimum(v[i], v[j])


def _bitonic_merge16(v):
    d = PEER_TOPK // 2
    while d:
        for i in range(PEER_TOPK):
            if not i & d:
                _compare_exchange(v, i, i + d)
        d //= 2


def _merge_top16(best, other):
    z = [jnp.maximum(best[i], other[PEER_TOPK - 1 - i]) if PEER_TOPK - 1 - i < len(other) else best[i]
         for i in range(PEER_TOPK)]
    _bitonic_merge16(z)
    return z


def _top16_of_128(v):
    v = list(v)
    for i, j in _SORT16:
        _compare_exchange(v, i, j)
    shift = SUBLANES // 2
    while shift:
        v = _merge_top16(v, [pltpu.roll(x, shift, 0) for x in v])
        shift //= 2
    return v


def _dup_bf16_bits(x):
    hi = lax.bitcast_convert_type(x.astype(BF16).astype(F32), jnp.uint32)
    return hi | (hi >> 16)


def _select_body(sc_ref, r2_ref, e2_ref, c1_ref, e1_ref):
    K = PEER_TOPK
    H = PEER_HEADS
    nv = N_KEYS // SUBLANES
    tl = sc_ref.shape[1]
    sub = lax.broadcasted_iota(jnp.int32, (SUBLANES, tl), 0)

    def group(g):
        return [sc_ref[g * N_KEYS + k * SUBLANES:g * N_KEYS + (k + 1) * SUBLANES, :] for k in range(nv)]

    t1 = [_top16_of_128(group(2 * h)) for h in range(H)]
    t2 = [_top16_of_128(group(2 * h + 1)) for h in range(H)]

    def pack(reps):
        out = reps[0]
        for h in range(1, H):
            out = jnp.where(sub == h, reps[h], out)
        return out

    def unpack(x, h):
        return jnp.broadcast_to(x[h:h + 1, :], x.shape)

    t1p = [pack([t1[h][a] for h in range(H)]) for a in range(K)]
    t2p = [pack([t2[h][b] for h in range(H)]) for b in range(K)]
    cand = [[t1p[a] + t2p[b] for b in range(K // (a + 1))] for a in range(K)]
    best = list(cand[0])
    for a in range(1, SUBLANES):
        best = _merge_top16(best, cand[a])
    best = _merge_top16(best, [cand[a][0] for a in range(SUBLANES, K)])
    thr = best[K - 1]
    z = jnp.ones_like(thr)
    for r in range(1, K):
        z = z + jnp.exp(best[r] - best[0])
    inv_z = 1.0 / z
    phi = []
    for b in range(K):
        f = jnp.full_like(thr, float("inf"))
        for a in range(K // (b + 1)):
            f = jnp.where(cand[a][b] >= thr, t1p[a], f)
        phi.append(f)

    for h in range(H):
        phi_h = [unpack(f, h) for f in phi]
        inv_z_h = unpack(inv_z, h)
        c1, e1, r2, e2 = [], [], [], []
        for x in group(2 * h):
            cnt = jnp.zeros_like(x)
            for b in range(K):
                cnt = cnt + jnp.where(x >= phi_h[b], 1.0, 0.0)
            c1.append(_dup_bf16_bits(cnt))
            e1.append(_dup_bf16_bits(jnp.exp(x - t1[h][0])))
        for x in group(2 * h + 1):
            rank = jnp.zeros_like(x)
            for r in range(K):
                rank = rank + jnp.where(t2[h][r] > x, 1.0, 0.0)
            r2.append(rank)
            e2.append(jnp.exp(x - t2[h][0]) * inv_z_h)
        rows = slice(h * N_KEYS, (h + 1) * N_KEYS)
        c1_ref[rows, :] = jnp.concatenate(c1, axis=0)
        e1_ref[rows, :] = jnp.concatenate(e1, axis=0)
        r2_ref[rows, :] = jnp.concatenate(r2, axis=0).astype(BF16)
        e2_ref[rows, :] = jnp.concatenate(e2, axis=0).astype(BF16)


def _select(sc):
    R, T = sc.shape
    tl = TL_SEL
    half = R // 2
    blk = pl.BlockSpec((half, tl), lambda t: (0, t))
    return pl.pallas_call(
        _select_body,
        grid=(T // tl,),
        in_specs=[pl.BlockSpec((R, tl), lambda t: (0, t))],
        out_specs=[blk, blk, blk, blk],
        out_shape=[jax.ShapeDtypeStruct((half, T), BF16), jax.ShapeDtypeStruct((half, T), BF16),
                   jax.ShapeDtypeStruct((half, T), jnp.uint32), jax.ShapeDtypeStruct((half, T), jnp.uint32)],
        compiler_params=_cparams(("parallel",)),
        name="select",
    )(sc)


def _peer_body(h2_ref, u_ref, vt_ref, r2_ref, e2_ref, c1_ref, e1_ref, x1_ref, gate2_ref,
               out_ref, acc_ref, at_ref, w_ref):
    e = pl.program_id(1)
    tm = h2_ref.shape[0]
    n_i = u_ref.shape[0] // N_KEYS
    pack_rows = N_KEYS // 2

    @pl.when(e == 0)
    def _():
        acc_ref[...] = jnp.zeros_like(acc_ref)

    at_ref[...] = _dot_nt(u_ref[...], h2_ref[...]).astype(BF16)

    assert n_i == SUBLANES
    i0 = pl.multiple_of(e * n_i, SUBLANES)

    def row_bf16(ref, h, ii, ls):
        row = ref[pl.ds(h * N_KEYS + i0, SUBLANES), ls][ii:ii + 1]
        return pltpu.bitcast(jnp.broadcast_to(row, (pack_rows, LANES)), BF16)

    def per_lane_block(lb, _):
        ls = pl.ds(pl.multiple_of(lb * LANES, LANES), LANES)
        for ii in range(n_i):
            rs = slice(ii * N_KEYS, (ii + 1) * N_KEYS)
            coef = jnp.zeros((N_KEYS, LANES), BF16)
            for h in range(PEER_HEADS):
                hs = slice(h * N_KEYS, (h + 1) * N_KEYS)
                e2 = e2_ref[hs, ls]
                sel = r2_ref[hs, ls] < row_bf16(c1_ref, h, ii, ls)
                coef = coef + jnp.where(sel, e2, jnp.zeros_like(e2)) * row_bf16(e1_ref, h, ii, ls)
            w_ref[rs, ls] = coef * jax.nn.gelu(at_ref[rs, ls])
        return 0

    lax.fori_loop(0, tm // LANES, per_lane_block, 0)
    acc_ref[...] += _dot(vt_ref[...], w_ref[...])

    @pl.when(e == pl.num_programs(1) - 1)
    def _():
        out_ref[...] = x1_ref[...] + gate2_ref[0] * acc_ref[...].T


def _peer(h2, u, vt, r2, e2, c1, e1, x1, gate2, seq_len):
    T, D = h2.shape
    NE = u.shape[0]
    tm, te = TM_PEER, TE_PEER
    tiles_per_seq = seq_len // tm
    table = pl.BlockSpec((r2.shape[0], tm), lambda t, e: (0, t))
    return pl.pallas_call(
        _peer_body,
        grid=(T // tm, NE // te),
        in_specs=[pl.BlockSpec((tm, D), lambda t, e: (t, 0)),
                  pl.BlockSpec((te, D), lambda t, e: (e, 0)),
                  pl.BlockSpec((D, te), lambda t, e: (0, e)),
                  table, table, table, table,
                  pl.BlockSpec((tm, D), lambda t, e: (t, 0)),
                  pl.BlockSpec((1, 1, D), lambda t, e: (t // tiles_per_seq, 0, 0))],
        out_specs=pl.BlockSpec((tm, D), lambda t, e: (t, 0)),
        out_shape=jax.ShapeDtypeStruct((T, D), F32),
        scratch_shapes=[pltpu.VMEM((D, tm), F32), pltpu.VMEM((te, tm), BF16), pltpu.VMEM((te, tm), BF16)],
        compiler_params=_cparams(("parallel", "arbitrary")),
        name="peer",
    )(h2, u, vt, r2, e2, c1, e1, x1, gate2)


def _pad_heads(w, real):
    K = w.shape[0]
    w = w.reshape(K, -1, real)
    return jnp.pad(w, ((0, 0), (0, 0), (0, LANES - real))).reshape(K, -1)


def _layer(x, cs_mod, positions, p, l):
    B, S, D = x.shape
    H = MLA_HEADS
    shift1, scale1, gate1, shift2, scale2, gate2 = [cs_mod[:, j].reshape(B, 1, D) for j in range(6)]
    w_in = p["w_in"][l]
    c0, c1, c2 = Q_LORA, Q_LORA + KV_LORA, Q_LORA + KV_LORA + QK_ROPE
    w_rope = jnp.pad(w_in[:, c1:c2], ((0, 0), (QK_NOPE, LANES - QK_HEAD)))
    wqkr = jnp.concatenate([w_in[:, :c1], w_rope], axis=1).astype(BF16)
    wlru = w_in[:, c2:c2 + 2 * D].astype(BF16)
    wmg = w_in[:, c2 + 2 * D:].astype(BF16)
    wuq = _pad_heads(p["w_uq"][l], QK_HEAD).astype(BF16)
    wukv = p["w_ukv"][l].reshape(KV_LORA, H, QK_NOPE + V_HEAD)
    wuk = jnp.pad(wukv[:, :, :QK_NOPE], ((0, 0), (0, 0), (0, LANES - QK_NOPE))).reshape(KV_LORA, H * LANES).astype(BF16)
    wuv = jnp.pad(wukv[:, :, QK_NOPE:], ((0, 0), (0, 0), (0, LANES - V_HEAD))).reshape(KV_LORA, H * LANES).astype(BF16)
    gqn = jnp.pad(p["q_norm_g"][l] * (QK_HEAD ** -0.5), (0, LANES - QK_HEAD)).reshape(1, LANES)
    gkn = jnp.pad(p["k_norm_g"][l], (0, LANES - QK_HEAD)).reshape(1, LANES)
    inv_freq = 1.0 / (ROPE_THETA ** (jnp.arange(0, QK_ROPE, 2, dtype=F32) / QK_ROPE))
    half = QK_ROPE // 2
    zeros_lo = jnp.zeros((QK_NOPE,), F32)
    zeros_hi = jnp.zeros((LANES - QK_HEAD,), F32)
    freq = jnp.concatenate([zeros_lo, inv_freq, inv_freq, zeros_hi]).reshape(1, LANES)
    sign = jnp.concatenate([zeros_lo, -jnp.ones((half,), F32), jnp.ones((half,), F32), zeros_hi]).reshape(1, LANES)

    q, k, v, lx, lg, mg = _in_proj(
        x, positions.reshape(B, S, 1), shift1, scale1, p["norm1_g"][l].reshape(1, D), wqkr, wlru, wmg,
        p["q_a_norm_g"][l].reshape(1, Q_LORA), wuq, p["kv_a_norm_g"][l].reshape(1, KV_LORA), wuk, wuv,
        gqn, gkn, freq, sign)
    o = _attention(q, k, v)

    C = p["conv_b"].shape[-1]
    wg = jnp.concatenate([p["w_rg"][l], p["w_ig"][l]], axis=-1).astype(BF16)
    rec = _lru(lx, lg, p["conv_w"][l], p["conv_b"][l].reshape(1, C), wg, p["b_rg"][l].reshape(1, C),
               p["b_ig"][l].reshape(1, C), p["lru_lambda"][l].reshape(1, C))

    woa = jnp.pad(p["w_o_attn"][l].reshape(H, V_HEAD, D), ((0, 0), (0, LANES - V_HEAD), (0, 0)))
    woa = woa.reshape(H * LANES, D).astype(BF16)
    keys = p["sub_keys"][l].reshape(PEER_HEADS * 2, N_KEYS, -1).astype(BF16)
    x1, h2, sc = _merge(o, rec, mg, x, gate1, shift2, scale2, p["norm2_g"][l].reshape(1, D), woa,
                        p["w_o_lru"][l].astype(BF16), p["w_out"][l].astype(BF16), p["w_query"][l].astype(BF16), keys)
    r2, e2, c1, e1 = _select(sc)
    out = _peer(h2.reshape(B * S, D), p["expert_u"][l].astype(BF16), p["expert_v"][l].T.astype(BF16),
                r2, e2, c1, e1, x1.reshape(B * S, D), gate2, S)
    return out.reshape(B, S, D)


def kernel(x, c, positions, w_ada, b_ada, norm1_g, w_in, q_a_norm_g, w_uq, kv_a_norm_g, w_ukv, q_norm_g, k_norm_g, w_o_attn, conv_w, conv_b, w_rg, b_rg, w_ig, b_ig, lru_lambda, w_o_lru, w_out, norm2_g, w_query, sub_keys, expert_u, expert_v):
    p = dict(w_in=w_in, norm1_g=norm1_g, q_a_norm_g=q_a_norm_g, w_uq=w_uq, kv_a_norm_g=kv_a_norm_g, w_ukv=w_ukv,
             q_norm_g=q_norm_g, k_norm_g=k_norm_g, w_o_attn=w_o_attn, conv_w=conv_w, conv_b=conv_b, w_rg=w_rg,
             b_rg=b_rg, w_ig=w_ig, b_ig=b_ig, lru_lambda=lru_lambda, w_o_lru=w_o_lru, w_out=w_out,
             norm2_g=norm2_g, w_query=w_query, sub_keys=sub_keys, expert_u=expert_u, expert_v=expert_v)
    B, D = c.shape
    for l in range(w_ada.shape[0]):
        mod = _mod(c, w_ada[l], b_ada[l]).reshape(B, 6, D)
        x = _layer(x, mod, positions, p, l)
    return x
```

```python
import functools

import jax
import jax.numpy as jnp
from jax import lax
from jax.experimental import pallas as pl
from jax.experimental.pallas import tpu as pltpu

F32 = jnp.float32
BF16 = jnp.bfloat16

EPS = 1e-6
LANES = 128
SUBLANES = 8
VMEM_LIMIT = 56 * 1024 * 1024

CHUNK = 64
MLA_HEADS = 8
QK_NOPE = 64
QK_ROPE = 32
QK_HEAD = QK_NOPE + QK_ROPE
V_HEAD = 64
Q_LORA = 256
KV_LORA = 128
ROPE_THETA = 10000.0
LRU_BLOCKS = 8
CONV_WIDTH = 4
LRU_C = 8.0
PEER_HEADS = 8
N_KEYS = 128
PEER_TOPK = 16
GELU_K0 = 0.7978845608028654
GELU_K1 = 0.044715 * GELU_K0

TM_PROJ = 256
TQ = 256
TS_LRU = 256
TL_SEL = 128
TM_PEER = 512
TE_PEER = 1024


def _cparams(sem):
    return pltpu.CompilerParams(dimension_semantics=sem, vmem_limit_bytes=VMEM_LIMIT)


def _dot(a, b):
    return jnp.dot(a, b, preferred_element_type=F32)


def _dot_nt(a, b):
    return lax.dot_general(a, b, (((1,), (1,)), ((), ())), preferred_element_type=F32)


def _mod_body(c_ref, w_ref, b_ref, o_ref):
    c = c_ref[...]
    cs = c * jax.nn.sigmoid(c)
    o_ref[...] = _dot(cs.astype(BF16), w_ref[...].astype(BF16)) + b_ref[...]


def _mod(c, w_ada, b_ada):
    B, D = c.shape
    N = w_ada.shape[1]
    tn = 1024
    return pl.pallas_call(
        _mod_body,
        grid=(N // tn,),
        in_specs=[pl.BlockSpec((B, D), lambda n: (0, 0)),
                  pl.BlockSpec((D, tn), lambda n: (0, n)),
                  pl.BlockSpec((1, tn), lambda n: (0, n))],
        out_specs=pl.BlockSpec((B, tn), lambda n: (0, n)),
        out_shape=jax.ShapeDtypeStruct((B, N), F32),
        compiler_params=_cparams(("parallel",)),
        name="mod",
    )(c, w_ada, b_ada.reshape(1, N))


def _rope(t, cos_l, sin_l, lane):
    rot = jnp.where(lane < QK_NOPE + QK_ROPE // 2,
                    pltpu.roll(t, LANES - QK_ROPE // 2, 1), pltpu.roll(t, QK_ROPE // 2, 1))
    return t * cos_l + rot * sin_l


def _in_proj_body(x_ref, pos_ref, shift_ref, scale_ref, g1_ref, wqkr_ref, wlru_ref, wmg_ref,
                  gqa_ref, wuq_ref, gkva_ref, wuk_ref, wuv_ref, gqn_ref, gkn_ref, freq_ref, sign_ref,
                  q_ref, k_ref, v_ref, lx_ref, lg_ref, mg_ref):
    x = x_ref[0]
    D = x.shape[-1]
    y = x * lax.rsqrt(jnp.mean(x * x, axis=-1, keepdims=True) + EPS) * g1_ref[...]
    h = y * (1.0 + scale_ref[0]) + shift_ref[0]
    hb = h.astype(BF16)

    lru = _dot(hb, wlru_ref[...])
    lx_ref[0] = lru[:, :D]
    lg_ref[0] = lru[:, D:]
    mg_ref[0] = _dot(hb, wmg_ref[...])

    qkr = _dot(hb, wqkr_ref[...])
    cq = qkr[:, :Q_LORA]
    ckv = qkr[:, Q_LORA:Q_LORA + KV_LORA]
    kr = qkr[:, Q_LORA + KV_LORA:]
    qn = cq * lax.rsqrt(jnp.mean(cq * cq, axis=-1, keepdims=True) + EPS) * gqa_ref[...]
    kvn = ckv * lax.rsqrt(jnp.mean(ckv * ckv, axis=-1, keepdims=True) + EPS) * gkva_ref[...]
    kvb = kvn.astype(BF16)
    q_all = _dot(qn.astype(BF16), wuq_ref[...])
    k_all = _dot(kvb, wuk_ref[...])
    v_ref[0] = _dot(kvb, wuv_ref[...]).astype(BF16)

    ang = pos_ref[0].astype(F32) * freq_ref[...]
    cos_l = jnp.cos(ang)
    sin_l = jnp.sin(ang) * sign_ref[...]
    lane = lax.broadcasted_iota(jnp.int32, ang.shape, 1)
    inv_n = 1.0 / QK_HEAD
    for hh in range(MLA_HEADS):
        sl = slice(hh * LANES, (hh + 1) * LANES)
        qh = q_all[:, sl]
        qh = qh * lax.rsqrt(jnp.sum(qh * qh, axis=-1, keepdims=True) * inv_n + EPS) * gqn_ref[...]
        q_ref[0, :, sl] = _rope(qh, cos_l, sin_l, lane).astype(BF16)
        kh = k_all[:, sl] + kr
        kh = kh * lax.rsqrt(jnp.sum(kh * kh, axis=-1, keepdims=True) * inv_n + EPS) * gkn_ref[...]
        k_ref[0, :, sl] = _rope(kh, cos_l, sin_l, lane).astype(BF16)


def _in_proj(x, pos3, shift1, scale1, g1, wqkr, wlru, wmg, gqa, wuq, gkva, wuk, wuv, gqn, gkn, freq, sign):
    B, S, D = x.shape
    tm = TM_PROJ
    HP = MLA_HEADS * LANES
    tok = lambda w: pl.BlockSpec((1, tm, w), lambda b, s: (b, s, 0))
    per_b = pl.BlockSpec((1, 1, D), lambda b, s: (b, 0, 0))
    full = lambda a: pl.BlockSpec(a.shape, lambda b, s: (0,) * a.ndim)
    return pl.pallas_call(
        _in_proj_body,
        grid=(B, S // tm),
        in_specs=[tok(D), tok(1), per_b, per_b, full(g1), full(wqkr), full(wlru), full(wmg),
                  full(gqa), full(wuq), full(gkva), full(wuk), full(wuv), full(gqn), full(gkn),
                  full(freq), full(sign)],
        out_specs=[tok(HP), tok(HP), tok(HP), tok(D), tok(D), tok(2 * D)],
        out_shape=[jax.ShapeDtypeStruct((B, S, HP), BF16)] * 3
        + [jax.ShapeDtypeStruct((B, S, D), F32)] * 2 + [jax.ShapeDtypeStruct((B, S, 2 * D), F32)],
        compiler_params=_cparams(("parallel", "parallel")),
        name="in_proj",
    )(x, pos3, shift1, scale1, g1, wqkr, wlru, wmg, gqa, wuq, gkva, wuk, wuv, gqn, gkn, freq, sign)


def _attn_body(q_ref, k_ref, v_ref, o_ref):
    qi = pl.program_id(2)
    q = q_ref[0]
    neg = float(jnp.finfo(jnp.float32).min)
    shift = CHUNK.bit_length() - 1
    qc = jnp.right_shift(lax.broadcasted_iota(jnp.int32, (TQ, TQ), 0), shift)
    kc = jnp.right_shift(lax.broadcasted_iota(jnp.int32, (TQ, TQ), 1), shift)
    allowed = kc <= qc

    for nk in range(1, k_ref.shape[1] // TQ + 1):
        @pl.when(qi == nk - 1)
        def _(nk=nk):
            n_keys = nk * TQ
            s = _dot_nt(q, k_ref[0, :n_keys, :])
            diag = jnp.where(allowed, s[:, n_keys - TQ:], neg)
            s = diag if nk == 1 else jnp.concatenate([s[:, :n_keys - TQ], diag], axis=1)
            p = jnp.exp(s - jnp.max(s, axis=-1, keepdims=True))
            l = jnp.sum(p, axis=-1, keepdims=True)
            o_ref[0] = (_dot(p.astype(BF16), v_ref[0, :n_keys, :]) / l).astype(BF16)


def _attention(q, k, v):
    B, S, HP = q.shape
    H = HP // LANES
    return pl.pallas_call(
        _attn_body,
        grid=(B, H, S // TQ),
        in_specs=[pl.BlockSpec((1, TQ, LANES), lambda b, h, i: (b, i, h)),
                  pl.BlockSpec((1, S, LANES), lambda b, h, i: (b, 0, h)),
                  pl.BlockSpec((1, S, LANES), lambda b, h, i: (b, 0, h))],
        out_specs=pl.BlockSpec((1, TQ, LANES), lambda b, h, i: (b, i, h)),
        out_shape=jax.ShapeDtypeStruct((B, S, HP), BF16),
        compiler_params=_cparams(("parallel", "parallel", "arbitrary")),
        name="attn",
    )(q, k, v)


def _lru_body(lx_ref, lg_ref, cw_ref, cb_ref, wg_ref, brg_ref, big_ref, lam_ref, rec_ref, prev_ref, h_ref):
    si = pl.program_id(1)

    @pl.when(si == 0)
    def _():
        prev_ref[...] = jnp.zeros_like(prev_ref)
        h_ref[...] = jnp.zeros_like(h_ref)

    x = lx_ref[0]
    ts, C = x.shape
    prev8 = prev_ref[...]
    row8 = lax.broadcasted_iota(jnp.int32, prev8.shape, 0)
    xc = x * cw_ref[CONV_WIDTH - 1:CONV_WIDTH, :] + cb_ref[...]
    for d in range(1, CONV_WIDTH):
        xr = pltpu.roll(x, d, 0)
        first = jnp.where(row8 < d, pltpu.roll(prev8, d, 0), xr[:SUBLANES])
        xs = jnp.concatenate([first, xr[SUBLANES:]], axis=0)
        xc = xc + xs * cw_ref[CONV_WIDTH - 1 - d:CONV_WIDTH - d, :]
    prev_ref[...] = x[ts - SUBLANES:]

    xcb = xc.astype(BF16)
    rowmod = jnp.bitwise_and(lax.broadcasted_iota(jnp.int32, (ts, LANES), 0), SUBLANES - 1)
    for blk in range(LRU_BLOCKS):
        sl = slice(blk * LANES, (blk + 1) * LANES)
        xb = xc[:, sl]
        g = _dot(xcb[:, sl], wg_ref[blk])
        r = jax.nn.sigmoid(g[:, :LANES] + brg_ref[:, sl])
        i = jax.nn.sigmoid(g[:, LANES:] + big_ref[:, sl])
        nl = -lam_ref[:, sl]
        softplus = jnp.maximum(nl, 0.0) + jnp.log(1.0 + jnp.exp(-jnp.abs(nl)))
        log_a = (-LRU_C * r) * softplus
        a = jnp.exp(log_a)
        b = jnp.sqrt(1.0 - a * a) * (i * xb)
        for d in (1, 2, 4):
            valid = rowmod >= d
            b = jnp.where(valid, a * pltpu.roll(b, d, 0) + b, b)
            a = jnp.where(valid, a * pltpu.roll(a, d, 0), a)
        carry = h_ref[:, sl]
        hs = []
        for gi in range(ts // SUBLANES):
            rs = slice(gi * SUBLANES, (gi + 1) * SUBLANES)
            hg = a[rs] * carry + b[rs]
            carry = hg[SUBLANES - 1:SUBLANES]
            hs.append(hg)
        h_ref[:, sl] = carry
        hfull = jnp.concatenate(hs, axis=0)
        rec_ref[0, :, sl] = (hfull * jax.nn.gelu(lg_ref[0, :, sl])).astype(BF16)


def _lru(lx, lg, cw, cb, wg, brg, big, lam):
    B, S, C = lx.shape
    ts = TS_LRU
    tok = pl.BlockSpec((1, ts, C), lambda b, s: (b, s, 0))
    full = lambda a: pl.BlockSpec(a.shape, lambda b, s: (0,) * a.ndim)
    return pl.pallas_call(
        _lru_body,
        grid=(B, S // ts),
        in_specs=[tok, tok, full(cw), full(cb), full(wg), full(brg), full(big), full(lam)],
        out_specs=tok,
        out_shape=jax.ShapeDtypeStruct((B, S, C), BF16),
        scratch_shapes=[pltpu.VMEM((SUBLANES, C), F32), pltpu.VMEM((1, C), F32)],
        compiler_params=_cparams(("parallel", "arbitrary")),
        name="lru",
    )(lx, lg, cw, cb, wg, brg, big, lam)


def _merge_body(o_ref, rec_ref, mg_ref, x_ref, gate1_ref, shift2_ref, scale2_ref, g2_ref,
                woa_ref, wob_ref, wout_ref, wq_ref, keys_ref, x1_ref, h2_ref, sc_ref):
    D = x_ref.shape[-1]
    ya = _dot(o_ref[0], woa_ref[...])
    yb = _dot(rec_ref[0], wob_ref[...])
    mg = mg_ref[0]
    y = jax.nn.sigmoid(mg[:, :D]) * ya + jax.nn.sigmoid(mg[:, D:]) * yb
    x1 = x_ref[0] + gate1_ref[0] * _dot(y.astype(BF16), wout_ref[...])
    x1_ref[0] = x1
    h2 = x1 * lax.rsqrt(jnp.mean(x1 * x1, axis=-1, keepdims=True) + EPS) * g2_ref[...]
    h2 = h2 * (1.0 + scale2_ref[0]) + shift2_ref[0]
    h2b = h2.astype(BF16)
    h2_ref[0] = h2b
    qp = _dot(h2b, wq_ref[...]).astype(BF16)
    for g in range(keys_ref.shape[0]):
        sc_ref[g * N_KEYS:(g + 1) * N_KEYS, :] = _dot_nt(keys_ref[g], qp[:, g * LANES:(g + 1) * LANES])


def _merge(o, rec, mg, x, gate1, shift2, scale2, g2, woa, wob, wout, wq, keys):
    B, S, D = x.shape
    tm = TM_PROJ
    ns = S // tm
    NG = keys.shape[0]
    tok = lambda w: pl.BlockSpec((1, tm, w), lambda b, s: (b, s, 0))
    per_b = pl.BlockSpec((1, 1, D), lambda b, s: (b, 0, 0))
    full = lambda a: pl.BlockSpec(a.shape, lambda b, s: (0,) * a.ndim)
    return pl.pallas_call(
        _merge_body,
        grid=(B, ns),
        in_specs=[tok(o.shape[-1]), tok(D), tok(2 * D), tok(D), per_b, per_b, per_b, full(g2),
                  full(woa), full(wob), full(wout), full(wq), full(keys)],
        out_specs=[tok(D), tok(D), pl.BlockSpec((NG * N_KEYS, tm), lambda b, s: (0, b * ns + s))],
        out_shape=[jax.ShapeDtypeStruct((B, S, D), F32), jax.ShapeDtypeStruct((B, S, D), BF16),
                   jax.ShapeDtypeStruct((NG * N_KEYS, B * S), F32)],
        compiler_params=_cparams(("parallel", "parallel")),
        name="merge",
    )(o, rec, mg, x, gate1, shift2, scale2, g2, woa, wob, wout, wq, keys)


def _sort16_network():
    def merge(lo, hi, r):
        step = r * 2
        if step < hi - lo:
            yield from merge(lo, hi, step)
            yield from merge(lo + r, hi, step)
            yield from [(i, i + r) for i in range(lo + r, hi - r, step)]
        else:
            yield (lo, lo + r)

    def sort(lo, hi):
        if hi - lo >= 1:
            mid = lo + (hi - lo) // 2
            yield from sort(lo, mid)
            yield from sort(mid + 1, hi)
            yield from merge(lo, hi, 1)

    return tuple(sort(0, PEER_TOPK - 1))


_SORT16 = _sort16_network()


def _compare_exchange(v, i, j):
    v[i], v[j] = jnp.maximum(v[i], v[j]), jnp.minimum(v[i], v[j])


def _bitonic_merge16(v):
    d = PEER_TOPK // 2
    while d:
        for i in range(PEER_TOPK):
            if not i & d:
                _compare_exchange(v, i, i + d)
        d //= 2


def _merge_top16(best, other):
    z = [jnp.maximum(best[i], other[PEER_TOPK - 1 - i]) if PEER_TOPK - 1 - i < len(other) else best[i]
         for i in range(PEER_TOPK)]
    _bitonic_merge16(z)
    return z


def _top16_of_128(v):
    v = list(v)
    for i, j in _SORT16:
        _compare_exchange(v, i, j)
    shift = SUBLANES // 2
    while shift:
        v = _merge_top16(v, [pltpu.roll(x, shift, 0) for x in v])
        shift //= 2
    return v


def _dup_bf16_bits(x):
    hi = lax.bitcast_convert_type(x.astype(BF16).astype(F32), jnp.uint32)
    return hi | (hi >> 16)


def _select_body(sc_ref, r2_ref, e2_ref, c1_ref, e1_ref):
    K = PEER_TOPK
    H = PEER_HEADS
    nv = N_KEYS // SUBLANES
    tl = sc_ref.shape[1]
    sub = lax.broadcasted_iota(jnp.int32, (SUBLANES, tl), 0)

    def group(g):
        return [sc_ref[g * N_KEYS + k * SUBLANES:g * N_KEYS + (k + 1) * SUBLANES, :] for k in range(nv)]

    t1 = [_top16_of_128(group(2 * h)) for h in range(H)]
    t2 = [_top16_of_128(group(2 * h + 1)) for h in range(H)]

    def pack(reps):
        out = reps[0]
        for h in range(1, H):
            out = jnp.where(sub == h, reps[h], out)
        return out

    def unpack(x, h):
        return jnp.broadcast_to(x[h:h + 1, :], x.shape)

    t1p = [pack([t1[h][a] for h in range(H)]) for a in range(K)]
    t2p = [pack([t2[h][b] for h in range(H)]) for b in range(K)]
    cand = [[t1p[a] + t2p[b] for b in range(K // (a + 1))] for a in range(K)]
    best = list(cand[0])
    for a in range(1, SUBLANES):
        best = _merge_top16(best, cand[a])
    best = _merge_top16(best, [cand[a][0] for a in range(SUBLANES, K)])
    thr = best[K - 1]
    z = jnp.ones_like(thr)
    for r in range(1, K):
        z = z + jnp.exp(best[r] - best[0])
    inv_z = 1.0 / z
    phi = []
    for b in range(K):
        f = jnp.full_like(thr, float("inf"))
        for a in range(K // (b + 1)):
            f = jnp.where(cand[a][b] >= thr, t1p[a], f)
        phi.append(f)

    for h in range(H):
        phi_h = [unpack(f, h) for f in phi]
        inv_z_h = unpack(inv_z, h)
        c1, e1, r2, e2 = [], [], [], []
        for x in group(2 * h):
            cnt = jnp.zeros_like(x)
            for b in range(K):
                cnt = cnt + jnp.where(x >= phi_h[b], 1.0, 0.0)
            c1.append(_dup_bf16_bits(cnt))
            e1.append(_dup_bf16_bits(jnp.exp(x - t1[h][0])))
        for x in group(2 * h + 1):
            rank = jnp.zeros_like(x)
            for r in range(K):
                rank = rank + jnp.where(t2[h][r] > x, 1.0, 0.0)
            r2.append(rank)
            e2.append(jnp.exp(x - t2[h][0]) * inv_z_h)
        rows = slice(h * N_KEYS, (h + 1) * N_KEYS)
        c1_ref[rows, :] = jnp.concatenate(c1, axis=0)
        e1_ref[rows, :] = jnp.concatenate(e1, axis=0)
        r2_ref[rows, :] = jnp.concatenate(r2, axis=0).astype(BF16)
        e2_ref[rows, :] = jnp.concatenate(e2, axis=0).astype(BF16)


def _select(sc):
    R, T = sc.shape
    tl = TL_SEL
    half = R // 2
    blk = pl.BlockSpec((half, tl), lambda t: (0, t))
    return pl.pallas_call(
        _select_body,
        grid=(T // tl,),
        in_specs=[pl.BlockSpec((R, tl), lambda t: (0, t))],
        out_specs=[blk, blk, blk, blk],
        out_shape=[jax.ShapeDtypeStruct((half, T), BF16), jax.ShapeDtypeStruct((half, T), BF16),
                   jax.ShapeDtypeStruct((half, T), jnp.uint32), jax.ShapeDtypeStruct((half, T), jnp.uint32)],
        compiler_params=_cparams(("parallel",)),
        name="select",
    )(sc)


def _peer_body(h2_ref, u0_ref, un_ref, vt_ref, r2_ref, e2_ref, c1_ref, e1_ref, x1_ref, gate2_ref,
               out_ref, acc_ref, at_ref, w_ref):
    e = pl.program_id(1)
    tm = h2_ref.shape[0]
    n_i = un_ref.shape[0] // N_KEYS
    pack_rows = N_KEYS // 2

    @pl.when(e == 0)
    def _():
        acc_ref[...] = jnp.zeros_like(acc_ref)
        at_ref[...] = _dot_nt(u0_ref[...], h2_ref[...]).astype(BF16)

    assert n_i == SUBLANES
    i0 = pl.multiple_of(e * n_i, SUBLANES)

    def row_bf16(ref, h, ii, ls):
        row = ref[pl.ds(h * N_KEYS + i0, SUBLANES), ls][ii:ii + 1]
        return pltpu.bitcast(jnp.broadcast_to(row, (pack_rows, LANES)), BF16)

    def routing_weights(lb):
        ls = slice(lb * LANES, (lb + 1) * LANES)
        for ii in range(n_i):
            rs = slice(ii * N_KEYS, (ii + 1) * N_KEYS)
            coef = None
            for h in range(PEER_HEADS):
                hs = slice(h * N_KEYS, (h + 1) * N_KEYS)
                e2 = e2_ref[hs, ls]
                sel = r2_ref[hs, ls] < row_bf16(c1_ref, h, ii, ls)
                term = jnp.where(sel, e2, jnp.zeros_like(e2)) * row_bf16(e1_ref, h, ii, ls)
                coef = term if coef is None else coef + term
            a = at_ref[rs, ls]
            cdf = 0.5 * jnp.tanh(a * (GELU_K0 + GELU_K1 * (a * a))) + 0.5
            w_ref[rs, ls] = coef * (a * cdf)

    half = tm // 2
    for ts in (slice(0, half), slice(half, tm)):
        for lb in range(ts.start // LANES, ts.stop // LANES):
            routing_weights(lb)
        at_ref[:, ts] = _dot_nt(un_ref[...], h2_ref[ts, :]).astype(BF16)
        acc_ref[:, ts] += _dot(vt_ref[...], w_ref[:, ts])

    @pl.when(e == pl.num_programs(1) - 1)
    def _():
        out_ref[...] = x1_ref[...] + gate2_ref[0] * acc_ref[...].T


def _peer(h2, u, vt, r2, e2, c1, e1, x1, gate2, seq_len):
    T, D = h2.shape
    NE = u.shape[0]
    tm, te = TM_PEER, TE_PEER
    tiles_per_seq = seq_len // tm
    table = pl.BlockSpec((r2.shape[0], tm), lambda t, e: (0, t))
    return pl.pallas_call(
        _peer_body,
        grid=(T // tm, NE // te),
        in_specs=[pl.BlockSpec((tm, D), lambda t, e: (t, 0)),
                  pl.BlockSpec((te, D), lambda t, e: (0, 0)),
                  pl.BlockSpec((te, D), lambda t, e: (jnp.minimum(e + 1, NE // te - 1), 0)),
                  pl.BlockSpec((D, te), lambda t, e: (0, e)),
                  table, table, table, table,
                  pl.BlockSpec((tm, D), lambda t, e: (t, 0)),
                  pl.BlockSpec((1, 1, D), lambda t, e: (t // tiles_per_seq, 0, 0))],
        out_specs=pl.BlockSpec((tm, D), lambda t, e: (t, 0)),
        out_shape=jax.ShapeDtypeStruct((T, D), F32),
        scratch_shapes=[pltpu.VMEM((D, tm), F32), pltpu.VMEM((te, tm), BF16), pltpu.VMEM((te, tm), BF16)],
        compiler_params=_cparams(("parallel", "arbitrary")),
        name="peer",
    )(h2, u, u, vt, r2, e2, c1, e1, x1, gate2)


def _pad_heads(w, real):
    K = w.shape[0]
    w = w.reshape(K, -1, real)
    return jnp.pad(w, ((0, 0), (0, 0), (0, LANES - real))).reshape(K, -1)


def _layer(x, cs_mod, positions, p, l):
    B, S, D = x.shape
    H = MLA_HEADS
    shift1, scale1, gate1, shift2, scale2, gate2 = [cs_mod[:, j].reshape(B, 1, D) for j in range(6)]
    w_in = p["w_in"][l]
    c0, c1, c2 = Q_LORA, Q_LORA + KV_LORA, Q_LORA + KV_LORA + QK_ROPE
    w_rope = jnp.pad(w_in[:, c1:c2], ((0, 0), (QK_NOPE, LANES - QK_HEAD)))
    wqkr = jnp.concatenate([w_in[:, :c1], w_rope], axis=1).astype(BF16)
    wlru = w_in[:, c2:c2 + 2 * D].astype(BF16)
    wmg = w_in[:, c2 + 2 * D:].astype(BF16)
    wuq = _pad_heads(p["w_uq"][l], QK_HEAD).astype(BF16)
    wukv = p["w_ukv"][l].reshape(KV_LORA, H, QK_NOPE + V_HEAD)
    wuk = jnp.pad(wukv[:, :, :QK_NOPE], ((0, 0), (0, 0), (0, LANES - QK_NOPE))).reshape(KV_LORA, H * LANES).astype(BF16)
    wuv = jnp.pad(wukv[:, :, QK_NOPE:], ((0, 0), (0, 0), (0, LANES - V_HEAD))).reshape(KV_LORA, H * LANES).astype(BF16)
    gqn = jnp.pad(p["q_norm_g"][l] * (QK_HEAD ** -0.5), (0, LANES - QK_HEAD)).reshape(1, LANES)
    gkn = jnp.pad(p["k_norm_g"][l], (0, LANES - QK_HEAD)).reshape(1, LANES)
    inv_freq = 1.0 / (ROPE_THETA ** (jnp.arange(0, QK_ROPE, 2, dtype=F32) / QK_ROPE))
    half = QK_ROPE // 2
    zeros_lo = jnp.zeros((QK_NOPE,), F32)
    zeros_hi = jnp.zeros((LANES - QK_HEAD,), F32)
    freq = jnp.concatenate([zeros_lo, inv_freq, inv_freq, zeros_hi]).reshape(1, LANES)
    sign = jnp.concatenate([zeros_lo, -jnp.ones((half,), F32), jnp.ones((half,), F32), zeros_hi]).reshape(1, LANES)

    q, k, v, lx, lg, mg = _in_proj(
        x, positions.reshape(B, S, 1), shift1, scale1, p["norm1_g"][l].reshape(1, D), wqkr, wlru, wmg,
        p["q_a_norm_g"][l].reshape(1, Q_LORA), wuq, p["kv_a_norm_g"][l].reshape(1, KV_LORA), wuk, wuv,
        gqn, gkn, freq, sign)
    o = _attention(q, k, v)

    C = p["conv_b"].shape[-1]
    wg = jnp.concatenate([p["w_rg"][l], p["w_ig"][l]], axis=-1).astype(BF16)
    rec = _lru(lx, lg, p["conv_w"][l], p["conv_b"][l].reshape(1, C), wg, p["b_rg"][l].reshape(1, C),
               p["b_ig"][l].reshape(1, C), p["lru_lambda"][l].reshape(1, C))

    woa = jnp.pad(p["w_o_attn"][l].reshape(H, V_HEAD, D), ((0, 0), (0, LANES - V_HEAD), (0, 0)))
    woa = woa.reshape(H * LANES, D).astype(BF16)
    keys = p["sub_keys"][l].reshape(PEER_HEADS * 2, N_KEYS, -1).astype(BF16)
    x1, h2, sc = _merge(o, rec, mg, x, gate1, shift2, scale2, p["norm2_g"][l].reshape(1, D), woa,
                        p["w_o_lru"][l].astype(BF16), p["w_out"][l].astype(BF16), p["w_query"][l].astype(BF16), keys)
    r2, e2, c1, e1 = _select(sc)
    out = _peer(h2.reshape(B * S, D), p["expert_u"][l].astype(BF16), p["expert_v"][l].T.astype(BF16),
                r2, e2, c1, e1, x1.reshape(B * S, D), gate2, S)
    return out.reshape(B, S, D)


def kernel(x, c, positions, w_ada, b_ada, norm1_g, w_in, q_a_norm_g, w_uq, kv_a_norm_g, w_ukv, q_norm_g, k_norm_g, w_o_attn, conv_w, conv_b, w_rg, b_rg, w_ig, b_ig, lru_lambda, w_o_lru, w_out, norm2_g, w_query, sub_keys, expert_u, expert_v):
    p = dict(w_in=w_in, norm1_g=norm1_g, q_a_norm_g=q_a_norm_g, w_uq=w_uq, kv_a_norm_g=kv_a_norm_g, w_ukv=w_ukv,
             q_norm_g=q_norm_g, k_norm_g=k_norm_g, w_o_attn=w_o_attn, conv_w=conv_w, conv_b=conv_b, w_rg=w_rg,
             b_rg=b_rg, w_ig=w_ig, b_ig=b_ig, lru_lambda=lru_lambda, w_o_lru=w_o_lru, w_out=w_out,
             norm2_g=norm2_g, w_query=w_query, sub_keys=sub_keys, expert_u=expert_u, expert_v=expert_v)
    B, D = c.shape
    for l in range(w_ada.shape[0]):
        mod = _mod(c, w_ada[l], b_ada[l]).reshape(B, 6, D)
        x = _layer(x, mod, positions, p, l)
    return x
```

```python
import functools

import jax
import jax.numpy as jnp
from jax import lax
from jax.experimental import pallas as pl
from jax.experimental.pallas import tpu as pltpu

F32 = jnp.float32
BF16 = jnp.bfloat16

EPS = 1e-6
LANES = 128
SUBLANES = 8
VMEM_LIMIT = 56 * 1024 * 1024

CHUNK = 64
MLA_HEADS = 8
QK_NOPE = 64
QK_ROPE = 32
QK_HEAD = QK_NOPE + QK_ROPE
V_HEAD = 64
Q_LORA = 256
KV_LORA = 128
ROPE_THETA = 10000.0
LRU_BLOCKS = 8
CONV_WIDTH = 4
LRU_C = 8.0
PEER_HEADS = 8
N_KEYS = 128
PEER_TOPK = 16
GELU_K0 = 0.7978845608028654
GELU_K1 = 0.044715 * GELU_K0

TM_PROJ = 256
TQ = 256
TS_LRU = 256
TL_SEL = 128
TM_PEER = 512
TE_PEER = 1024


def _cparams(sem):
    return pltpu.CompilerParams(dimension_semantics=sem, vmem_limit_bytes=VMEM_LIMIT)


def _dot(a, b):
    return jnp.dot(a, b, preferred_element_type=F32)


def _dot_nt(a, b):
    return lax.dot_general(a, b, (((1,), (1,)), ((), ())), preferred_element_type=F32)


def _mod_body(c_ref, w_ref, b_ref, o_ref):
    c = c_ref[...]
    cs = c * jax.nn.sigmoid(c)
    o_ref[...] = _dot(cs.astype(BF16), w_ref[...].astype(BF16)) + b_ref[...]


def _mod(c, w_ada, b_ada):
    B, D = c.shape
    N = w_ada.shape[1]
    tn = 1024
    return pl.pallas_call(
        _mod_body,
        grid=(N // tn,),
        in_specs=[pl.BlockSpec((B, D), lambda n: (0, 0)),
                  pl.BlockSpec((D, tn), lambda n: (0, n)),
                  pl.BlockSpec((1, tn), lambda n: (0, n))],
        out_specs=pl.BlockSpec((B, tn), lambda n: (0, n)),
        out_shape=jax.ShapeDtypeStruct((B, N), F32),
        compiler_params=_cparams(("parallel",)),
        name="mod",
    )(c, w_ada, b_ada.reshape(1, N))


def _rope(t, cos_l, sin_l, lane):
    rot = jnp.where(lane < QK_NOPE + QK_ROPE // 2,
                    pltpu.roll(t, LANES - QK_ROPE // 2, 1), pltpu.roll(t, QK_ROPE // 2, 1))
    return t * cos_l + rot * sin_l


def _in_proj_body(x_ref, pos_ref, shift_ref, scale_ref, g1_ref, wqkr_ref, wlru_ref, wmg_ref,
                  gqa_ref, wuq_ref, gkva_ref, wuk_ref, wuv_ref, gqn_ref, gkn_ref, freq_ref, sign_ref,
                  q_ref, k_ref, v_ref, lx_ref, lg_ref, mg_ref):
    x = x_ref[0]
    D = x.shape[-1]
    y = x * lax.rsqrt(jnp.mean(x * x, axis=-1, keepdims=True) + EPS) * g1_ref[...]
    h = y * (1.0 + scale_ref[0]) + shift_ref[0]
    hb = h.astype(BF16)

    lru = _dot(hb, wlru_ref[...])
    lx_ref[0] = lru[:, :D]
    lg_ref[0] = lru[:, D:]
    mg_ref[0] = _dot(hb, wmg_ref[...])

    qkr = _dot(hb, wqkr_ref[...])
    cq = qkr[:, :Q_LORA]
    ckv = qkr[:, Q_LORA:Q_LORA + KV_LORA]
    kr = qkr[:, Q_LORA + KV_LORA:]
    qn = cq * lax.rsqrt(jnp.mean(cq * cq, axis=-1, keepdims=True) + EPS) * gqa_ref[...]
    kvn = ckv * lax.rsqrt(jnp.mean(ckv * ckv, axis=-1, keepdims=True) + EPS) * gkva_ref[...]
    kvb = kvn.astype(BF16)
    q_all = _dot(qn.astype(BF16), wuq_ref[...])
    k_all = _dot(kvb, wuk_ref[...])
    v_ref[0] = _dot(kvb, wuv_ref[...]).astype(BF16)

    ang = pos_ref[0].astype(F32) * freq_ref[...]
    cos_l = jnp.cos(ang)
    sin_l = jnp.sin(ang) * sign_ref[...]
    lane = lax.broadcasted_iota(jnp.int32, ang.shape, 1)
    inv_n = 1.0 / QK_HEAD
    for hh in range(MLA_HEADS):
        sl = slice(hh * LANES, (hh + 1) * LANES)
        qh = q_all[:, sl]
        qh = qh * lax.rsqrt(jnp.sum(qh * qh, axis=-1, keepdims=True) * inv_n + EPS) * gqn_ref[...]
        q_ref[0, :, sl] = _rope(qh, cos_l, sin_l, lane).astype(BF16)
        kh = k_all[:, sl] + kr
        kh = kh * lax.rsqrt(jnp.sum(kh * kh, axis=-1, keepdims=True) * inv_n + EPS) * gkn_ref[...]
        k_ref[0, :, sl] = _rope(kh, cos_l, sin_l, lane).astype(BF16)


def _in_proj(x, pos3, shift1, scale1, g1, wqkr, wlru, wmg, gqa, wuq, gkva, wuk, wuv, gqn, gkn, freq, sign):
    B, S, D = x.shape
    tm = TM_PROJ
    HP = MLA_HEADS * LANES
    tok = lambda w: pl.BlockSpec((1, tm, w), lambda b, s: (b, s, 0))
    per_b = pl.BlockSpec((1, 1, D), lambda b, s: (b, 0, 0))
    full = lambda a: pl.BlockSpec(a.shape, lambda b, s: (0,) * a.ndim)
    return pl.pallas_call(
        _in_proj_body,
        grid=(B, S // tm),
        in_specs=[tok(D), tok(1), per_b, per_b, full(g1), full(wqkr), full(wlru), full(wmg),
                  full(gqa), full(wuq), full(gkva), full(wuk), full(wuv), full(gqn), full(gkn),
                  full(freq), full(sign)],
        out_specs=[tok(HP), tok(HP), tok(HP), tok(D), tok(D), tok(2 * D)],
        out_shape=[jax.ShapeDtypeStruct((B, S, HP), BF16)] * 3
        + [jax.ShapeDtypeStruct((B, S, D), F32)] * 2 + [jax.ShapeDtypeStruct((B, S, 2 * D), F32)],
        compiler_params=_cparams(("parallel", "parallel")),
        name="in_proj",
    )(x, pos3, shift1, scale1, g1, wqkr, wlru, wmg, gqa, wuq, gkva, wuk, wuv, gqn, gkn, freq, sign)


def _attn_body(q_ref, k_ref, v_ref, o_ref):
    qi = pl.program_id(2)
    q = q_ref[0]
    neg = float(jnp.finfo(jnp.float32).min)
    shift = CHUNK.bit_length() - 1
    qc = jnp.right_shift(lax.broadcasted_iota(jnp.int32, (TQ, TQ), 0), shift)
    kc = jnp.right_shift(lax.broadcasted_iota(jnp.int32, (TQ, TQ), 1), shift)
    allowed = kc <= qc

    for nk in range(1, k_ref.shape[1] // TQ + 1):
        @pl.when(qi == nk - 1)
        def _(nk=nk):
            n_keys = nk * TQ
            s = _dot_nt(q, k_ref[0, :n_keys, :])
            diag = jnp.where(allowed, s[:, n_keys - TQ:], neg)
            s = diag if nk == 1 else jnp.concatenate([s[:, :n_keys - TQ], diag], axis=1)
            p = jnp.exp(s - jnp.max(s, axis=-1, keepdims=True))
            l = jnp.sum(p, axis=-1, keepdims=True)
            o_ref[0] = (_dot(p.astype(BF16), v_ref[0, :n_keys, :]) / l).astype(BF16)


def _attention(q, k, v):
    B, S, HP = q.shape
    H = HP // LANES
    return pl.pallas_call(
        _attn_body,
        grid=(B, H, S // TQ),
        in_specs=[pl.BlockSpec((1, TQ, LANES), lambda b, h, i: (b, i, h)),
                  pl.BlockSpec((1, S, LANES), lambda b, h, i: (b, 0, h)),
                  pl.BlockSpec((1, S, LANES), lambda b, h, i: (b, 0, h))],
        out_specs=pl.BlockSpec((1, TQ, LANES), lambda b, h, i: (b, i, h)),
        out_shape=jax.ShapeDtypeStruct((B, S, HP), BF16),
        compiler_params=_cparams(("parallel", "parallel", "arbitrary")),
        name="attn",
    )(q, k, v)


def _lru_body(lx_ref, lg_ref, cw_ref, cb_ref, wg_ref, brg_ref, big_ref, lam_ref, rec_ref, prev_ref, h_ref):
    si = pl.program_id(1)

    @pl.when(si == 0)
    def _():
        prev_ref[...] = jnp.zeros_like(prev_ref)
        h_ref[...] = jnp.zeros_like(h_ref)

    x = lx_ref[0]
    ts, C = x.shape
    prev8 = prev_ref[...]
    row8 = lax.broadcasted_iota(jnp.int32, prev8.shape, 0)
    xc = x * cw_ref[CONV_WIDTH - 1:CONV_WIDTH, :] + cb_ref[...]
    for d in range(1, CONV_WIDTH):
        xr = pltpu.roll(x, d, 0)
        first = jnp.where(row8 < d, pltpu.roll(prev8, d, 0), xr[:SUBLANES])
        xs = jnp.concatenate([first, xr[SUBLANES:]], axis=0)
        xc = xc + xs * cw_ref[CONV_WIDTH - 1 - d:CONV_WIDTH - d, :]
    prev_ref[...] = x[ts - SUBLANES:]

    xcb = xc.astype(BF16)
    rowmod = jnp.bitwise_and(lax.broadcasted_iota(jnp.int32, (ts, LANES), 0), SUBLANES - 1)
    for blk in range(LRU_BLOCKS):
        sl = slice(blk * LANES, (blk + 1) * LANES)
        xb = xc[:, sl]
        g = _dot(xcb[:, sl], wg_ref[blk])
        r = jax.nn.sigmoid(g[:, :LANES] + brg_ref[:, sl])
        i = jax.nn.sigmoid(g[:, LANES:] + big_ref[:, sl])
        nl = -lam_ref[:, sl]
        softplus = jnp.maximum(nl, 0.0) + jnp.log(1.0 + jnp.exp(-jnp.abs(nl)))
        log_a = (-LRU_C * r) * softplus
        a = jnp.exp(log_a)
        b = jnp.sqrt(1.0 - a * a) * (i * xb)
        for d in (1, 2, 4):
            valid = rowmod >= d
            b = jnp.where(valid, a * pltpu.roll(b, d, 0) + b, b)
            a = jnp.where(valid, a * pltpu.roll(a, d, 0), a)
        carry = h_ref[:, sl]
        hs = []
        for gi in range(ts // SUBLANES):
            rs = slice(gi * SUBLANES, (gi + 1) * SUBLANES)
            hg = a[rs] * carry + b[rs]
            carry = hg[SUBLANES - 1:SUBLANES]
            hs.append(hg)
        h_ref[:, sl] = carry
        hfull = jnp.concatenate(hs, axis=0)
        rec_ref[0, :, sl] = (hfull * jax.nn.gelu(lg_ref[0, :, sl])).astype(BF16)


def _lru(lx, lg, cw, cb, wg, brg, big, lam):
    B, S, C = lx.shape
    ts = TS_LRU
    tok = pl.BlockSpec((1, ts, C), lambda b, s: (b, s, 0))
    full = lambda a: pl.BlockSpec(a.shape, lambda b, s: (0,) * a.ndim)
    return pl.pallas_call(
        _lru_body,
        grid=(B, S // ts),
        in_specs=[tok, tok, full(cw), full(cb), full(wg), full(brg), full(big), full(lam)],
        out_specs=tok,
        out_shape=jax.ShapeDtypeStruct((B, S, C), BF16),
        scratch_shapes=[pltpu.VMEM((SUBLANES, C), F32), pltpu.VMEM((1, C), F32)],
        compiler_params=_cparams(("parallel", "arbitrary")),
        name="lru",
    )(lx, lg, cw, cb, wg, brg, big, lam)


def _merge_body(o_ref, rec_ref, mg_ref, x_ref, gate1_ref, shift2_ref, scale2_ref, g2_ref,
                woa_ref, wob_ref, wout_ref, wq_ref, keys_ref, x1_ref, h2_ref, sc_ref):
    D = x_ref.shape[-1]
    ya = _dot(o_ref[0], woa_ref[...])
    yb = _dot(rec_ref[0], wob_ref[...])
    mg = mg_ref[0]
    y = jax.nn.sigmoid(mg[:, :D]) * ya + jax.nn.sigmoid(mg[:, D:]) * yb
    x1 = x_ref[0] + gate1_ref[0] * _dot(y.astype(BF16), wout_ref[...])
    x1_ref[0] = x1
    h2 = x1 * lax.rsqrt(jnp.mean(x1 * x1, axis=-1, keepdims=True) + EPS) * g2_ref[...]
    h2 = h2 * (1.0 + scale2_ref[0]) + shift2_ref[0]
    h2b = h2.astype(BF16)
    h2_ref[0] = h2b
    qp = _dot(h2b, wq_ref[...]).astype(BF16)
    for g in range(keys_ref.shape[0]):
        sc_ref[g * N_KEYS:(g + 1) * N_KEYS, :] = _dot_nt(keys_ref[g], qp[:, g * LANES:(g + 1) * LANES])


def _merge(o, rec, mg, x, gate1, shift2, scale2, g2, woa, wob, wout, wq, keys):
    B, S, D = x.shape
    tm = TM_PROJ
    ns = S // tm
    NG = keys.shape[0]
    tok = lambda w: pl.BlockSpec((1, tm, w), lambda b, s: (b, s, 0))
    per_b = pl.BlockSpec((1, 1, D), lambda b, s: (b, 0, 0))
    full = lambda a: pl.BlockSpec(a.shape, lambda b, s: (0,) * a.ndim)
    return pl.pallas_call(
        _merge_body,
        grid=(B, ns),
        in_specs=[tok(o.shape[-1]), tok(D), tok(2 * D), tok(D), per_b, per_b, per_b, full(g2),
                  full(woa), full(wob), full(wout), full(wq), full(keys)],
        out_specs=[tok(D), tok(D), pl.BlockSpec((NG * N_KEYS, tm), lambda b, s: (0, b * ns + s))],
        out_shape=[jax.ShapeDtypeStruct((B, S, D), F32), jax.ShapeDtypeStruct((B, S, D), BF16),
                   jax.ShapeDtypeStruct((NG * N_KEYS, B * S), F32)],
        compiler_params=_cparams(("parallel", "parallel")),
        name="merge",
    )(o, rec, mg, x, gate1, shift2, scale2, g2, woa, wob, wout, wq, keys)


def _sort16_network():
    def merge(lo, hi, r):
        step = r * 2
        if step < hi - lo:
            yield from merge(lo, hi, step)
            yield from merge(lo + r, hi, step)
            yield from [(i, i + r) for i in range(lo + r, hi - r, step)]
        else:
            yield (lo, lo + r)

    def sort(lo, hi):
        if hi - lo >= 1:
            mid = lo + (hi - lo) // 2
            yield from sort(lo, mid)
            yield from sort(mid + 1, hi)
            yield from merge(lo, hi, 1)

    return tuple(sort(0, PEER_TOPK - 1))


_SORT16 = _sort16_network()


def _compare_exchange(v, i, j):
    v[i], v[j] = jnp.maximum(v[i], v[j]), jnp.minimum(v[i], v[j])


def _bitonic_merge16(v):
    d = PEER_TOPK // 2
    while d:
        for i in range(PEER_TOPK):
            if not i & d:
                _compare_exchange(v, i, i + d)
        d //= 2


def _merge_top16(best, other):
    z = [jnp.maximum(best[i], other[PEER_TOPK - 1 - i]) if PEER_TOPK - 1 - i < len(other) else best[i]
         for i in range(PEER_TOPK)]
    _bitonic_merge16(z)
    return z


def _top16_of_128(v):
    v = list(v)
    for i, j in _SORT16:
        _compare_exchange(v, i, j)
    shift = SUBLANES // 2
    while shift:
        v = _merge_top16(v, [pltpu.roll(x, shift, 0) for x in v])
        shift //= 2
    return v


def _select_body(sc_ref, r2_ref, e2_ref, c1_ref, e1_ref):
    K = PEER_TOPK
    H = PEER_HEADS
    nv = N_KEYS // SUBLANES
    tl = sc_ref.shape[1]
    sub = lax.broadcasted_iota(jnp.int32, (SUBLANES, tl), 0)

    def group(g):
        return [sc_ref[g * N_KEYS + k * SUBLANES:g * N_KEYS + (k + 1) * SUBLANES, :] for k in range(nv)]

    t1 = [_top16_of_128(group(2 * h)) for h in range(H)]
    t2 = [_top16_of_128(group(2 * h + 1)) for h in range(H)]

    def pack(reps):
        out = reps[0]
        for h in range(1, H):
            out = jnp.where(sub == h, reps[h], out)
        return out

    def unpack(x, h):
        return jnp.broadcast_to(x[h:h + 1, :], x.shape)

    t1p = [pack([t1[h][a] for h in range(H)]) for a in range(K)]
    t2p = [pack([t2[h][b] for h in range(H)]) for b in range(K)]
    cand = [[t1p[a] + t2p[b] for b in range(K // (a + 1))] for a in range(K)]
    best = list(cand[0])
    for a in range(1, SUBLANES):
        best = _merge_top16(best, cand[a])
    best = _merge_top16(best, [cand[a][0] for a in range(SUBLANES, K)])
    thr = best[K - 1]
    z = jnp.ones_like(thr)
    for r in range(1, K):
        z = z + jnp.exp(best[r] - best[0])
    inv_z = 1.0 / z
    phi = []
    for b in range(K):
        f = jnp.full_like(thr, float("inf"))
        for a in range(K // (b + 1)):
            f = jnp.where(cand[a][b] >= thr, t1p[a], f)
        phi.append(f)

    for h in range(H):
        phi_h = [unpack(f, h) for f in phi]
        inv_z_h = unpack(inv_z, h)
        c1, e1, r2, e2 = [], [], [], []
        for x in group(2 * h):
            cnt = jnp.zeros_like(x)
            for b in range(K):
                cnt = cnt + jnp.where(x >= phi_h[b], 1.0, 0.0)
            c1.append(cnt)
            e1.append(jnp.exp(x - t1[h][0]))
        for x in group(2 * h + 1):
            rank = jnp.zeros_like(x)
            for r in range(K):
                rank = rank + jnp.where(t2[h][r] > x, 1.0, 0.0)
            r2.append(rank)
            e2.append(jnp.exp(x - t2[h][0]) * inv_z_h)
        rows = slice(h * N_KEYS, (h + 1) * N_KEYS)
        c1_ref[rows, :] = jnp.concatenate(c1, axis=0)
        e1_ref[rows, :] = jnp.concatenate(e1, axis=0)
        r2_ref[rows, :] = jnp.concatenate(r2, axis=0).astype(BF16)
        e2_ref[rows, :] = jnp.concatenate(e2, axis=0).astype(BF16)


def _select(sc):
    R, T = sc.shape
    tl = TL_SEL
    half = R // 2
    blk = pl.BlockSpec((half, tl), lambda t: (0, t))
    return pl.pallas_call(
        _select_body,
        grid=(T // tl,),
        in_specs=[pl.BlockSpec((R, tl), lambda t: (0, t))],
        out_specs=[blk, blk, blk, blk],
        out_shape=[jax.ShapeDtypeStruct((half, T), BF16), jax.ShapeDtypeStruct((half, T), BF16),
                   jax.ShapeDtypeStruct((half, T), F32), jax.ShapeDtypeStruct((half, T), F32)],
        compiler_params=_cparams(("parallel",)),
        name="select",
    )(sc)


def _peer_body(h2_ref, u0_ref, un_ref, vt_ref, r2_ref, e2_ref, c1_ref, e1_ref, x1_ref, gate2_ref,
               out_ref, acc_ref, at_ref, w_ref):
    e = pl.program_id(1)
    tm = h2_ref.shape[0]
    n_i = un_ref.shape[0] // N_KEYS

    @pl.when(e == 0)
    def _():
        acc_ref[...] = jnp.zeros_like(acc_ref)
        at_ref[...] = _dot_nt(u0_ref[...], h2_ref[...]).astype(BF16)

    assert n_i == SUBLANES
    i0 = pl.multiple_of(e * n_i, SUBLANES)

    def row_bf16(ref, h, ii, ls):
        row = ref[pl.ds(h * N_KEYS + i0, SUBLANES), ls][ii:ii + 1]
        one = jnp.broadcast_to(row, (2 * SUBLANES, LANES)).astype(BF16)
        return jnp.concatenate([one] * (N_KEYS // (2 * SUBLANES)), axis=0)

    def routing_weights(lb):
        ls = slice(lb * LANES, (lb + 1) * LANES)
        for ii in range(n_i):
            rs = slice(ii * N_KEYS, (ii + 1) * N_KEYS)
            coef = None
            for h in range(PEER_HEADS):
                hs = slice(h * N_KEYS, (h + 1) * N_KEYS)
                e2 = e2_ref[hs, ls]
                sel = r2_ref[hs, ls] < row_bf16(c1_ref, h, ii, ls)
                term = jnp.where(sel, e2, jnp.zeros_like(e2)) * row_bf16(e1_ref, h, ii, ls)
                coef = term if coef is None else coef + term
            a = at_ref[rs, ls]
            cdf = 0.5 * jnp.tanh(a * (GELU_K0 + GELU_K1 * (a * a))) + 0.5
            w_ref[rs, ls] = coef * (a * cdf)

    half = tm // 2
    for ts in (slice(0, half), slice(half, tm)):
        for lb in range(ts.start // LANES, ts.stop // LANES):
            routing_weights(lb)
        at_ref[:, ts] = _dot_nt(un_ref[...], h2_ref[ts, :]).astype(BF16)
        acc_ref[:, ts] += _dot(vt_ref[...], w_ref[:, ts])

    @pl.when(e == pl.num_programs(1) - 1)
    def _():
        out_ref[...] = x1_ref[...] + gate2_ref[0] * acc_ref[...].T


def _peer(h2, u, vt, r2, e2, c1, e1, x1, gate2, seq_len):
    T, D = h2.shape
    NE = u.shape[0]
    tm, te = TM_PEER, TE_PEER
    tiles_per_seq = seq_len // tm
    table = pl.BlockSpec((r2.shape[0], tm), lambda t, e: (0, t))
    return pl.pallas_call(
        _peer_body,
        grid=(T // tm, NE // te),
        in_specs=[pl.BlockSpec((tm, D), lambda t, e: (t, 0)),
                  pl.BlockSpec((te, D), lambda t, e: (0, 0)),
                  pl.BlockSpec((te, D), lambda t, e: (jnp.minimum(e + 1, NE // te - 1), 0)),
                  pl.BlockSpec((D, te), lambda t, e: (0, e)),
                  table, table, table, table,
                  pl.BlockSpec((tm, D), lambda t, e: (t, 0)),
                  pl.BlockSpec((1, 1, D), lambda t, e: (t // tiles_per_seq, 0, 0))],
        out_specs=pl.BlockSpec((tm, D), lambda t, e: (t, 0)),
        out_shape=jax.ShapeDtypeStruct((T, D), F32),
        scratch_shapes=[pltpu.VMEM((D, tm), F32), pltpu.VMEM((te, tm), BF16), pltpu.VMEM((te, tm), BF16)],
        compiler_params=_cparams(("parallel", "arbitrary")),
        name="peer",
    )(h2, u, u, vt, r2, e2, c1, e1, x1, gate2)


def _pad_heads(w, real):
    K = w.shape[0]
    w = w.reshape(K, -1, real)
    return jnp.pad(w, ((0, 0), (0, 0), (0, LANES - real))).reshape(K, -1)


def _layer(x, cs_mod, positions, p, l):
    B, S, D = x.shape
    H = MLA_HEADS
    shift1, scale1, gate1, shift2, scale2, gate2 = [cs_mod[:, j].reshape(B, 1, D) for j in range(6)]
    w_in = p["w_in"][l]
    c0, c1, c2 = Q_LORA, Q_LORA + KV_LORA, Q_LORA + KV_LORA + QK_ROPE
    w_rope = jnp.pad(w_in[:, c1:c2], ((0, 0), (QK_NOPE, LANES - QK_HEAD)))
    wqkr = jnp.concatenate([w_in[:, :c1], w_rope], axis=1).astype(BF16)
    wlru = w_in[:, c2:c2 + 2 * D].astype(BF16)
    wmg = w_in[:, c2 + 2 * D:].astype(BF16)
    wuq = _pad_heads(p["w_uq"][l], QK_HEAD).astype(BF16)
    wukv = p["w_ukv"][l].reshape(KV_LORA, H, QK_NOPE + V_HEAD)
    wuk = jnp.pad(wukv[:, :, :QK_NOPE], ((0, 0), (0, 0), (0, LANES - QK_NOPE))).reshape(KV_LORA, H * LANES).astype(BF16)
    wuv = jnp.pad(wukv[:, :, QK_NOPE:], ((0, 0), (0, 0), (0, LANES - V_HEAD))).reshape(KV_LORA, H * LANES).astype(BF16)
    gqn = jnp.pad(p["q_norm_g"][l] * (QK_HEAD ** -0.5), (0, LANES - QK_HEAD)).reshape(1, LANES)
    gkn = jnp.pad(p["k_norm_g"][l], (0, LANES - QK_HEAD)).reshape(1, LANES)
    inv_freq = 1.0 / (ROPE_THETA ** (jnp.arange(0, QK_ROPE, 2, dtype=F32) / QK_ROPE))
    half = QK_ROPE // 2
    zeros_lo = jnp.zeros((QK_NOPE,), F32)
    zeros_hi = jnp.zeros((LANES - QK_HEAD,), F32)
    freq = jnp.concatenate([zeros_lo, inv_freq, inv_freq, zeros_hi]).reshape(1, LANES)
    sign = jnp.concatenate([zeros_lo, -jnp.ones((half,), F32), jnp.ones((half,), F32), zeros_hi]).reshape(1, LANES)

    q, k, v, lx, lg, mg = _in_proj(
        x, positions.reshape(B, S, 1), shift1, scale1, p["norm1_g"][l].reshape(1, D), wqkr, wlru, wmg,
        p["q_a_norm_g"][l].reshape(1, Q_LORA), wuq, p["kv_a_norm_g"][l].reshape(1, KV_LORA), wuk, wuv,
        gqn, gkn, freq, sign)
    o = _attention(q, k, v)

    C = p["conv_b"].shape[-1]
    wg = jnp.concatenate([p["w_rg"][l], p["w_ig"][l]], axis=-1).astype(BF16)
    rec = _lru(lx, lg, p["conv_w"][l], p["conv_b"][l].reshape(1, C), wg, p["b_rg"][l].reshape(1, C),
               p["b_ig"][l].reshape(1, C), p["lru_lambda"][l].reshape(1, C))

    woa = jnp.pad(p["w_o_attn"][l].reshape(H, V_HEAD, D), ((0, 0), (0, LANES - V_HEAD), (0, 0)))
    woa = woa.reshape(H * LANES, D).astype(BF16)
    keys = p["sub_keys"][l].reshape(PEER_HEADS * 2, N_KEYS, -1).astype(BF16)
    x1, h2, sc = _merge(o, rec, mg, x, gate1, shift2, scale2, p["norm2_g"][l].reshape(1, D), woa,
                        p["w_o_lru"][l].astype(BF16), p["w_out"][l].astype(BF16), p["w_query"][l].astype(BF16), keys)
    r2, e2, c1, e1 = _select(sc)
    out = _peer(h2.reshape(B * S, D), p["expert_u"][l].astype(BF16), p["expert_v"][l].T.astype(BF16),
                r2, e2, c1, e1, x1.reshape(B * S, D), gate2, S)
    return out.reshape(B, S, D)


def kernel(x, c, positions, w_ada, b_ada, norm1_g, w_in, q_a_norm_g, w_uq, kv_a_norm_g, w_ukv, q_norm_g, k_norm_g, w_o_attn, conv_w, conv_b, w_rg, b_rg, w_ig, b_ig, lru_lambda, w_o_lru, w_out, norm2_g, w_query, sub_keys, expert_u, expert_v):
    p = dict(w_in=w_in, norm1_g=norm1_g, q_a_norm_g=q_a_norm_g, w_uq=w_uq, kv_a_norm_g=kv_a_norm_g, w_ukv=w_ukv,
             q_norm_g=q_norm_g, k_norm_g=k_norm_g, w_o_attn=w_o_attn, conv_w=conv_w, conv_b=conv_b, w_rg=w_rg,
             b_rg=b_rg, w_ig=w_ig, b_ig=b_ig, lru_lambda=lru_lambda, w_o_lru=w_o_lru, w_out=w_out,
             norm2_g=norm2_g, w_query=w_query, sub_keys=sub_keys, expert_u=expert_u, expert_v=expert_v)
    B, D = c.shape
    for l in range(w_ada.shape[0]):
        mod = _mod(c, w_ada[l], b_ada[l]).reshape(B, 6, D)
        x = _layer(x, mod, positions, p, l)
    return x
```

```python
import functools

import jax
import jax.numpy as jnp
from jax import lax
from jax.experimental import pallas as pl
from jax.experimental.pallas import tpu as pltpu

F32 = jnp.float32
BF16 = jnp.bfloat16

EPS = 1e-6
LANES = 128
SUBLANES = 8
VMEM_LIMIT = 56 * 1024 * 1024

CHUNK = 64
MLA_HEADS = 8
QK_NOPE = 64
QK_ROPE = 32
QK_HEAD = QK_NOPE + QK_ROPE
V_HEAD = 64
Q_LORA = 256
KV_LORA = 128
ROPE_THETA = 10000.0
LRU_BLOCKS = 8
CONV_WIDTH = 4
LRU_C = 8.0
PEER_HEADS = 8
N_KEYS = 128
PEER_TOPK = 16
GELU_K0 = 0.7978845608028654
GELU_K1 = 0.044715 * GELU_K0

TM_PROJ = 256
TQ = 256
ATT_HEADS_PER_STEP = 4
TS_LRU = 256
TL_SEL = 128
TM_PEER = 512
TE_PEER = 1024


def _cparams(sem):
    return pltpu.CompilerParams(dimension_semantics=sem, vmem_limit_bytes=VMEM_LIMIT)


def _dot(a, b):
    return jnp.dot(a, b, preferred_element_type=F32)


def _dot_nt(a, b):
    return lax.dot_general(a, b, (((1,), (1,)), ((), ())), preferred_element_type=F32)


def _mod_body(c_ref, w_ref, b_ref, o_ref):
    c = c_ref[...]
    cs = c * jax.nn.sigmoid(c)
    o_ref[...] = _dot(cs.astype(BF16), w_ref[...].astype(BF16)) + b_ref[...]


def _mod(c, w_ada, b_ada):
    B, D = c.shape
    N = w_ada.shape[1]
    tn = 1024
    return pl.pallas_call(
        _mod_body,
        grid=(N // tn,),
        in_specs=[pl.BlockSpec((B, D), lambda n: (0, 0)),
                  pl.BlockSpec((D, tn), lambda n: (0, n)),
                  pl.BlockSpec((1, tn), lambda n: (0, n))],
        out_specs=pl.BlockSpec((B, tn), lambda n: (0, n)),
        out_shape=jax.ShapeDtypeStruct((B, N), F32),
        compiler_params=_cparams(("parallel",)),
        name="mod",
    )(c, w_ada, b_ada.reshape(1, N))


def _rope(t, cos_l, sin_l, lane):
    rot = jnp.where(lane < QK_NOPE + QK_ROPE // 2,
                    pltpu.roll(t, LANES - QK_ROPE // 2, 1), pltpu.roll(t, QK_ROPE // 2, 1))
    return t * cos_l + rot * sin_l


def _in_proj_body(x_ref, pos_ref, shift_ref, scale_ref, g1_ref, wqkr_ref, wlru_ref, wmg_ref,
                  gqa_ref, wuq_ref, gkva_ref, wuk_ref, wuv_ref, gqn_ref, gkn_ref, freq_ref, sign_ref,
                  q_ref, k_ref, v_ref, lx_ref, lg_ref, mg_ref):
    x = x_ref[0]
    D = x.shape[-1]
    y = x * lax.rsqrt(jnp.mean(x * x, axis=-1, keepdims=True) + EPS) * g1_ref[...]
    h = y * (1.0 + scale_ref[0]) + shift_ref[0]
    hb = h.astype(BF16)

    lru = _dot(hb, wlru_ref[...])
    lx_ref[0] = lru[:, :D]
    lg_ref[0] = lru[:, D:]
    mg_ref[0] = _dot(hb, wmg_ref[...])

    qkr = _dot(hb, wqkr_ref[...])
    cq = qkr[:, :Q_LORA]
    ckv = qkr[:, Q_LORA:Q_LORA + KV_LORA]
    kr = qkr[:, Q_LORA + KV_LORA:]
    qn = cq * lax.rsqrt(jnp.mean(cq * cq, axis=-1, keepdims=True) + EPS) * gqa_ref[...]
    kvn = ckv * lax.rsqrt(jnp.mean(ckv * ckv, axis=-1, keepdims=True) + EPS) * gkva_ref[...]
    kvb = kvn.astype(BF16)
    q_all = _dot(qn.astype(BF16), wuq_ref[...])
    k_all = _dot(kvb, wuk_ref[...])
    v_ref[0] = _dot(kvb, wuv_ref[...]).astype(BF16)

    ang = pos_ref[0].astype(F32) * freq_ref[...]
    cos_l = jnp.cos(ang)
    sin_l = jnp.sin(ang) * sign_ref[...]
    lane = lax.broadcasted_iota(jnp.int32, ang.shape, 1)
    inv_n = 1.0 / QK_HEAD
    for hh in range(MLA_HEADS):
        sl = slice(hh * LANES, (hh + 1) * LANES)
        qh = q_all[:, sl]
        qh = qh * lax.rsqrt(jnp.sum(qh * qh, axis=-1, keepdims=True) * inv_n + EPS) * gqn_ref[...]
        q_ref[0, :, sl] = _rope(qh, cos_l, sin_l, lane).astype(BF16)
        kh = k_all[:, sl] + kr
        kh = kh * lax.rsqrt(jnp.sum(kh * kh, axis=-1, keepdims=True) * inv_n + EPS) * gkn_ref[...]
        k_ref[0, :, sl] = _rope(kh, cos_l, sin_l, lane).astype(BF16)


def _in_proj(x, pos3, shift1, scale1, g1, wqkr, wlru, wmg, gqa, wuq, gkva, wuk, wuv, gqn, gkn, freq, sign):
    B, S, D = x.shape
    tm = TM_PROJ
    HP = MLA_HEADS * LANES
    tok = lambda w: pl.BlockSpec((1, tm, w), lambda b, s: (b, s, 0))
    per_b = pl.BlockSpec((1, 1, D), lambda b, s: (b, 0, 0))
    full = lambda a: pl.BlockSpec(a.shape, lambda b, s: (0,) * a.ndim)
    return pl.pallas_call(
        _in_proj_body,
        grid=(B, S // tm),
        in_specs=[tok(D), tok(1), per_b, per_b, full(g1), full(wqkr), full(wlru), full(wmg),
                  full(gqa), full(wuq), full(gkva), full(wuk), full(wuv), full(gqn), full(gkn),
                  full(freq), full(sign)],
        out_specs=[tok(HP), tok(HP), tok(HP), tok(D), tok(D), tok(2 * D)],
        out_shape=[jax.ShapeDtypeStruct((B, S, HP), BF16)] * 3
        + [jax.ShapeDtypeStruct((B, S, D), F32)] * 2 + [jax.ShapeDtypeStruct((B, S, 2 * D), F32)],
        compiler_params=_cparams(("parallel", "parallel")),
        name="in_proj",
    )(x, pos3, shift1, scale1, g1, wqkr, wlru, wmg, gqa, wuq, gkva, wuk, wuv, gqn, gkn, freq, sign)


def _attn_body(q_ref, k_ref, v_ref, o_ref):
    qi = pl.program_id(2)
    neg = float(jnp.finfo(jnp.float32).min)
    shift = CHUNK.bit_length() - 1
    qc = jnp.right_shift(lax.broadcasted_iota(jnp.int32, (TQ, TQ), 0), shift)
    kc = jnp.right_shift(lax.broadcasted_iota(jnp.int32, (TQ, TQ), 1), shift)
    allowed = kc <= qc

    for nk in range(1, k_ref.shape[1] // TQ + 1):
        @pl.when(qi == nk - 1)
        def _(nk=nk):
            n_keys = nk * TQ
            for hh in range(ATT_HEADS_PER_STEP):
                hl = slice(hh * LANES, (hh + 1) * LANES)
                q = q_ref[0, :, hl]
                s = _dot_nt(q, k_ref[0, :n_keys, hl])
                diag = jnp.where(allowed, s[:, n_keys - TQ:], neg)
                s = diag if nk == 1 else jnp.concatenate([s[:, :n_keys - TQ], diag], axis=1)
                p = jnp.exp(s - jnp.max(s, axis=-1, keepdims=True))
                l = jnp.sum(p, axis=-1, keepdims=True)
                o_ref[0, :, hl] = (_dot(p.astype(BF16), v_ref[0, :n_keys, hl]) / l).astype(BF16)


def _attention(q, k, v):
    B, S, HP = q.shape
    hw = ATT_HEADS_PER_STEP * LANES
    H = HP // hw
    return pl.pallas_call(
        _attn_body,
        grid=(B, H, S // TQ),
        in_specs=[pl.BlockSpec((1, TQ, hw), lambda b, h, i: (b, i, h)),
                  pl.BlockSpec((1, S, hw), lambda b, h, i: (b, 0, h)),
                  pl.BlockSpec((1, S, hw), lambda b, h, i: (b, 0, h))],
        out_specs=pl.BlockSpec((1, TQ, hw), lambda b, h, i: (b, i, h)),
        out_shape=jax.ShapeDtypeStruct((B, S, HP), BF16),
        compiler_params=_cparams(("parallel", "parallel", "arbitrary")),
        name="attn",
    )(q, k, v)


def _lru_body(lx_ref, lg_ref, cw_ref, cb_ref, wg_ref, brg_ref, big_ref, lam_ref, rec_ref, prev_ref, h_ref):
    si = pl.program_id(1)

    @pl.when(si == 0)
    def _():
        prev_ref[...] = jnp.zeros_like(prev_ref)
        h_ref[...] = jnp.zeros_like(h_ref)

    x = lx_ref[0]
    ts, C = x.shape
    prev8 = prev_ref[...]
    row8 = lax.broadcasted_iota(jnp.int32, prev8.shape, 0)
    xc = x * cw_ref[CONV_WIDTH - 1:CONV_WIDTH, :] + cb_ref[...]
    for d in range(1, CONV_WIDTH):
        xr = pltpu.roll(x, d, 0)
        first = jnp.where(row8 < d, pltpu.roll(prev8, d, 0), xr[:SUBLANES])
        xs = jnp.concatenate([first, xr[SUBLANES:]], axis=0)
        xc = xc + xs * cw_ref[CONV_WIDTH - 1 - d:CONV_WIDTH - d, :]
    prev_ref[...] = x[ts - SUBLANES:]

    xcb = xc.astype(BF16)
    rowmod = jnp.bitwise_and(lax.broadcasted_iota(jnp.int32, (ts, LANES), 0), SUBLANES - 1)
    for blk in range(LRU_BLOCKS):
        sl = slice(blk * LANES, (blk + 1) * LANES)
        xb = xc[:, sl]
        g = _dot(xcb[:, sl], wg_ref[blk])
        r = jax.nn.sigmoid(g[:, :LANES] + brg_ref[:, sl])
        i = jax.nn.sigmoid(g[:, LANES:] + big_ref[:, sl])
        nl = -lam_ref[:, sl]
        softplus = jnp.maximum(nl, 0.0) + jnp.log(1.0 + jnp.exp(-jnp.abs(nl)))
        log_a = (-LRU_C * r) * softplus
        a = jnp.exp(log_a)
        b = jnp.sqrt(1.0 - a * a) * (i * xb)
        for d in (1, 2, 4):
            valid = rowmod >= d
            b = jnp.where(valid, a * pltpu.roll(b, d, 0) + b, b)
            a = jnp.where(valid, a * pltpu.roll(a, d, 0), a)
        carry = h_ref[:, sl]
        hs = []
        for gi in range(ts // SUBLANES):
            rs = slice(gi * SUBLANES, (gi + 1) * SUBLANES)
            hg = a[rs] * carry + b[rs]
            carry = hg[SUBLANES - 1:SUBLANES]
            hs.append(hg)
        h_ref[:, sl] = carry
        hfull = jnp.concatenate(hs, axis=0)
        rec_ref[0, :, sl] = (hfull * jax.nn.gelu(lg_ref[0, :, sl])).astype(BF16)


def _lru(lx, lg, cw, cb, wg, brg, big, lam):
    B, S, C = lx.shape
    ts = TS_LRU
    tok = pl.BlockSpec((1, ts, C), lambda b, s: (b, s, 0))
    full = lambda a: pl.BlockSpec(a.shape, lambda b, s: (0,) * a.ndim)
    return pl.pallas_call(
        _lru_body,
        grid=(B, S // ts),
        in_specs=[tok, tok, full(cw), full(cb), full(wg), full(brg), full(big), full(lam)],
        out_specs=tok,
        out_shape=jax.ShapeDtypeStruct((B, S, C), BF16),
        scratch_shapes=[pltpu.VMEM((SUBLANES, C), F32), pltpu.VMEM((1, C), F32)],
        compiler_params=_cparams(("parallel", "arbitrary")),
        name="lru",
    )(lx, lg, cw, cb, wg, brg, big, lam)


def _merge_body(o_ref, rec_ref, mg_ref, x_ref, gate1_ref, shift2_ref, scale2_ref, g2_ref,
                woa_ref, wob_ref, wout_ref, wq_ref, keys_ref, x1_ref, h2_ref, sc_ref):
    D = x_ref.shape[-1]
    ya = _dot(o_ref[0], woa_ref[...])
    yb = _dot(rec_ref[0], wob_ref[...])
    mg = mg_ref[0]
    y = jax.nn.sigmoid(mg[:, :D]) * ya + jax.nn.sigmoid(mg[:, D:]) * yb
    x1 = x_ref[0] + gate1_ref[0] * _dot(y.astype(BF16), wout_ref[...])
    x1_ref[0] = x1
    h2 = x1 * lax.rsqrt(jnp.mean(x1 * x1, axis=-1, keepdims=True) + EPS) * g2_ref[...]
    h2 = h2 * (1.0 + scale2_ref[0]) + shift2_ref[0]
    h2b = h2.astype(BF16)
    h2_ref[...] = h2.T.astype(BF16)
    qp = _dot(h2b, wq_ref[...]).astype(BF16)
    for g in range(keys_ref.shape[0]):
        sc_ref[g * N_KEYS:(g + 1) * N_KEYS, :] = _dot_nt(keys_ref[g], qp[:, g * LANES:(g + 1) * LANES])


def _merge(o, rec, mg, x, gate1, shift2, scale2, g2, woa, wob, wout, wq, keys):
    B, S, D = x.shape
    tm = TM_PROJ
    ns = S // tm
    NG = keys.shape[0]
    tok = lambda w: pl.BlockSpec((1, tm, w), lambda b, s: (b, s, 0))
    per_b = pl.BlockSpec((1, 1, D), lambda b, s: (b, 0, 0))
    full = lambda a: pl.BlockSpec(a.shape, lambda b, s: (0,) * a.ndim)
    return pl.pallas_call(
        _merge_body,
        grid=(B, ns),
        in_specs=[tok(o.shape[-1]), tok(D), tok(2 * D), tok(D), per_b, per_b, per_b, full(g2),
                  full(woa), full(wob), full(wout), full(wq), full(keys)],
        out_specs=[tok(D), pl.BlockSpec((D, tm), lambda b, s: (0, b * ns + s)),
                   pl.BlockSpec((NG * N_KEYS, tm), lambda b, s: (0, b * ns + s))],
        out_shape=[jax.ShapeDtypeStruct((B, S, D), F32), jax.ShapeDtypeStruct((D, B * S), BF16),
                   jax.ShapeDtypeStruct((NG * N_KEYS, B * S), F32)],
        compiler_params=_cparams(("parallel", "parallel")),
        name="merge",
    )(o, rec, mg, x, gate1, shift2, scale2, g2, woa, wob, wout, wq, keys)


def _sort16_network():
    def merge(lo, hi, r):
        step = r * 2
        if step < hi - lo:
            yield from merge(lo, hi, step)
            yield from merge(lo + r, hi, step)
            yield from [(i, i + r) for i in range(lo + r, hi - r, step)]
        else:
            yield (lo, lo + r)

    def sort(lo, hi):
        if hi - lo >= 1:
            mid = lo + (hi - lo) // 2
            yield from sort(lo, mid)
            yield from sort(mid + 1, hi)
            yield from merge(lo, hi, 1)

    return tuple(sort(0, PEER_TOPK - 1))


_SORT16 = _sort16_network()


def _compare_exchange(v, i, j):
    v[i], v[j] = jnp.maximum(v[i], v[j]), jnp.minimum(v[i], v[j])


def _bitonic_merge16(v):
    d = PEER_TOPK // 2
    while d:
        for i in range(PEER_TOPK):
            if not i & d:
                _compare_exchange(v, i, i + d)
        d //= 2


def _merge_top16(best, other):
    z = [jnp.maximum(best[i], other[PEER_TOPK - 1 - i]) if PEER_TOPK - 1 - i < len(other) else best[i]
         for i in range(PEER_TOPK)]
    _bitonic_merge16(z)
    return z


def _top16_of_128(v):
    v = list(v)
    for i, j in _SORT16:
        _compare_exchange(v, i, j)
    shift = SUBLANES // 2
    while shift:
        v = _merge_top16(v, [pltpu.roll(x, shift, 0) for x in v])
        shift //= 2
    return v


def _select_body(sc_ref, r2_ref, e2_ref, c1_ref, e1_ref):
    K = PEER_TOPK
    H = PEER_HEADS
    nv = N_KEYS // SUBLANES
    tl = sc_ref.shape[1]
    sub = lax.broadcasted_iota(jnp.int32, (SUBLANES, tl), 0)

    def group(g):
        return [sc_ref[g * N_KEYS + k * SUBLANES:g * N_KEYS + (k + 1) * SUBLANES, :] for k in range(nv)]

    t1 = [_top16_of_128(group(2 * h)) for h in range(H)]
    t2 = [_top16_of_128(group(2 * h + 1)) for h in range(H)]

    def pack(reps):
        out = reps[0]
        for h in range(1, H):
            out = jnp.where(sub == h, reps[h], out)
        return out

    def unpack(x, h):
        return jnp.broadcast_to(x[h:h + 1, :], x.shape)

    t1p = [pack([t1[h][a] for h in range(H)]) for a in range(K)]
    t2p = [pack([t2[h][b] for h in range(H)]) for b in range(K)]
    cand = [[t1p[a] + t2p[b] for b in range(K // (a + 1))] for a in range(K)]
    best = list(cand[0])
    for a in range(1, SUBLANES):
        best = _merge_top16(best, cand[a])
    best = _merge_top16(best, [cand[a][0] for a in range(SUBLANES, K)])
    thr = best[K - 1]
    z = jnp.ones_like(thr)
    for r in range(1, K):
        z = z + jnp.exp(best[r] - best[0])
    inv_z = 1.0 / z
    phi = []
    for b in range(K):
        f = jnp.full_like(thr, float("inf"))
        for a in range(K // (b + 1)):
            f = jnp.where(cand[a][b] >= thr, t1p[a], f)
        phi.append(f)

    for h in range(H):
        phi_h = [unpack(f, h) for f in phi]
        inv_z_h = unpack(inv_z, h)
        c1, e1, r2, e2 = [], [], [], []
        for x in group(2 * h):
            cnt = jnp.zeros_like(x)
            for b in range(K):
                cnt = cnt + jnp.where(x >= phi_h[b], 1.0, 0.0)
            c1.append(cnt)
            e1.append(jnp.exp(x - t1[h][0]))
        for x in group(2 * h + 1):
            rank = jnp.zeros_like(x)
            for r in range(K):
                rank = rank + jnp.where(t2[h][r] > x, 1.0, 0.0)
            r2.append(rank)
            e2.append(jnp.exp(x - t2[h][0]) * inv_z_h)
        rows = slice(h * N_KEYS, (h + 1) * N_KEYS)
        c1_ref[rows, :] = jnp.concatenate(c1, axis=0)
        e1_ref[rows, :] = jnp.concatenate(e1, axis=0)
        r2_ref[rows, :] = jnp.concatenate(r2, axis=0).astype(BF16)
        e2_ref[rows, :] = jnp.concatenate(e2, axis=0).astype(BF16)


def _select(sc):
    R, T = sc.shape
    tl = TL_SEL
    half = R // 2
    blk = pl.BlockSpec((half, tl), lambda t: (0, t))
    return pl.pallas_call(
        _select_body,
        grid=(T // tl,),
        in_specs=[pl.BlockSpec((R, tl), lambda t: (0, t))],
        out_specs=[blk, blk, blk, blk],
        out_shape=[jax.ShapeDtypeStruct((half, T), BF16), jax.ShapeDtypeStruct((half, T), BF16),
                   jax.ShapeDtypeStruct((half, T), F32), jax.ShapeDtypeStruct((half, T), F32)],
        compiler_params=_cparams(("parallel",)),
        name="select",
    )(sc)


def _peer_body(h2t_ref, u0_ref, un_ref, vt_ref, r2_ref, e2_ref, c1_ref, e1_ref, x1_ref, gate2_ref,
               out_ref, acc_ref, at_ref, w_ref):
    e = pl.program_id(1)
    tm = h2t_ref.shape[1]
    n_i = un_ref.shape[0] // N_KEYS

    @pl.when(e == 0)
    def _():
        acc_ref[...] = jnp.zeros_like(acc_ref)
        at_ref[...] = _dot(u0_ref[...], h2t_ref[...]).astype(BF16)

    assert n_i == SUBLANES
    i0 = pl.multiple_of(e * n_i, SUBLANES)

    def row_bf16(ref, h, ii, ls):
        row = ref[pl.ds(h * N_KEYS + i0, SUBLANES), ls][ii:ii + 1]
        one = jnp.broadcast_to(row, (2 * SUBLANES, LANES)).astype(BF16)
        return jnp.concatenate([one] * (N_KEYS // (2 * SUBLANES)), axis=0)

    def routing_weights(lb):
        ls = slice(lb * LANES, (lb + 1) * LANES)
        for ii in range(n_i):
            rs = slice(ii * N_KEYS, (ii + 1) * N_KEYS)
            coef = None
            for h in range(PEER_HEADS):
                hs = slice(h * N_KEYS, (h + 1) * N_KEYS)
                e2 = e2_ref[hs, ls]
                sel = r2_ref[hs, ls] < row_bf16(c1_ref, h, ii, ls)
                term = jnp.where(sel, e2, jnp.zeros_like(e2)) * row_bf16(e1_ref, h, ii, ls)
                coef = term if coef is None else coef + term
            a = at_ref[rs, ls]
            cdf = 0.5 * jnp.tanh(a * (GELU_K0 + GELU_K1 * (a * a))) + 0.5
            w_ref[rs, ls] = coef * (a * cdf)

    half = tm // 2
    for ts in (slice(0, half), slice(half, tm)):
        for lb in range(ts.start // LANES, ts.stop // LANES):
            routing_weights(lb)
        at_ref[:, ts] = _dot(un_ref[...], h2t_ref[:, ts]).astype(BF16)
        acc_ref[:, ts] += _dot(vt_ref[...], w_ref[:, ts])

    @pl.when(e == pl.num_programs(1) - 1)
    def _():
        out_ref[...] = x1_ref[...] + gate2_ref[0] * acc_ref[...].T


def _peer(h2t, u, vt, r2, e2, c1, e1, x1, gate2, seq_len):
    D, T = h2t.shape
    NE = u.shape[0]
    tm, te = TM_PEER, TE_PEER
    tiles_per_seq = seq_len // tm
    table = pl.BlockSpec((r2.shape[0], tm), lambda t, e: (0, t))
    return pl.pallas_call(
        _peer_body,
        grid=(T // tm, NE // te),
        in_specs=[pl.BlockSpec((D, tm), lambda t, e: (0, t)),
                  pl.BlockSpec((te, D), lambda t, e: (0, 0)),
                  pl.BlockSpec((te, D), lambda t, e: (jnp.minimum(e + 1, NE // te - 1), 0)),
                  pl.BlockSpec((D, te), lambda t, e: (0, e)),
                  table, table, table, table,
                  pl.BlockSpec((tm, D), lambda t, e: (t, 0)),
                  pl.BlockSpec((1, 1, D), lambda t, e: (t // tiles_per_seq, 0, 0))],
        out_specs=pl.BlockSpec((tm, D), lambda t, e: (t, 0)),
        out_shape=jax.ShapeDtypeStruct((T, D), F32),
        scratch_shapes=[pltpu.VMEM((D, tm), F32), pltpu.VMEM((te, tm), BF16), pltpu.VMEM((te, tm), BF16)],
        compiler_params=_cparams(("parallel", "arbitrary")),
        name="peer",
    )(h2t, u, u, vt, r2, e2, c1, e1, x1, gate2)


def _pad_heads(w, real):
    K = w.shape[0]
    w = w.reshape(K, -1, real)
    return jnp.pad(w, ((0, 0), (0, 0), (0, LANES - real))).reshape(K, -1)


def _layer(x, cs_mod, positions, p, l):
    B, S, D = x.shape
    H = MLA_HEADS
    shift1, scale1, gate1, shift2, scale2, gate2 = [cs_mod[:, j].reshape(B, 1, D) for j in range(6)]
    w_in = p["w_in"][l]
    c0, c1, c2 = Q_LORA, Q_LORA + KV_LORA, Q_LORA + KV_LORA + QK_ROPE
    w_rope = jnp.pad(w_in[:, c1:c2], ((0, 0), (QK_NOPE, LANES - QK_HEAD)))
    wqkr = jnp.concatenate([w_in[:, :c1], w_rope], axis=1).astype(BF16)
    wlru = w_in[:, c2:c2 + 2 * D].astype(BF16)
    wmg = w_in[:, c2 + 2 * D:].astype(BF16)
    wuq = _pad_heads(p["w_uq"][l], QK_HEAD).astype(BF16)
    wukv = p["w_ukv"][l].reshape(KV_LORA, H, QK_NOPE + V_HEAD)
    wuk = jnp.pad(wukv[:, :, :QK_NOPE], ((0, 0), (0, 0), (0, LANES - QK_NOPE))).reshape(KV_LORA, H * LANES).astype(BF16)
    wuv = jnp.pad(wukv[:, :, QK_NOPE:], ((0, 0), (0, 0), (0, LANES - V_HEAD))).reshape(KV_LORA, H * LANES).astype(BF16)
    gqn = jnp.pad(p["q_norm_g"][l] * (QK_HEAD ** -0.5), (0, LANES - QK_HEAD)).reshape(1, LANES)
    gkn = jnp.pad(p["k_norm_g"][l], (0, LANES - QK_HEAD)).reshape(1, LANES)
    inv_freq = 1.0 / (ROPE_THETA ** (jnp.arange(0, QK_ROPE, 2, dtype=F32) / QK_ROPE))
    half = QK_ROPE // 2
    zeros_lo = jnp.zeros((QK_NOPE,), F32)
    zeros_hi = jnp.zeros((LANES - QK_HEAD,), F32)
    freq = jnp.concatenate([zeros_lo, inv_freq, inv_freq, zeros_hi]).reshape(1, LANES)
    sign = jnp.concatenate([zeros_lo, -jnp.ones((half,), F32), jnp.ones((half,), F32), zeros_hi]).reshape(1, LANES)

    q, k, v, lx, lg, mg = _in_proj(
        x, positions.reshape(B, S, 1), shift1, scale1, p["norm1_g"][l].reshape(1, D), wqkr, wlru, wmg,
        p["q_a_norm_g"][l].reshape(1, Q_LORA), wuq, p["kv_a_norm_g"][l].reshape(1, KV_LORA), wuk, wuv,
        gqn, gkn, freq, sign)
    o = _attention(q, k, v)

    C = p["conv_b"].shape[-1]
    wg = jnp.concatenate([p["w_rg"][l], p["w_ig"][l]], axis=-1).astype(BF16)
    rec = _lru(lx, lg, p["conv_w"][l], p["conv_b"][l].reshape(1, C), wg, p["b_rg"][l].reshape(1, C),
               p["b_ig"][l].reshape(1, C), p["lru_lambda"][l].reshape(1, C))

    woa = jnp.pad(p["w_o_attn"][l].reshape(H, V_HEAD, D), ((0, 0), (0, LANES - V_HEAD), (0, 0)))
    woa = woa.reshape(H * LANES, D).astype(BF16)
    keys = p["sub_keys"][l].reshape(PEER_HEADS * 2, N_KEYS, -1).astype(BF16)
    x1, h2t, sc = _merge(o, rec, mg, x, gate1, shift2, scale2, p["norm2_g"][l].reshape(1, D), woa,
                        p["w_o_lru"][l].astype(BF16), p["w_out"][l].astype(BF16), p["w_query"][l].astype(BF16), keys)
    r2, e2, c1, e1 = _select(sc)
    out = _peer(h2t, p["expert_u"][l].astype(BF16), p["expert_v"][l].T.astype(BF16),
                r2, e2, c1, e1, x1.reshape(B * S, D), gate2, S)
    return out.reshape(B, S, D)


def kernel(x, c, positions, w_ada, b_ada, norm1_g, w_in, q_a_norm_g, w_uq, kv_a_norm_g, w_ukv, q_norm_g, k_norm_g, w_o_attn, conv_w, conv_b, w_rg, b_rg, w_ig, b_ig, lru_lambda, w_o_lru, w_out, norm2_g, w_query, sub_keys, expert_u, expert_v):
    p = dict(w_in=w_in, norm1_g=norm1_g, q_a_norm_g=q_a_norm_g, w_uq=w_uq, kv_a_norm_g=kv_a_norm_g, w_ukv=w_ukv,
             q_norm_g=q_norm_g, k_norm_g=k_norm_g, w_o_attn=w_o_attn, conv_w=conv_w, conv_b=conv_b, w_rg=w_rg,
             b_rg=b_rg, w_ig=w_ig, b_ig=b_ig, lru_lambda=lru_lambda, w_o_lru=w_o_lru, w_out=w_out,
             norm2_g=norm2_g, w_query=w_query, sub_keys=sub_keys, expert_u=expert_u, expert_v=expert_v)
    B, D = c.shape
    for l in range(w_ada.shape[0]):
        mod = _mod(c, w_ada[l], b_ada[l]).reshape(B, 6, D)
        x = _layer(x, mod, positions, p, l)
    return x
```

```python
import functools

import jax
import jax.numpy as jnp
from jax import lax
from jax.experimental import pallas as pl
from jax.experimental.pallas import tpu as pltpu

F32 = jnp.float32
BF16 = jnp.bfloat16

EPS = 1e-6
LANES = 128
SUBLANES = 8
VMEM_LIMIT = 56 * 1024 * 1024

CHUNK = 64
MLA_HEADS = 8
QK_NOPE = 64
QK_ROPE = 32
QK_HEAD = QK_NOPE + QK_ROPE
V_HEAD = 64
Q_LORA = 256
KV_LORA = 128
ROPE_THETA = 10000.0
LRU_BLOCKS = 8
CONV_WIDTH = 4
LRU_C = 8.0
PEER_HEADS = 8
N_KEYS = 128
PEER_TOPK = 16
GELU_K0 = 0.7978845608028654
GELU_K1 = 0.044715 * GELU_K0

TM_PROJ = 256
TQ = 256
ATT_HEADS_PER_STEP = 4
TS_LRU = 256
TL_SEL = 128
TM_PEER = 512
TE_PEER = 1024
PEER_MM_DTYPE = jnp.float8_e4m3fn
PEER_U_SCALE = 32.0


def _cparams(sem):
    return pltpu.CompilerParams(dimension_semantics=sem, vmem_limit_bytes=VMEM_LIMIT)


def _dot(a, b):
    return jnp.dot(a, b, preferred_element_type=F32)


def _dot_nt(a, b):
    return lax.dot_general(a, b, (((1,), (1,)), ((), ())), preferred_element_type=F32)


def _mod_body(c_ref, w_ref, b_ref, o_ref):
    c = c_ref[...]
    cs = c * jax.nn.sigmoid(c)
    o_ref[...] = _dot(cs.astype(BF16), w_ref[...].astype(BF16)) + b_ref[...]


def _mod(c, w_ada, b_ada):
    B, D = c.shape
    N = w_ada.shape[1]
    tn = 1024
    return pl.pallas_call(
        _mod_body,
        grid=(N // tn,),
        in_specs=[pl.BlockSpec((B, D), lambda n: (0, 0)),
                  pl.BlockSpec((D, tn), lambda n: (0, n)),
                  pl.BlockSpec((1, tn), lambda n: (0, n))],
        out_specs=pl.BlockSpec((B, tn), lambda n: (0, n)),
        out_shape=jax.ShapeDtypeStruct((B, N), F32),
        compiler_params=_cparams(("parallel",)),
        name="mod",
    )(c, w_ada, b_ada.reshape(1, N))


def _rope(t, cos_l, sin_l, lane):
    rot = jnp.where(lane < QK_NOPE + QK_ROPE // 2,
                    pltpu.roll(t, LANES - QK_ROPE // 2, 1), pltpu.roll(t, QK_ROPE // 2, 1))
    return t * cos_l + rot * sin_l


def _in_proj_body(x_ref, pos_ref, shift_ref, scale_ref, g1_ref, wqkr_ref, wlru_ref, wmg_ref,
                  gqa_ref, wuq_ref, gkva_ref, wuk_ref, wuv_ref, gqn_ref, gkn_ref, freq_ref, sign_ref,
                  q_ref, k_ref, v_ref, lx_ref, lg_ref, mg_ref):
    x = x_ref[0]
    D = x.shape[-1]
    y = x * lax.rsqrt(jnp.mean(x * x, axis=-1, keepdims=True) + EPS) * g1_ref[...]
    h = y * (1.0 + scale_ref[0]) + shift_ref[0]
    hb = h.astype(BF16)

    lru = _dot(hb, wlru_ref[...])
    lx_ref[0] = lru[:, :D]
    lg_ref[0] = lru[:, D:]
    mg_ref[0] = _dot(hb, wmg_ref[...])

    qkr = _dot(hb, wqkr_ref[...])
    cq = qkr[:, :Q_LORA]
    ckv = qkr[:, Q_LORA:Q_LORA + KV_LORA]
    kr = qkr[:, Q_LORA + KV_LORA:]
    qn = cq * lax.rsqrt(jnp.mean(cq * cq, axis=-1, keepdims=True) + EPS) * gqa_ref[...]
    kvn = ckv * lax.rsqrt(jnp.mean(ckv * ckv, axis=-1, keepdims=True) + EPS) * gkva_ref[...]
    kvb = kvn.astype(BF16)
    q_all = _dot(qn.astype(BF16), wuq_ref[...])
    k_all = _dot(kvb, wuk_ref[...])
    v_ref[0] = _dot(kvb, wuv_ref[...]).astype(BF16)

    ang = pos_ref[0].astype(F32) * freq_ref[...]
    cos_l = jnp.cos(ang)
    sin_l = jnp.sin(ang) * sign_ref[...]
    lane = lax.broadcasted_iota(jnp.int32, ang.shape, 1)
    inv_n = 1.0 / QK_HEAD
    for hh in range(MLA_HEADS):
        sl = slice(hh * LANES, (hh + 1) * LANES)
        qh = q_all[:, sl]
        qh = qh * lax.rsqrt(jnp.sum(qh * qh, axis=-1, keepdims=True) * inv_n + EPS) * gqn_ref[...]
        q_ref[0, :, sl] = _rope(qh, cos_l, sin_l, lane).astype(BF16)
        kh = k_all[:, sl] + kr
        kh = kh * lax.rsqrt(jnp.sum(kh * kh, axis=-1, keepdims=True) * inv_n + EPS) * gkn_ref[...]
        k_ref[0, :, sl] = _rope(kh, cos_l, sin_l, lane).astype(BF16)


def _in_proj(x, pos3, shift1, scale1, g1, wqkr, wlru, wmg, gqa, wuq, gkva, wuk, wuv, gqn, gkn, freq, sign):
    B, S, D = x.shape
    tm = TM_PROJ
    HP = MLA_HEADS * LANES
    tok = lambda w: pl.BlockSpec((1, tm, w), lambda b, s: (b, s, 0))
    per_b = pl.BlockSpec((1, 1, D), lambda b, s: (b, 0, 0))
    full = lambda a: pl.BlockSpec(a.shape, lambda b, s: (0,) * a.ndim)
    return pl.pallas_call(
        _in_proj_body,
        grid=(B, S // tm),
        in_specs=[tok(D), tok(1), per_b, per_b, full(g1), full(wqkr), full(wlru), full(wmg),
                  full(gqa), full(wuq), full(gkva), full(wuk), full(wuv), full(gqn), full(gkn),
                  full(freq), full(sign)],
        out_specs=[tok(HP), tok(HP), tok(HP), tok(D), tok(D), tok(2 * D)],
        out_shape=[jax.ShapeDtypeStruct((B, S, HP), BF16)] * 3
        + [jax.ShapeDtypeStruct((B, S, D), F32)] * 2 + [jax.ShapeDtypeStruct((B, S, 2 * D), F32)],
        compiler_params=_cparams(("parallel", "parallel")),
        name="in_proj",
    )(x, pos3, shift1, scale1, g1, wqkr, wlru, wmg, gqa, wuq, gkva, wuk, wuv, gqn, gkn, freq, sign)


def _attn_body(q_ref, k_ref, v_ref, o_ref):
    qi = pl.program_id(2)
    neg = float(jnp.finfo(jnp.float32).min)
    shift = CHUNK.bit_length() - 1
    qc = jnp.right_shift(lax.broadcasted_iota(jnp.int32, (TQ, TQ), 0), shift)
    kc = jnp.right_shift(lax.broadcasted_iota(jnp.int32, (TQ, TQ), 1), shift)
    allowed = kc <= qc

    for nk in range(1, k_ref.shape[1] // TQ + 1):
        @pl.when(qi == nk - 1)
        def _(nk=nk):
            n_keys = nk * TQ
            for hh in range(ATT_HEADS_PER_STEP):
                hl = slice(hh * LANES, (hh + 1) * LANES)
                q = q_ref[0, :, hl]
                s = _dot_nt(q, k_ref[0, :n_keys, hl])
                diag = jnp.where(allowed, s[:, n_keys - TQ:], neg)
                s = diag if nk == 1 else jnp.concatenate([s[:, :n_keys - TQ], diag], axis=1)
                p = jnp.exp(s - jnp.max(s, axis=-1, keepdims=True))
                l = jnp.sum(p, axis=-1, keepdims=True)
                o_ref[0, :, hl] = (_dot(p.astype(BF16), v_ref[0, :n_keys, hl]) / l).astype(BF16)


def _attention(q, k, v):
    B, S, HP = q.shape
    hw = ATT_HEADS_PER_STEP * LANES
    H = HP // hw
    return pl.pallas_call(
        _attn_body,
        grid=(B, H, S // TQ),
        in_specs=[pl.BlockSpec((1, TQ, hw), lambda b, h, i: (b, i, h)),
                  pl.BlockSpec((1, S, hw), lambda b, h, i: (b, 0, h)),
                  pl.BlockSpec((1, S, hw), lambda b, h, i: (b, 0, h))],
        out_specs=pl.BlockSpec((1, TQ, hw), lambda b, h, i: (b, i, h)),
        out_shape=jax.ShapeDtypeStruct((B, S, HP), BF16),
        compiler_params=_cparams(("parallel", "parallel", "arbitrary")),
        name="attn",
    )(q, k, v)


def _lru_body(lx_ref, lg_ref, cw_ref, cb_ref, wg_ref, brg_ref, big_ref, lam_ref, rec_ref, prev_ref, h_ref):
    si = pl.program_id(1)

    @pl.when(si == 0)
    def _():
        prev_ref[...] = jnp.zeros_like(prev_ref)
        h_ref[...] = jnp.zeros_like(h_ref)

    x = lx_ref[0]
    ts, C = x.shape
    prev8 = prev_ref[...]
    row8 = lax.broadcasted_iota(jnp.int32, prev8.shape, 0)
    xc = x * cw_ref[CONV_WIDTH - 1:CONV_WIDTH, :] + cb_ref[...]
    for d in range(1, CONV_WIDTH):
        xr = pltpu.roll(x, d, 0)
        first = jnp.where(row8 < d, pltpu.roll(prev8, d, 0), xr[:SUBLANES])
        xs = jnp.concatenate([first, xr[SUBLANES:]], axis=0)
        xc = xc + xs * cw_ref[CONV_WIDTH - 1 - d:CONV_WIDTH - d, :]
    prev_ref[...] = x[ts - SUBLANES:]

    xcb = xc.astype(BF16)
    rowmod = jnp.bitwise_and(lax.broadcasted_iota(jnp.int32, (ts, LANES), 0), SUBLANES - 1)
    for blk in range(LRU_BLOCKS):
        sl = slice(blk * LANES, (blk + 1) * LANES)
        xb = xc[:, sl]
        g = _dot(xcb[:, sl], wg_ref[blk])
        r = jax.nn.sigmoid(g[:, :LANES] + brg_ref[:, sl])
        i = jax.nn.sigmoid(g[:, LANES:] + big_ref[:, sl])
        nl = -lam_ref[:, sl]
        softplus = jnp.maximum(nl, 0.0) + jnp.log(1.0 + jnp.exp(-jnp.abs(nl)))
        log_a = (-LRU_C * r) * softplus
        a = jnp.exp(log_a)
        b = jnp.sqrt(1.0 - a * a) * (i * xb)
        for d in (1, 2, 4):
            valid = rowmod >= d
            b = jnp.where(valid, a * pltpu.roll(b, d, 0) + b, b)
            a = jnp.where(valid, a * pltpu.roll(a, d, 0), a)
        carry = h_ref[:, sl]
        hs = []
        for gi in range(ts // SUBLANES):
            rs = slice(gi * SUBLANES, (gi + 1) * SUBLANES)
            hg = a[rs] * carry + b[rs]
            carry = hg[SUBLANES - 1:SUBLANES]
            hs.append(hg)
        h_ref[:, sl] = carry
        hfull = jnp.concatenate(hs, axis=0)
        rec_ref[0, :, sl] = (hfull * jax.nn.gelu(lg_ref[0, :, sl])).astype(BF16)


def _lru(lx, lg, cw, cb, wg, brg, big, lam):
    B, S, C = lx.shape
    ts = TS_LRU
    tok = pl.BlockSpec((1, ts, C), lambda b, s: (b, s, 0))
    full = lambda a: pl.BlockSpec(a.shape, lambda b, s: (0,) * a.ndim)
    return pl.pallas_call(
        _lru_body,
        grid=(B, S // ts),
        in_specs=[tok, tok, full(cw), full(cb), full(wg), full(brg), full(big), full(lam)],
        out_specs=tok,
        out_shape=jax.ShapeDtypeStruct((B, S, C), BF16),
        scratch_shapes=[pltpu.VMEM((SUBLANES, C), F32), pltpu.VMEM((1, C), F32)],
        compiler_params=_cparams(("parallel", "arbitrary")),
        name="lru",
    )(lx, lg, cw, cb, wg, brg, big, lam)


def _merge_body(o_ref, rec_ref, mg_ref, x_ref, gate1_ref, shift2_ref, scale2_ref, g2_ref,
                woa_ref, wob_ref, wout_ref, wq_ref, keys_ref, x1_ref, h2_ref, sc_ref):
    D = x_ref.shape[-1]
    ya = _dot(o_ref[0], woa_ref[...])
    yb = _dot(rec_ref[0], wob_ref[...])
    mg = mg_ref[0]
    y = jax.nn.sigmoid(mg[:, :D]) * ya + jax.nn.sigmoid(mg[:, D:]) * yb
    x1 = x_ref[0] + gate1_ref[0] * _dot(y.astype(BF16), wout_ref[...])
    x1_ref[0] = x1
    h2 = x1 * lax.rsqrt(jnp.mean(x1 * x1, axis=-1, keepdims=True) + EPS) * g2_ref[...]
    h2 = h2 * (1.0 + scale2_ref[0]) + shift2_ref[0]
    h2b = h2.astype(BF16)
    h2_ref[...] = h2.T.astype(PEER_MM_DTYPE)
    qp = _dot(h2b, wq_ref[...]).astype(BF16)
    for g in range(keys_ref.shape[0]):
        sc_ref[g * N_KEYS:(g + 1) * N_KEYS, :] = _dot_nt(keys_ref[g], qp[:, g * LANES:(g + 1) * LANES])


def _merge(o, rec, mg, x, gate1, shift2, scale2, g2, woa, wob, wout, wq, keys):
    B, S, D = x.shape
    tm = TM_PROJ
    ns = S // tm
    NG = keys.shape[0]
    tok = lambda w: pl.BlockSpec((1, tm, w), lambda b, s: (b, s, 0))
    per_b = pl.BlockSpec((1, 1, D), lambda b, s: (b, 0, 0))
    full = lambda a: pl.BlockSpec(a.shape, lambda b, s: (0,) * a.ndim)
    return pl.pallas_call(
        _merge_body,
        grid=(B, ns),
        in_specs=[tok(o.shape[-1]), tok(D), tok(2 * D), tok(D), per_b, per_b, per_b, full(g2),
                  full(woa), full(wob), full(wout), full(wq), full(keys)],
        out_specs=[tok(D), pl.BlockSpec((D, tm), lambda b, s: (0, b * ns + s)),
                   pl.BlockSpec((NG * N_KEYS, tm), lambda b, s: (0, b * ns + s))],
        out_shape=[jax.ShapeDtypeStruct((B, S, D), F32), jax.ShapeDtypeStruct((D, B * S), PEER_MM_DTYPE),
                   jax.ShapeDtypeStruct((NG * N_KEYS, B * S), F32)],
        compiler_params=_cparams(("parallel", "parallel")),
        name="merge",
    )(o, rec, mg, x, gate1, shift2, scale2, g2, woa, wob, wout, wq, keys)


def _sort16_network():
    def merge(lo, hi, r):
        step = r * 2
        if step < hi - lo:
            yield from merge(lo, hi, step)
            yield from merge(lo + r, hi, step)
            yield from [(i, i + r) for i in range(lo + r, hi - r, step)]
        else:
            yield (lo, lo + r)

    def sort(lo, hi):
        if hi - lo >= 1:
            mid = lo + (hi - lo) // 2
            yield from sort(lo, mid)
            yield from sort(mid + 1, hi)
            yield from merge(lo, hi, 1)

    return tuple(sort(0, PEER_TOPK - 1))


_SORT16 = _sort16_network()


def _compare_exchange(v, i, j):
    v[i], v[j] = jnp.maximum(v[i], v[j]), jnp.minimum(v[i], v[j])


def _bitonic_merge16(v):
    d = PEER_TOPK // 2
    while d:
        for i in range(PEER_TOPK):
            if not i & d:
                _compare_exchange(v, i, i + d)
        d //= 2


def _merge_top16(best, other):
    z = [jnp.maximum(best[i], other[PEER_TOPK - 1 - i]) if PEER_TOPK - 1 - i < len(other) else best[i]
         for i in range(PEER_TOPK)]
    _bitonic_merge16(z)
    return z


def _top16_of_128(v):
    v = list(v)
    for i, j in _SORT16:
        _compare_exchange(v, i, j)
    shift = SUBLANES // 2
    while shift:
        v = _merge_top16(v, [pltpu.roll(x, shift, 0) for x in v])
        shift //= 2
    return v


def _select_body(sc_ref, r2_ref, e2_ref, c1_ref, e1_ref):
    K = PEER_TOPK
    H = PEER_HEADS
    nv = N_KEYS // SUBLANES
    tl = sc_ref.shape[1]
    sub = lax.broadcasted_iota(jnp.int32, (SUBLANES, tl), 0)

    def group(g):
        return [sc_ref[g * N_KEYS + k * SUBLANES:g * N_KEYS + (k + 1) * SUBLANES, :] for k in range(nv)]

    t1 = [_top16_of_128(group(2 * h)) for h in range(H)]
    t2 = [_top16_of_128(group(2 * h + 1)) for h in range(H)]

    def pack(reps):
        out = reps[0]
        for h in range(1, H):
            out = jnp.where(sub == h, reps[h], out)
        return out

    def unpack(x, h):
        return jnp.broadcast_to(x[h:h + 1, :], x.shape)

    t1p = [pack([t1[h][a] for h in range(H)]) for a in range(K)]
    t2p = [pack([t2[h][b] for h in range(H)]) for b in range(K)]
    cand = [[t1p[a] + t2p[b] for b in range(K // (a + 1))] for a in range(K)]
    best = list(cand[0])
    for a in range(1, SUBLANES):
        best = _merge_top16(best, cand[a])
    best = _merge_top16(best, [cand[a][0] for a in range(SUBLANES, K)])
    thr = best[K - 1]
    z = jnp.ones_like(thr)
    for r in range(1, K):
        z = z + jnp.exp(best[r] - best[0])
    inv_z = 1.0 / z
    phi = []
    for b in range(K):
        f = jnp.full_like(thr, float("inf"))
        for a in range(K // (b + 1)):
            f = jnp.where(cand[a][b] >= thr, t1p[a], f)
        phi.append(f)

    for h in range(H):
        phi_h = [unpack(f, h) for f in phi]
        inv_z_h = unpack(inv_z, h)
        c1, e1, r2, e2 = [], [], [], []
        for x in group(2 * h):
            cnt = jnp.zeros_like(x)
            for b in range(K):
                cnt = cnt + jnp.where(x >= phi_h[b], 1.0, 0.0)
            c1.append(cnt)
            e1.append(jnp.exp(x - t1[h][0]))
        for x in group(2 * h + 1):
            rank = jnp.zeros_like(x)
            for r in range(K):
                rank = rank + jnp.where(t2[h][r] > x, 1.0, 0.0)
            r2.append(rank)
            e2.append(jnp.exp(x - t2[h][0]) * inv_z_h)
        rows = slice(h * N_KEYS, (h + 1) * N_KEYS)
        c1_ref[rows, :] = jnp.concatenate(c1, axis=0)
        e1_ref[rows, :] = jnp.concatenate(e1, axis=0)
        r2_ref[rows, :] = jnp.concatenate(r2, axis=0).astype(BF16)
        e2_ref[rows, :] = jnp.concatenate(e2, axis=0).astype(BF16)


def _select(sc):
    R, T = sc.shape
    tl = TL_SEL
    half = R // 2
    blk = pl.BlockSpec((half, tl), lambda t: (0, t))
    return pl.pallas_call(
        _select_body,
        grid=(T // tl,),
        in_specs=[pl.BlockSpec((R, tl), lambda t: (0, t))],
        out_specs=[blk, blk, blk, blk],
        out_shape=[jax.ShapeDtypeStruct((half, T), BF16), jax.ShapeDtypeStruct((half, T), BF16),
                   jax.ShapeDtypeStruct((half, T), F32), jax.ShapeDtypeStruct((half, T), F32)],
        compiler_params=_cparams(("parallel",)),
        name="select",
    )(sc)


def _peer_body(h2t_ref, u0_ref, un_ref, vt_ref, r2_ref, e2_ref, c1_ref, e1_ref, x1_ref, gate2_ref,
               out_ref, acc_ref, at_ref, w_ref):
    e = pl.program_id(1)
    tm = h2t_ref.shape[1]
    n_i = un_ref.shape[0] // N_KEYS

    @pl.when(e == 0)
    def _():
        acc_ref[...] = jnp.zeros_like(acc_ref)
        at_ref[...] = (_dot(u0_ref[...], h2t_ref[...]) * (1.0 / PEER_U_SCALE)).astype(BF16)

    assert n_i == SUBLANES
    i0 = pl.multiple_of(e * n_i, SUBLANES)

    def row_bf16(ref, h, ii, ls):
        row = ref[pl.ds(h * N_KEYS + i0, SUBLANES), ls][ii:ii + 1]
        one = jnp.broadcast_to(row, (2 * SUBLANES, LANES)).astype(BF16)
        return jnp.concatenate([one] * (N_KEYS // (2 * SUBLANES)), axis=0)

    def routing_weights(lb):
        ls = slice(lb * LANES, (lb + 1) * LANES)
        for ii in range(n_i):
            rs = slice(ii * N_KEYS, (ii + 1) * N_KEYS)
            coef = None
            for h in range(PEER_HEADS):
                hs = slice(h * N_KEYS, (h + 1) * N_KEYS)
                e2 = e2_ref[hs, ls]
                sel = r2_ref[hs, ls] < row_bf16(c1_ref, h, ii, ls)
                term = jnp.where(sel, e2, jnp.zeros_like(e2)) * row_bf16(e1_ref, h, ii, ls)
                coef = term if coef is None else coef + term
            a = at_ref[rs, ls]
            cdf = 0.5 * jnp.tanh(a * (GELU_K0 + GELU_K1 * (a * a))) + 0.5
            w_ref[rs, ls] = (coef * (a * cdf)).astype(PEER_MM_DTYPE)

    half = tm // 2
    for ts in (slice(0, half), slice(half, tm)):
        for lb in range(ts.start // LANES, ts.stop // LANES):
            routing_weights(lb)
        at_ref[:, ts] = (_dot(un_ref[...], h2t_ref[:, ts]) * (1.0 / PEER_U_SCALE)).astype(BF16)
        acc_ref[:, ts] += _dot(vt_ref[...], w_ref[:, ts])

    @pl.when(e == pl.num_programs(1) - 1)
    def _():
        out_ref[...] = x1_ref[...] + gate2_ref[0] * acc_ref[...].T


def _peer(h2t, u, vt, r2, e2, c1, e1, x1, gate2, seq_len):
    D, T = h2t.shape
    NE = u.shape[0]
    tm, te = TM_PEER, TE_PEER
    tiles_per_seq = seq_len // tm
    table = pl.BlockSpec((r2.shape[0], tm), lambda t, e: (0, t))
    return pl.pallas_call(
        _peer_body,
        grid=(T // tm, NE // te),
        in_specs=[pl.BlockSpec((D, tm), lambda t, e: (0, t)),
                  pl.BlockSpec((te, D), lambda t, e: (0, 0)),
                  pl.BlockSpec((te, D), lambda t, e: (jnp.minimum(e + 1, NE // te - 1), 0)),
                  pl.BlockSpec((D, te), lambda t, e: (0, e)),
                  table, table, table, table,
                  pl.BlockSpec((tm, D), lambda t, e: (t, 0)),
                  pl.BlockSpec((1, 1, D), lambda t, e: (t // tiles_per_seq, 0, 0))],
        out_specs=pl.BlockSpec((tm, D), lambda t, e: (t, 0)),
        out_shape=jax.ShapeDtypeStruct((T, D), F32),
        scratch_shapes=[pltpu.VMEM((D, tm), F32), pltpu.VMEM((te, tm), BF16), pltpu.VMEM((te, tm), PEER_MM_DTYPE)],
        compiler_params=_cparams(("parallel", "arbitrary")),
        name="peer",
    )(h2t, u, u, vt, r2, e2, c1, e1, x1, gate2)


def _pad_heads(w, real):
    K = w.shape[0]
    w = w.reshape(K, -1, real)
    return jnp.pad(w, ((0, 0), (0, 0), (0, LANES - real))).reshape(K, -1)


def _layer(x, cs_mod, positions, p, l):
    B, S, D = x.shape
    H = MLA_HEADS
    shift1, scale1, gate1, shift2, scale2, gate2 = [cs_mod[:, j].reshape(B, 1, D) for j in range(6)]
    w_in = p["w_in"][l]
    c0, c1, c2 = Q_LORA, Q_LORA + KV_LORA, Q_LORA + KV_LORA + QK_ROPE
    w_rope = jnp.pad(w_in[:, c1:c2], ((0, 0), (QK_NOPE, LANES - QK_HEAD)))
    wqkr = jnp.concatenate([w_in[:, :c1], w_rope], axis=1).astype(BF16)
    wlru = w_in[:, c2:c2 + 2 * D].astype(BF16)
    wmg = w_in[:, c2 + 2 * D:].astype(BF16)
    wuq = _pad_heads(p["w_uq"][l], QK_HEAD).astype(BF16)
    wukv = p["w_ukv"][l].reshape(KV_LORA, H, QK_NOPE + V_HEAD)
    wuk = jnp.pad(wukv[:, :, :QK_NOPE], ((0, 0), (0, 0), (0, LANES - QK_NOPE))).reshape(KV_LORA, H * LANES).astype(BF16)
    wuv = jnp.pad(wukv[:, :, QK_NOPE:], ((0, 0), (0, 0), (0, LANES - V_HEAD))).reshape(KV_LORA, H * LANES).astype(BF16)
    gqn = jnp.pad(p["q_norm_g"][l] * (QK_HEAD ** -0.5), (0, LANES - QK_HEAD)).reshape(1, LANES)
    gkn = jnp.pad(p["k_norm_g"][l], (0, LANES - QK_HEAD)).reshape(1, LANES)
    inv_freq = 1.0 / (ROPE_THETA ** (jnp.arange(0, QK_ROPE, 2, dtype=F32) / QK_ROPE))
    half = QK_ROPE // 2
    zeros_lo = jnp.zeros((QK_NOPE,), F32)
    zeros_hi = jnp.zeros((LANES - QK_HEAD,), F32)
    freq = jnp.concatenate([zeros_lo, inv_freq, inv_freq, zeros_hi]).reshape(1, LANES)
    sign = jnp.concatenate([zeros_lo, -jnp.ones((half,), F32), jnp.ones((half,), F32), zeros_hi]).reshape(1, LANES)

    q, k, v, lx, lg, mg = _in_proj(
        x, positions.reshape(B, S, 1), shift1, scale1, p["norm1_g"][l].reshape(1, D), wqkr, wlru, wmg,
        p["q_a_norm_g"][l].reshape(1, Q_LORA), wuq, p["kv_a_norm_g"][l].reshape(1, KV_LORA), wuk, wuv,
        gqn, gkn, freq, sign)
    o = _attention(q, k, v)

    C = p["conv_b"].shape[-1]
    wg = jnp.concatenate([p["w_rg"][l], p["w_ig"][l]], axis=-1).astype(BF16)
    rec = _lru(lx, lg, p["conv_w"][l], p["conv_b"][l].reshape(1, C), wg, p["b_rg"][l].reshape(1, C),
               p["b_ig"][l].reshape(1, C), p["lru_lambda"][l].reshape(1, C))

    woa = jnp.pad(p["w_o_attn"][l].reshape(H, V_HEAD, D), ((0, 0), (0, LANES - V_HEAD), (0, 0)))
    woa = woa.reshape(H * LANES, D).astype(BF16)
    keys = p["sub_keys"][l].reshape(PEER_HEADS * 2, N_KEYS, -1).astype(BF16)
    x1, h2t, sc = _merge(o, rec, mg, x, gate1, shift2, scale2, p["norm2_g"][l].reshape(1, D), woa,
                        p["w_o_lru"][l].astype(BF16), p["w_out"][l].astype(BF16), p["w_query"][l].astype(BF16), keys)
    r2, e2, c1, e1 = _select(sc)
    out = _peer(h2t, (p["expert_u"][l] * PEER_U_SCALE).astype(PEER_MM_DTYPE), p["expert_v"][l].T.astype(PEER_MM_DTYPE),
                r2, e2, c1, e1, x1.reshape(B * S, D), gate2, S)
    return out.reshape(B, S, D)


def kernel(x, c, positions, w_ada, b_ada, norm1_g, w_in, q_a_norm_g, w_uq, kv_a_norm_g, w_ukv, q_norm_g, k_norm_g, w_o_attn, conv_w, conv_b, w_rg, b_rg, w_ig, b_ig, lru_lambda, w_o_lru, w_out, norm2_g, w_query, sub_keys, expert_u, expert_v):
    p = dict(w_in=w_in, norm1_g=norm1_g, q_a_norm_g=q_a_norm_g, w_uq=w_uq, kv_a_norm_g=kv_a_norm_g, w_ukv=w_ukv,
             q_norm_g=q_norm_g, k_norm_g=k_norm_g, w_o_attn=w_o_attn, conv_w=conv_w, conv_b=conv_b, w_rg=w_rg,
             b_rg=b_rg, w_ig=w_ig, b_ig=b_ig, lru_lambda=lru_lambda, w_o_lru=w_o_lru, w_out=w_out,
             norm2_g=norm2_g, w_query=w_query, sub_keys=sub_keys, expert_u=expert_u, expert_v=expert_v)
    B, D = c.shape
    for l in range(w_ada.shape[0]):
        mod = _mod(c, w_ada[l], b_ada[l]).reshape(B, 6, D)
        x = _layer(x, mod, positions, p, l)
    return x
```

```python
import functools

import jax
import jax.numpy as jnp
from jax import lax
from jax.experimental import pallas as pl
from jax.experimental.pallas import tpu as pltpu

F32 = jnp.float32
BF16 = jnp.bfloat16

EPS = 1e-6
LANES = 128
SUBLANES = 8
VMEM_LIMIT = 56 * 1024 * 1024

CHUNK = 64
MLA_HEADS = 8
QK_NOPE = 64
QK_ROPE = 32
QK_HEAD = QK_NOPE + QK_ROPE
V_HEAD = 64
Q_LORA = 256
KV_LORA = 128
ROPE_THETA = 10000.0
LRU_BLOCKS = 8
CONV_WIDTH = 4
LRU_C = 8.0
PEER_HEADS = 8
N_KEYS = 128
PEER_TOPK = 16
GELU_K0 = 0.7978845608028654
GELU_K1 = 0.044715 * GELU_K0

TM_PROJ = 256
TQ = 256
ATT_HEADS_PER_STEP = 4
TS_LRU = 256
TL_SEL = 128
TM_PEER = 512
TE_PEER = 1024
PEER_MM_DTYPE = jnp.float8_e4m3fn
PEER_U_SCALE = 32.0


def _cparams(sem):
    return pltpu.CompilerParams(dimension_semantics=sem, vmem_limit_bytes=VMEM_LIMIT)


def _dot(a, b):
    return jnp.dot(a, b, preferred_element_type=F32)


def _dot_nt(a, b):
    return lax.dot_general(a, b, (((1,), (1,)), ((), ())), preferred_element_type=F32)


def _mod_body(c_ref, w_ref, b_ref, o_ref):
    c = c_ref[...]
    cs = c * jax.nn.sigmoid(c)
    o_ref[...] = _dot(cs.astype(BF16), w_ref[...].astype(BF16)) + b_ref[...]


def _mod(c, w_ada, b_ada):
    B, D = c.shape
    N = w_ada.shape[1]
    tn = 1024
    return pl.pallas_call(
        _mod_body,
        grid=(N // tn,),
        in_specs=[pl.BlockSpec((B, D), lambda n: (0, 0)),
                  pl.BlockSpec((D, tn), lambda n: (0, n)),
                  pl.BlockSpec((1, tn), lambda n: (0, n))],
        out_specs=pl.BlockSpec((B, tn), lambda n: (0, n)),
        out_shape=jax.ShapeDtypeStruct((B, N), F32),
        compiler_params=_cparams(("parallel",)),
        name="mod",
    )(c, w_ada, b_ada.reshape(1, N))


def _in_proj_body(x_ref, pos_ref, shift_ref, scale_ref, g1_ref, wqkr_ref, wlru_ref, wmg_ref,
                  gqa_ref, wuq_ref, gkva_ref, wuk_ref, wuv_ref, gqn_ref, gkn_ref, freq_ref,
                  q_ref, k_ref, v_ref, lx_ref, lg_ref, mg_ref):
    HP = MLA_HEADS * LANES
    x = x_ref[0]
    D = x.shape[-1]
    y = x * lax.rsqrt(jnp.mean(x * x, axis=-1, keepdims=True) + EPS) * g1_ref[...]
    h = y * (1.0 + scale_ref[0]) + shift_ref[0]
    hb = h.astype(BF16)

    lru = _dot(hb, wlru_ref[...])
    lx_ref[0] = lru[:, :D]
    lg_ref[0] = lru[:, D:]
    mg_ref[0] = _dot(hb, wmg_ref[...])

    qkr = _dot(hb, wqkr_ref[...])
    cq = qkr[:, :Q_LORA]
    ckv = qkr[:, Q_LORA:Q_LORA + KV_LORA]
    kr = qkr[:, Q_LORA + KV_LORA:Q_LORA + KV_LORA + LANES]
    kr_partner = qkr[:, Q_LORA + KV_LORA + LANES:]
    qn = cq * lax.rsqrt(jnp.mean(cq * cq, axis=-1, keepdims=True) + EPS) * gqa_ref[...]
    kvn = ckv * lax.rsqrt(jnp.mean(ckv * ckv, axis=-1, keepdims=True) + EPS) * gkva_ref[...]
    kvb = kvn.astype(BF16)
    q_both = _dot(qn.astype(BF16), wuq_ref[...])
    k_all = _dot(kvb, wuk_ref[...])
    v_ref[0] = _dot(kvb, wuv_ref[...]).astype(BF16)

    ang = pos_ref[0].astype(F32) * freq_ref[...]
    cos_l = jnp.cos(ang)
    sin_l = jnp.sin(ang)
    q_cos = cos_l * gqn_ref[...]
    k_cos = cos_l * gkn_ref[...]
    k_sin = kr_partner * sin_l
    inv_n = 1.0 / QK_HEAD
    for hh in range(MLA_HEADS):
        sl = slice(hh * LANES, (hh + 1) * LANES)
        qh = q_both[:, sl]
        rq = lax.rsqrt(jnp.sum(qh * qh, axis=-1, keepdims=True) * inv_n + EPS)
        q_ref[0, :, sl] = ((qh * q_cos + q_both[:, HP + hh * LANES:HP + (hh + 1) * LANES] * sin_l) * rq).astype(BF16)
        kh = k_all[:, sl] + kr
        rk = lax.rsqrt(jnp.sum(kh * kh, axis=-1, keepdims=True) * inv_n + EPS)
        k_ref[0, :, sl] = ((kh * k_cos + k_sin) * rk).astype(BF16)


def _in_proj(x, pos3, shift1, scale1, g1, wqkr, wlru, wmg, gqa, wuq, gkva, wuk, wuv, gqn, gkn, freq):
    B, S, D = x.shape
    tm = TM_PROJ
    HP = MLA_HEADS * LANES
    tok = lambda w: pl.BlockSpec((1, tm, w), lambda b, s: (b, s, 0))
    per_b = pl.BlockSpec((1, 1, D), lambda b, s: (b, 0, 0))
    full = lambda a: pl.BlockSpec(a.shape, lambda b, s: (0,) * a.ndim)
    return pl.pallas_call(
        _in_proj_body,
        grid=(B, S // tm),
        in_specs=[tok(D), tok(1), per_b, per_b, full(g1), full(wqkr), full(wlru), full(wmg),
                  full(gqa), full(wuq), full(gkva), full(wuk), full(wuv), full(gqn), full(gkn),
                  full(freq)],
        out_specs=[tok(HP), tok(HP), tok(HP), tok(D), tok(D), tok(2 * D)],
        out_shape=[jax.ShapeDtypeStruct((B, S, HP), BF16)] * 3
        + [jax.ShapeDtypeStruct((B, S, D), F32)] * 2 + [jax.ShapeDtypeStruct((B, S, 2 * D), F32)],
        compiler_params=_cparams(("parallel", "parallel")),
        name="in_proj",
    )(x, pos3, shift1, scale1, g1, wqkr, wlru, wmg, gqa, wuq, gkva, wuk, wuv, gqn, gkn, freq)


def _attn_body(q_ref, k_ref, v_ref, o_ref):
    qi = pl.program_id(2)
    neg = float(jnp.finfo(jnp.float32).min)
    shift = CHUNK.bit_length() - 1
    qc = jnp.right_shift(lax.broadcasted_iota(jnp.int32, (TQ, TQ), 0), shift)
    kc = jnp.right_shift(lax.broadcasted_iota(jnp.int32, (TQ, TQ), 1), shift)
    allowed = kc <= qc

    for nk in range(1, k_ref.shape[1] // TQ + 1):
        @pl.when(qi == nk - 1)
        def _(nk=nk):
            n_keys = nk * TQ
            for hh in range(ATT_HEADS_PER_STEP):
                hl = slice(hh * LANES, (hh + 1) * LANES)
                q = q_ref[0, :, hl]
                s = _dot_nt(q, k_ref[0, :n_keys, hl])
                diag = jnp.where(allowed, s[:, n_keys - TQ:], neg)
                s = diag if nk == 1 else jnp.concatenate([s[:, :n_keys - TQ], diag], axis=1)
                p = jnp.exp(s - jnp.max(s, axis=-1, keepdims=True))
                l = jnp.sum(p, axis=-1, keepdims=True)
                o_ref[0, :, hl] = (_dot(p.astype(BF16), v_ref[0, :n_keys, hl]) / l).astype(BF16)


def _attention(q, k, v):
    B, S, HP = q.shape
    hw = ATT_HEADS_PER_STEP * LANES
    H = HP // hw
    return pl.pallas_call(
        _attn_body,
        grid=(B, H, S // TQ),
        in_specs=[pl.BlockSpec((1, TQ, hw), lambda b, h, i: (b, i, h)),
                  pl.BlockSpec((1, S, hw), lambda b, h, i: (b, 0, h)),
                  pl.BlockSpec((1, S, hw), lambda b, h, i: (b, 0, h))],
        out_specs=pl.BlockSpec((1, TQ, hw), lambda b, h, i: (b, i, h)),
        out_shape=jax.ShapeDtypeStruct((B, S, HP), BF16),
        compiler_params=_cparams(("parallel", "parallel", "arbitrary")),
        name="attn",
    )(q, k, v)


def _lru_body(lx_ref, lg_ref, cw_ref, cb_ref, wg_ref, brg_ref, big_ref, lam_ref, rec_ref, prev_ref, h_ref):
    si = pl.program_id(1)

    @pl.when(si == 0)
    def _():
        prev_ref[...] = jnp.zeros_like(prev_ref)
        h_ref[...] = jnp.zeros_like(h_ref)

    x = lx_ref[0]
    ts, C = x.shape
    prev8 = prev_ref[...]
    row8 = lax.broadcasted_iota(jnp.int32, prev8.shape, 0)
    xc = x * cw_ref[CONV_WIDTH - 1:CONV_WIDTH, :] + cb_ref[...]
    for d in range(1, CONV_WIDTH):
        xr = pltpu.roll(x, d, 0)
        first = jnp.where(row8 < d, pltpu.roll(prev8, d, 0), xr[:SUBLANES])
        xs = jnp.concatenate([first, xr[SUBLANES:]], axis=0)
        xc = xc + xs * cw_ref[CONV_WIDTH - 1 - d:CONV_WIDTH - d, :]
    prev_ref[...] = x[ts - SUBLANES:]

    xcb = xc.astype(BF16)
    rowmod = jnp.bitwise_and(lax.broadcasted_iota(jnp.int32, (ts, LANES), 0), SUBLANES - 1)
    for blk in range(LRU_BLOCKS):
        sl = slice(blk * LANES, (blk + 1) * LANES)
        xb = xc[:, sl]
        g = _dot(xcb[:, sl], wg_ref[blk])
        r = jax.nn.sigmoid(g[:, :LANES] + brg_ref[:, sl])
        i = jax.nn.sigmoid(g[:, LANES:] + big_ref[:, sl])
        nl = -lam_ref[:, sl]
        softplus = jnp.maximum(nl, 0.0) + jnp.log(1.0 + jnp.exp(-jnp.abs(nl)))
        log_a = (-LRU_C * r) * softplus
        a = jnp.exp(log_a)
        b = jnp.sqrt(1.0 - a * a) * (i * xb)
        for d in (1, 2, 4):
            valid = rowmod >= d
            b = jnp.where(valid, a * pltpu.roll(b, d, 0) + b, b)
            a = jnp.where(valid, a * pltpu.roll(a, d, 0), a)
        carry = h_ref[:, sl]
        hs = []
        for gi in range(ts // SUBLANES):
            rs = slice(gi * SUBLANES, (gi + 1) * SUBLANES)
            hg = a[rs] * carry + b[rs]
            carry = hg[SUBLANES - 1:SUBLANES]
            hs.append(hg)
        h_ref[:, sl] = carry
        hfull = jnp.concatenate(hs, axis=0)
        rec_ref[0, :, sl] = (hfull * jax.nn.gelu(lg_ref[0, :, sl])).astype(BF16)


def _lru(lx, lg, cw, cb, wg, brg, big, lam):
    B, S, C = lx.shape
    ts = TS_LRU
    tok = pl.BlockSpec((1, ts, C), lambda b, s: (b, s, 0))
    full = lambda a: pl.BlockSpec(a.shape, lambda b, s: (0,) * a.ndim)
    return pl.pallas_call(
        _lru_body,
        grid=(B, S // ts),
        in_specs=[tok, tok, full(cw), full(cb), full(wg), full(brg), full(big), full(lam)],
        out_specs=tok,
        out_shape=jax.ShapeDtypeStruct((B, S, C), BF16),
        scratch_shapes=[pltpu.VMEM((SUBLANES, C), F32), pltpu.VMEM((1, C), F32)],
        compiler_params=_cparams(("parallel", "arbitrary")),
        name="lru",
    )(lx, lg, cw, cb, wg, brg, big, lam)


def _merge_body(o_ref, rec_ref, mg_ref, x_ref, gate1_ref, shift2_ref, scale2_ref, g2_ref,
                woa_ref, wob_ref, wout_ref, wq_ref, keys_ref, x1_ref, h2_ref, sc_ref):
    D = x_ref.shape[-1]
    ya = _dot(o_ref[0], woa_ref[...])
    yb = _dot(rec_ref[0], wob_ref[...])
    mg = mg_ref[0]
    y = jax.nn.sigmoid(mg[:, :D]) * ya + jax.nn.sigmoid(mg[:, D:]) * yb
    x1 = x_ref[0] + gate1_ref[0] * _dot(y.astype(BF16), wout_ref[...])
    x1_ref[0] = x1
    h2 = x1 * lax.rsqrt(jnp.mean(x1 * x1, axis=-1, keepdims=True) + EPS) * g2_ref[...]
    h2 = h2 * (1.0 + scale2_ref[0]) + shift2_ref[0]
    h2b = h2.astype(BF16)
    h2_ref[...] = h2.T.astype(PEER_MM_DTYPE)
    qp = _dot(h2b, wq_ref[...]).astype(BF16)
    for g in range(keys_ref.shape[0]):
        sc_ref[g * N_KEYS:(g + 1) * N_KEYS, :] = _dot_nt(keys_ref[g], qp[:, g * LANES:(g + 1) * LANES])


def _merge(o, rec, mg, x, gate1, shift2, scale2, g2, woa, wob, wout, wq, keys):
    B, S, D = x.shape
    tm = TM_PROJ
    ns = S // tm
    NG = keys.shape[0]
    tok = lambda w: pl.BlockSpec((1, tm, w), lambda b, s: (b, s, 0))
    per_b = pl.BlockSpec((1, 1, D), lambda b, s: (b, 0, 0))
    full = lambda a: pl.BlockSpec(a.shape, lambda b, s: (0,) * a.ndim)
    return pl.pallas_call(
        _merge_body,
        grid=(B, ns),
        in_specs=[tok(o.shape[-1]), tok(D), tok(2 * D), tok(D), per_b, per_b, per_b, full(g2),
                  full(woa), full(wob), full(wout), full(wq), full(keys)],
        out_specs=[tok(D), pl.BlockSpec((D, tm), lambda b, s: (0, b * ns + s)),
                   pl.BlockSpec((NG * N_KEYS, tm), lambda b, s: (0, b * ns + s))],
        out_shape=[jax.ShapeDtypeStruct((B, S, D), F32), jax.ShapeDtypeStruct((D, B * S), PEER_MM_DTYPE),
                   jax.ShapeDtypeStruct((NG * N_KEYS, B * S), F32)],
        compiler_params=_cparams(("parallel", "parallel")),
        name="merge",
    )(o, rec, mg, x, gate1, shift2, scale2, g2, woa, wob, wout, wq, keys)


def _sort16_network():
    def merge(lo, hi, r):
        step = r * 2
        if step < hi - lo:
            yield from merge(lo, hi, step)
            yield from merge(lo + r, hi, step)
            yield from [(i, i + r) for i in range(lo + r, hi - r, step)]
        else:
            yield (lo, lo + r)

    def sort(lo, hi):
        if hi - lo >= 1:
            mid = lo + (hi - lo) // 2
            yield from sort(lo, mid)
            yield from sort(mid + 1, hi)
            yield from merge(lo, hi, 1)

    return tuple(sort(0, PEER_TOPK - 1))


_SORT16 = _sort16_network()


def _compare_exchange(v, i, j):
    v[i], v[j] = jnp.maximum(v[i], v[j]), jnp.minimum(v[i], v[j])


def _bitonic_merge16(v):
    d = PEER_TOPK // 2
    while d:
        for i in range(PEER_TOPK):
            if not i & d:
                _compare_exchange(v, i, i + d)
        d //= 2


def _merge_top16(best, other):
    z = [jnp.maximum(best[i], other[PEER_TOPK - 1 - i]) if PEER_TOPK - 1 - i < len(other) else best[i]
         for i in range(PEER_TOPK)]
    _bitonic_merge16(z)
    return z


def _top16_of_128(v):
    v = list(v)
    for i, j in _SORT16:
        _compare_exchange(v, i, j)
    shift = SUBLANES // 2
    while shift:
        v = _merge_top16(v, [pltpu.roll(x, shift, 0) for x in v])
        shift //= 2
    return v


def _select_body(sc_ref, r2_ref, e2_ref, c1_ref, e1_ref):
    K = PEER_TOPK
    H = PEER_HEADS
    nv = N_KEYS // SUBLANES
    tl = sc_ref.shape[1]
    sub = lax.broadcasted_iota(jnp.int32, (SUBLANES, tl), 0)

    def group(g):
        return [sc_ref[g * N_KEYS + k * SUBLANES:g * N_KEYS + (k + 1) * SUBLANES, :] for k in range(nv)]

    t1 = [_top16_of_128(group(2 * h)) for h in range(H)]
    t2 = [_top16_of_128(group(2 * h + 1)) for h in range(H)]

    def pack(reps):
        out = reps[0]
        for h in range(1, H):
            out = jnp.where(sub == h, reps[h], out)
        return out

    def unpack(x, h):
        return jnp.broadcast_to(x[h:h + 1, :], x.shape)

    t1p = [pack([t1[h][a] for h in range(H)]) for a in range(K)]
    t2p = [pack([t2[h][b] for h in range(H)]) for b in range(K)]
    cand = [[t1p[a] + t2p[b] for b in range(K // (a + 1))] for a in range(K)]
    best = list(cand[0])
    for a in range(1, SUBLANES):
        best = _merge_top16(best, cand[a])
    best = _merge_top16(best, [cand[a][0] for a in range(SUBLANES, K)])
    thr = best[K - 1]
    z = jnp.ones_like(thr)
    for r in range(1, K):
        z = z + jnp.exp(best[r] - best[0])
    inv_z = 1.0 / z
    phi = []
    for b in range(K):
        f = jnp.full_like(thr, float("inf"))
        for a in range(K // (b + 1)):
            f = jnp.where(cand[a][b] >= thr, t1p[a], f)
        phi.append(f)

    for h in range(H):
        phi_h = [unpack(f, h) for f in phi]
        inv_z_h = unpack(inv_z, h)
        c1, e1, r2, e2 = [], [], [], []
        for x in group(2 * h):
            cnt = jnp.zeros_like(x)
            for b in range(K):
                cnt = cnt + jnp.where(x >= phi_h[b], 1.0, 0.0)
            c1.append(cnt)
            e1.append(jnp.exp(x - t1[h][0]))
        for x in group(2 * h + 1):
            rank = jnp.zeros_like(x)
            for r in range(K):
                rank = rank + jnp.where(t2[h][r] > x, 1.0, 0.0)
            r2.append(rank)
            e2.append(jnp.exp(x - t2[h][0]) * inv_z_h)
        rows = slice(h * N_KEYS, (h + 1) * N_KEYS)
        c1_ref[rows, :] = jnp.concatenate(c1, axis=0)
        e1_ref[rows, :] = jnp.concatenate(e1, axis=0)
        r2_ref[rows, :] = jnp.concatenate(r2, axis=0).astype(BF16)
        e2_ref[rows, :] = jnp.concatenate(e2, axis=0).astype(BF16)


def _select(sc):
    R, T = sc.shape
    tl = TL_SEL
    half = R // 2
    blk = pl.BlockSpec((half, tl), lambda t: (0, t))
    return pl.pallas_call(
        _select_body,
        grid=(T // tl,),
        in_specs=[pl.BlockSpec((R, tl), lambda t: (0, t))],
        out_specs=[blk, blk, blk, blk],
        out_shape=[jax.ShapeDtypeStruct((half, T), BF16), jax.ShapeDtypeStruct((half, T), BF16),
                   jax.ShapeDtypeStruct((half, T), F32), jax.ShapeDtypeStruct((half, T), F32)],
        compiler_params=_cparams(("parallel",)),
        name="select",
    )(sc)


def _peer_body(h2t_ref, u0_ref, un_ref, vt_ref, r2_ref, e2_ref, c1_ref, e1_ref, x1_ref, gate2_ref,
               out_ref, acc_ref, at_ref, w_ref):
    e = pl.program_id(1)
    tm = h2t_ref.shape[1]
    n_i = un_ref.shape[0] // N_KEYS

    @pl.when(e == 0)
    def _():
        acc_ref[...] = jnp.zeros_like(acc_ref)
        at_ref[...] = (_dot(u0_ref[...], h2t_ref[...]) * (1.0 / PEER_U_SCALE)).astype(BF16)

    assert n_i == SUBLANES
    i0 = pl.multiple_of(e * n_i, SUBLANES)

    def row_bf16(ref, h, ii, ls):
        row = ref[pl.ds(h * N_KEYS + i0, SUBLANES), ls][ii:ii + 1]
        one = jnp.broadcast_to(row, (2 * SUBLANES, LANES)).astype(BF16)
        return jnp.concatenate([one] * (N_KEYS // (2 * SUBLANES)), axis=0)

    def routing_weights(lb):
        ls = slice(lb * LANES, (lb + 1) * LANES)
        for ii in range(n_i):
            rs = slice(ii * N_KEYS, (ii + 1) * N_KEYS)
            coef = None
            for h in range(PEER_HEADS):
                hs = slice(h * N_KEYS, (h + 1) * N_KEYS)
                e2 = e2_ref[hs, ls]
                sel = r2_ref[hs, ls] < row_bf16(c1_ref, h, ii, ls)
                term = jnp.where(sel, e2, jnp.zeros_like(e2)) * row_bf16(e1_ref, h, ii, ls)
                coef = term if coef is None else coef + term
            a = at_ref[rs, ls]
            cdf = 0.5 * jnp.tanh(a * (GELU_K0 + GELU_K1 * (a * a))) + 0.5
            w_ref[rs, ls] = (coef * (a * cdf)).astype(PEER_MM_DTYPE)

    half = tm // 2
    for ts in (slice(0, half), slice(half, tm)):
        for lb in range(ts.start // LANES, ts.stop // LANES):
            routing_weights(lb)
        at_ref[:, ts] = (_dot(un_ref[...], h2t_ref[:, ts]) * (1.0 / PEER_U_SCALE)).astype(BF16)
        acc_ref[:, ts] += _dot(vt_ref[...], w_ref[:, ts])

    @pl.when(e == pl.num_programs(1) - 1)
    def _():
        out_ref[...] = x1_ref[...] + gate2_ref[0] * acc_ref[...].T


def _peer(h2t, u, vt, r2, e2, c1, e1, x1, gate2, seq_len):
    D, T = h2t.shape
    NE = u.shape[0]
    tm, te = TM_PEER, TE_PEER
    tiles_per_seq = seq_len // tm
    table = pl.BlockSpec((r2.shape[0], tm), lambda t, e: (0, t))
    return pl.pallas_call(
        _peer_body,
        grid=(T // tm, NE // te),
        in_specs=[pl.BlockSpec((D, tm), lambda t, e: (0, t)),
                  pl.BlockSpec((te, D), lambda t, e: (0, 0)),
                  pl.BlockSpec((te, D), lambda t, e: (jnp.minimum(e + 1, NE // te - 1), 0)),
                  pl.BlockSpec((D, te), lambda t, e: (0, e)),
                  table, table, table, table,
                  pl.BlockSpec((tm, D), lambda t, e: (t, 0)),
                  pl.BlockSpec((1, 1, D), lambda t, e: (t // tiles_per_seq, 0, 0))],
        out_specs=pl.BlockSpec((tm, D), lambda t, e: (t, 0)),
        out_shape=jax.ShapeDtypeStruct((T, D), F32),
        scratch_shapes=[pltpu.VMEM((D, tm), F32), pltpu.VMEM((te, tm), BF16), pltpu.VMEM((te, tm), PEER_MM_DTYPE)],
        compiler_params=_cparams(("parallel", "arbitrary")),
        name="peer",
    )(h2t, u, u, vt, r2, e2, c1, e1, x1, gate2)


def _layer(x, cs_mod, positions, p, l):
    B, S, D = x.shape
    H = MLA_HEADS
    shift1, scale1, gate1, shift2, scale2, gate2 = [cs_mod[:, j].reshape(B, 1, D) for j in range(6)]
    w_in = p["w_in"][l]
    c0, c1, c2 = Q_LORA, Q_LORA + KV_LORA, Q_LORA + KV_LORA + QK_ROPE
    half = QK_ROPE // 2
    gqn = jnp.pad(p["q_norm_g"][l] * (QK_HEAD ** -0.5), (0, LANES - QK_HEAD))
    gkn = jnp.pad(p["k_norm_g"][l], (0, LANES - QK_HEAD))

    def rotary_partner(w, gain):
        wg = w * gain
        t1, t2 = wg[..., QK_NOPE:QK_NOPE + half], wg[..., QK_NOPE + half:QK_HEAD]
        lo = jnp.zeros(w.shape[:-1] + (QK_NOPE,), w.dtype)
        hi = jnp.zeros(w.shape[:-1] + (LANES - QK_HEAD,), w.dtype)
        return jnp.concatenate([lo, -t2, t1, hi], axis=-1)

    w_rope = jnp.pad(w_in[:, c1:c2], ((0, 0), (QK_NOPE, LANES - QK_HEAD)))
    wqkr = jnp.concatenate([w_in[:, :c1], w_rope, rotary_partner(w_rope, gkn)], axis=1).astype(BF16)
    wlru = w_in[:, c2:c2 + 2 * D].astype(BF16)
    wmg = w_in[:, c2 + 2 * D:].astype(BF16)
    wuq3 = jnp.pad(p["w_uq"][l].reshape(Q_LORA, H, QK_HEAD), ((0, 0), (0, 0), (0, LANES - QK_HEAD)))
    wuq = jnp.concatenate([wuq3.reshape(Q_LORA, H * LANES),
                           rotary_partner(wuq3, gqn).reshape(Q_LORA, H * LANES)], axis=1).astype(BF16)
    wukv = p["w_ukv"][l].reshape(KV_LORA, H, QK_NOPE + V_HEAD)
    wuk = jnp.pad(wukv[:, :, :QK_NOPE], ((0, 0), (0, 0), (0, LANES - QK_NOPE))).reshape(KV_LORA, H * LANES).astype(BF16)
    wuv = jnp.pad(wukv[:, :, QK_NOPE:], ((0, 0), (0, 0), (0, LANES - V_HEAD))).reshape(KV_LORA, H * LANES).astype(BF16)
    inv_freq = 1.0 / (ROPE_THETA ** (jnp.arange(0, QK_ROPE, 2, dtype=F32) / QK_ROPE))
    zeros_lo = jnp.zeros((QK_NOPE,), F32)
    zeros_hi = jnp.zeros((LANES - QK_HEAD,), F32)
    freq = jnp.concatenate([zeros_lo, inv_freq, inv_freq, zeros_hi]).reshape(1, LANES)

    q, k, v, lx, lg, mg = _in_proj(
        x, positions.reshape(B, S, 1), shift1, scale1, p["norm1_g"][l].reshape(1, D), wqkr, wlru, wmg,
        p["q_a_norm_g"][l].reshape(1, Q_LORA), wuq, p["kv_a_norm_g"][l].reshape(1, KV_LORA), wuk, wuv,
        gqn.reshape(1, LANES), gkn.reshape(1, LANES), freq)
    o = _attention(q, k, v)

    C = p["conv_b"].shape[-1]
    wg = jnp.concatenate([p["w_rg"][l], p["w_ig"][l]], axis=-1).astype(BF16)
    rec = _lru(lx, lg, p["conv_w"][l], p["conv_b"][l].reshape(1, C), wg, p["b_rg"][l].reshape(1, C),
               p["b_ig"][l].reshape(1, C), p["lru_lambda"][l].reshape(1, C))

    woa = jnp.pad(p["w_o_attn"][l].reshape(H, V_HEAD, D), ((0, 0), (0, LANES - V_HEAD), (0, 0)))
    woa = woa.reshape(H * LANES, D).astype(BF16)
    keys = p["sub_keys"][l].reshape(PEER_HEADS * 2, N_KEYS, -1).astype(BF16)
    x1, h2t, sc = _merge(o, rec, mg, x, gate1, shift2, scale2, p["norm2_g"][l].reshape(1, D), woa,
                        p["w_o_lru"][l].astype(BF16), p["w_out"][l].astype(BF16), p["w_query"][l].astype(BF16), keys)
    r2, e2, c1, e1 = _select(sc)
    out = _peer(h2t, (p["expert_u"][l] * PEER_U_SCALE).astype(PEER_MM_DTYPE), p["expert_v"][l].T.astype(PEER_MM_DTYPE),
                r2, e2, c1, e1, x1.reshape(B * S, D), gate2, S)
    return out.reshape(B, S, D)


def kernel(x, c, positions, w_ada, b_ada, norm1_g, w_in, q_a_norm_g, w_uq, kv_a_norm_g, w_ukv, q_norm_g, k_norm_g, w_o_attn, conv_w, conv_b, w_rg, b_rg, w_ig, b_ig, lru_lambda, w_o_lru, w_out, norm2_g, w_query, sub_keys, expert_u, expert_v):
    p = dict(w_in=w_in, norm1_g=norm1_g, q_a_norm_g=q_a_norm_g, w_uq=w_uq, kv_a_norm_g=kv_a_norm_g, w_ukv=w_ukv,
             q_norm_g=q_norm_g, k_norm_g=k_norm_g, w_o_attn=w_o_attn, conv_w=conv_w, conv_b=conv_b, w_rg=w_rg,
             b_rg=b_rg, w_ig=w_ig, b_ig=b_ig, lru_lambda=lru_lambda, w_o_lru=w_o_lru, w_out=w_out,
             norm2_g=norm2_g, w_query=w_query, sub_keys=sub_keys, expert_u=expert_u, expert_v=expert_v)
    B, D = c.shape
    for l in range(w_ada.shape[0]):
        mod = _mod(c, w_ada[l], b_ada[l]).reshape(B, 6, D)
        x = _layer(x, mod, positions, p, l)
    return x
```

```python
import functools

import jax
import jax.numpy as jnp
from jax import lax
from jax.experimental import pallas as pl
from jax.experimental.pallas import tpu as pltpu

F32 = jnp.float32
BF16 = jnp.bfloat16

EPS = 1e-6
LANES = 128
SUBLANES = 8
VMEM_LIMIT = 56 * 1024 * 1024

CHUNK = 64
MLA_HEADS = 8
QK_NOPE = 64
QK_ROPE = 32
QK_HEAD = QK_NOPE + QK_ROPE
V_HEAD = 64
Q_LORA = 256
KV_LORA = 128
ROPE_THETA = 10000.0
LRU_BLOCKS = 8
CONV_WIDTH = 4
LRU_C = 8.0
PEER_HEADS = 8
N_KEYS = 128
PEER_TOPK = 16
GELU_K0 = 0.7978845608028654
GELU_K1 = 0.044715 * GELU_K0

TM_PROJ = 256
TQ = 256
ATT_HEADS_PER_STEP = 4
TS_LRU = 256
TL_SEL = 128
TM_PEER = 512
TE_PEER = 1024
PEER_MM_DTYPE = jnp.float8_e4m3fn
PEER_U_SCALE = 32.0


def _cparams(sem):
    return pltpu.CompilerParams(dimension_semantics=sem, vmem_limit_bytes=VMEM_LIMIT)


def _dot(a, b):
    return jnp.dot(a, b, preferred_element_type=F32)


def _dot_nt(a, b):
    return lax.dot_general(a, b, (((1,), (1,)), ((), ())), preferred_element_type=F32)


def _mod_body(c_ref, w_ref, b_ref, o_ref):
    c = c_ref[...]
    cs = c * jax.nn.sigmoid(c)
    o_ref[...] = _dot(cs.astype(BF16), w_ref[...].astype(BF16)) + b_ref[...]


def _mod(c, w_ada, b_ada):
    B, D = c.shape
    N = w_ada.shape[1]
    tn = 1024
    return pl.pallas_call(
        _mod_body,
        grid=(N // tn,),
        in_specs=[pl.BlockSpec((B, D), lambda n: (0, 0)),
                  pl.BlockSpec((D, tn), lambda n: (0, n)),
                  pl.BlockSpec((1, tn), lambda n: (0, n))],
        out_specs=pl.BlockSpec((B, tn), lambda n: (0, n)),
        out_shape=jax.ShapeDtypeStruct((B, N), F32),
        compiler_params=_cparams(("parallel",)),
        name="mod",
    )(c, w_ada, b_ada.reshape(1, N))


def _in_proj_body(x_ref, pos_ref, shift_ref, scale_ref, g1_ref, wqkr_ref, wlru_ref, wmg_ref,
                  gqa_ref, wuq_ref, gkva_ref, wuk_ref, wuv_ref, gqn_ref, gkn_ref, freq_ref,
                  q_ref, k_ref, v_ref, lx_ref, lg_ref, mg_ref):
    HP = MLA_HEADS * LANES
    x = x_ref[0]
    D = x.shape[-1]
    y = x * lax.rsqrt(jnp.mean(x * x, axis=-1, keepdims=True) + EPS) * g1_ref[...]
    h = y * (1.0 + scale_ref[0]) + shift_ref[0]
    hb = h.astype(BF16)

    lru = _dot(hb, wlru_ref[...])
    lx_ref[0] = lru[:, :D]
    lg_ref[0] = lru[:, D:]
    mg_ref[0] = _dot(hb, wmg_ref[...])

    qkr = _dot(hb, wqkr_ref[...])
    cq = qkr[:, :Q_LORA]
    ckv = qkr[:, Q_LORA:Q_LORA + KV_LORA]
    kr = qkr[:, Q_LORA + KV_LORA:Q_LORA + KV_LORA + LANES]
    kr_partner = qkr[:, Q_LORA + KV_LORA + LANES:]
    qn = cq * lax.rsqrt(jnp.mean(cq * cq, axis=-1, keepdims=True) + EPS) * gqa_ref[...]
    kvn = ckv * lax.rsqrt(jnp.mean(ckv * ckv, axis=-1, keepdims=True) + EPS) * gkva_ref[...]
    kvb = kvn.astype(BF16)
    q_both = _dot(qn.astype(BF16), wuq_ref[...])
    k_all = _dot(kvb, wuk_ref[...])
    v_ref[0] = _dot(kvb, wuv_ref[...]).astype(BF16)

    ang = pos_ref[0].astype(F32) * freq_ref[...]
    cos_l = jnp.cos(ang)
    sin_l = jnp.sin(ang)
    q_cos = cos_l * gqn_ref[...]
    k_cos = cos_l * gkn_ref[...]
    k_sin = kr_partner * sin_l
    inv_n = 1.0 / QK_HEAD
    for hh in range(MLA_HEADS):
        sl = slice(hh * LANES, (hh + 1) * LANES)
        qh = q_both[:, sl]
        rq = lax.rsqrt(jnp.sum(qh * qh, axis=-1, keepdims=True) * inv_n + EPS)
        q_ref[0, :, sl] = ((qh * q_cos + q_both[:, HP + hh * LANES:HP + (hh + 1) * LANES] * sin_l) * rq).astype(BF16)
        kh = k_all[:, sl] + kr
        rk = lax.rsqrt(jnp.sum(kh * kh, axis=-1, keepdims=True) * inv_n + EPS)
        k_ref[0, :, sl] = ((kh * k_cos + k_sin) * rk).astype(BF16)


def _in_proj(x, pos3, shift1, scale1, g1, wqkr, wlru, wmg, gqa, wuq, gkva, wuk, wuv, gqn, gkn, freq):
    B, S, D = x.shape
    tm = TM_PROJ
    HP = MLA_HEADS * LANES
    tok = lambda w: pl.BlockSpec((1, tm, w), lambda b, s: (b, s, 0))
    per_b = pl.BlockSpec((1, 1, D), lambda b, s: (b, 0, 0))
    full = lambda a: pl.BlockSpec(a.shape, lambda b, s: (0,) * a.ndim)
    return pl.pallas_call(
        _in_proj_body,
        grid=(B, S // tm),
        in_specs=[tok(D), tok(1), per_b, per_b, full(g1), full(wqkr), full(wlru), full(wmg),
                  full(gqa), full(wuq), full(gkva), full(wuk), full(wuv), full(gqn), full(gkn),
                  full(freq)],
        out_specs=[tok(HP), tok(HP), tok(HP), tok(D), tok(D), tok(2 * D)],
        out_shape=[jax.ShapeDtypeStruct((B, S, HP), BF16)] * 3
        + [jax.ShapeDtypeStruct((B, S, D), F32)] * 2 + [jax.ShapeDtypeStruct((B, S, 2 * D), F32)],
        compiler_params=_cparams(("parallel", "parallel")),
        name="in_proj",
    )(x, pos3, shift1, scale1, g1, wqkr, wlru, wmg, gqa, wuq, gkva, wuk, wuv, gqn, gkn, freq)


def _attn_body(q_ref, k_ref, v_ref, o_ref):
    qi = pl.program_id(2)
    neg = float(jnp.finfo(jnp.float32).min)
    shift = CHUNK.bit_length() - 1
    qc = jnp.right_shift(lax.broadcasted_iota(jnp.int32, (TQ, TQ), 0), shift)
    kc = jnp.right_shift(lax.broadcasted_iota(jnp.int32, (TQ, TQ), 1), shift)
    allowed = kc <= qc

    for nk in range(1, k_ref.shape[1] // TQ + 1):
        @pl.when(qi == nk - 1)
        def _(nk=nk):
            n_keys = nk * TQ
            for hh in range(ATT_HEADS_PER_STEP):
                hl = slice(hh * LANES, (hh + 1) * LANES)
                q = q_ref[0, :, hl]
                s = _dot_nt(q, k_ref[0, :n_keys, hl])
                diag = jnp.where(allowed, s[:, n_keys - TQ:], neg)
                s = diag if nk == 1 else jnp.concatenate([s[:, :n_keys - TQ], diag], axis=1)
                p = jnp.exp(s - jnp.max(s, axis=-1, keepdims=True))
                l = jnp.sum(p, axis=-1, keepdims=True)
                o_ref[0, :, hl] = (_dot(p.astype(BF16), v_ref[0, :n_keys, hl]) / l).astype(BF16)


def _attention(q, k, v):
    B, S, HP = q.shape
    hw = ATT_HEADS_PER_STEP * LANES
    H = HP // hw
    return pl.pallas_call(
        _attn_body,
        grid=(B, H, S // TQ),
        in_specs=[pl.BlockSpec((1, TQ, hw), lambda b, h, i: (b, i, h)),
                  pl.BlockSpec((1, S, hw), lambda b, h, i: (b, 0, h)),
                  pl.BlockSpec((1, S, hw), lambda b, h, i: (b, 0, h))],
        out_specs=pl.BlockSpec((1, TQ, hw), lambda b, h, i: (b, i, h)),
        out_shape=jax.ShapeDtypeStruct((B, S, HP), BF16),
        compiler_params=_cparams(("parallel", "parallel", "arbitrary")),
        name="attn",
    )(q, k, v)


def _lru_body(lx_ref, lg_ref, cw_ref, cb_ref, wg_ref, brg_ref, big_ref, lam_ref, rec_ref, prev_ref, h_ref):
    si = pl.program_id(1)

    @pl.when(si == 0)
    def _():
        prev_ref[...] = jnp.zeros_like(prev_ref)
        h_ref[...] = jnp.zeros_like(h_ref)

    x = lx_ref[0]
    ts, C = x.shape
    prev8 = prev_ref[...]
    row8 = lax.broadcasted_iota(jnp.int32, prev8.shape, 0)
    xc = x * cw_ref[CONV_WIDTH - 1:CONV_WIDTH, :] + cb_ref[...]
    for d in range(1, CONV_WIDTH):
        xr = pltpu.roll(x, d, 0)
        first = jnp.where(row8 < d, pltpu.roll(prev8, d, 0), xr[:SUBLANES])
        xs = jnp.concatenate([first, xr[SUBLANES:]], axis=0)
        xc = xc + xs * cw_ref[CONV_WIDTH - 1 - d:CONV_WIDTH - d, :]
    prev_ref[...] = x[ts - SUBLANES:]

    xcb = xc.astype(BF16)
    rowmod = jnp.bitwise_and(lax.broadcasted_iota(jnp.int32, (ts, LANES), 0), SUBLANES - 1)
    for blk in range(LRU_BLOCKS):
        sl = slice(blk * LANES, (blk + 1) * LANES)
        xb = xc[:, sl]
        g = _dot(xcb[:, sl], wg_ref[blk])
        r = jax.nn.sigmoid(g[:, :LANES] + brg_ref[:, sl])
        i = jax.nn.sigmoid(g[:, LANES:] + big_ref[:, sl])
        nl = -lam_ref[:, sl]
        softplus = jnp.maximum(nl, 0.0) + jnp.log(1.0 + jnp.exp(-jnp.abs(nl)))
        log_a = (-LRU_C * r) * softplus
        a = jnp.exp(log_a)
        b = jnp.sqrt(1.0 - a * a) * (i * xb)
        for d in (1, 2, 4):
            valid = rowmod >= d
            b = jnp.where(valid, a * pltpu.roll(b, d, 0) + b, b)
            a = jnp.where(valid, a * pltpu.roll(a, d, 0), a)
        carry = h_ref[:, sl]
        hs = []
        for gi in range(ts // SUBLANES):
            rs = slice(gi * SUBLANES, (gi + 1) * SUBLANES)
            hg = a[rs] * carry + b[rs]
            carry = hg[SUBLANES - 1:SUBLANES]
            hs.append(hg)
        h_ref[:, sl] = carry
        hfull = jnp.concatenate(hs, axis=0)
        rec_ref[0, :, sl] = (hfull * jax.nn.gelu(lg_ref[0, :, sl])).astype(BF16)


def _lru(lx, lg, cw, cb, wg, brg, big, lam):
    B, S, C = lx.shape
    ts = TS_LRU
    tok = pl.BlockSpec((1, ts, C), lambda b, s: (b, s, 0))
    full = lambda a: pl.BlockSpec(a.shape, lambda b, s: (0,) * a.ndim)
    return pl.pallas_call(
        _lru_body,
        grid=(B, S // ts),
        in_specs=[tok, tok, full(cw), full(cb), full(wg), full(brg), full(big), full(lam)],
        out_specs=tok,
        out_shape=jax.ShapeDtypeStruct((B, S, C), BF16),
        scratch_shapes=[pltpu.VMEM((SUBLANES, C), F32), pltpu.VMEM((1, C), F32)],
        compiler_params=_cparams(("parallel", "arbitrary")),
        name="lru",
    )(lx, lg, cw, cb, wg, brg, big, lam)


def _merge_body(o_ref, rec_ref, mg_ref, x_ref, gate1_ref, shift2_ref, scale2_ref, g2_ref,
                woa_ref, wob_ref, wout_ref, wq_ref, keys_ref, x1_ref, h2_ref, sc_ref):
    D = x_ref.shape[-1]
    ya = _dot(o_ref[0], woa_ref[...])
    yb = _dot(rec_ref[0], wob_ref[...])
    mg = mg_ref[0]
    y = jax.nn.sigmoid(mg[:, :D]) * ya + jax.nn.sigmoid(mg[:, D:]) * yb
    x1 = x_ref[0] + gate1_ref[0] * _dot(y.astype(BF16), wout_ref[...])
    x1_ref[0] = x1
    h2 = x1 * lax.rsqrt(jnp.mean(x1 * x1, axis=-1, keepdims=True) + EPS) * g2_ref[...]
    h2 = h2 * (1.0 + scale2_ref[0]) + shift2_ref[0]
    h2b = h2.astype(BF16)
    h2_ref[...] = h2.T.astype(PEER_MM_DTYPE)
    qp = _dot(h2b, wq_ref[...]).astype(BF16)
    for g in range(keys_ref.shape[0]):
        sc_ref[g * N_KEYS:(g + 1) * N_KEYS, :] = _dot_nt(keys_ref[g], qp[:, g * LANES:(g + 1) * LANES])


def _merge(o, rec, mg, x, gate1, shift2, scale2, g2, woa, wob, wout, wq, keys):
    B, S, D = x.shape
    tm = TM_PROJ
    ns = S // tm
    NG = keys.shape[0]
    tok = lambda w: pl.BlockSpec((1, tm, w), lambda b, s: (b, s, 0))
    per_b = pl.BlockSpec((1, 1, D), lambda b, s: (b, 0, 0))
    full = lambda a: pl.BlockSpec(a.shape, lambda b, s: (0,) * a.ndim)
    return pl.pallas_call(
        _merge_body,
        grid=(B, ns),
        in_specs=[tok(o.shape[-1]), tok(D), tok(2 * D), tok(D), per_b, per_b, per_b, full(g2),
                  full(woa), full(wob), full(wout), full(wq), full(keys)],
        out_specs=[tok(D), pl.BlockSpec((D, tm), lambda b, s: (0, b * ns + s)),
                   pl.BlockSpec((NG * N_KEYS, tm), lambda b, s: (0, b * ns + s))],
        out_shape=[jax.ShapeDtypeStruct((B, S, D), F32), jax.ShapeDtypeStruct((D, B * S), PEER_MM_DTYPE),
                   jax.ShapeDtypeStruct((NG * N_KEYS, B * S), F32)],
        compiler_params=_cparams(("parallel", "parallel")),
        name="merge",
    )(o, rec, mg, x, gate1, shift2, scale2, g2, woa, wob, wout, wq, keys)


def _sort16_network():
    def merge(lo, hi, r):
        step = r * 2
        if step < hi - lo:
            yield from merge(lo, hi, step)
            yield from merge(lo + r, hi, step)
            yield from [(i, i + r) for i in range(lo + r, hi - r, step)]
        else:
            yield (lo, lo + r)

    def sort(lo, hi):
        if hi - lo >= 1:
            mid = lo + (hi - lo) // 2
            yield from sort(lo, mid)
            yield from sort(mid + 1, hi)
            yield from merge(lo, hi, 1)

    return tuple(sort(0, PEER_TOPK - 1))


_SORT16 = _sort16_network()


def _compare_exchange(v, i, j):
    v[i], v[j] = jnp.maximum(v[i], v[j]), jnp.minimum(v[i], v[j])


def _bitonic_merge16(v):
    d = PEER_TOPK // 2
    while d:
        for i in range(PEER_TOPK):
            if not i & d:
                _compare_exchange(v, i, i + d)
        d //= 2


def _merge_top16(best, other):
    z = [jnp.maximum(best[i], other[PEER_TOPK - 1 - i]) if PEER_TOPK - 1 - i < len(other) else best[i]
         for i in range(PEER_TOPK)]
    _bitonic_merge16(z)
    return z


def _top16_of_128(v):
    v = list(v)
    for i, j in _SORT16:
        _compare_exchange(v, i, j)
    shift = SUBLANES // 2
    while shift:
        v = _merge_top16(v, [pltpu.roll(x, shift, 0) for x in v])
        shift //= 2
    return v


def _select_body(sc_ref, e2_ref, th_ref, e1_ref):
    K = PEER_TOPK
    H = PEER_HEADS
    nv = N_KEYS // SUBLANES
    tl = sc_ref.shape[1]
    sub = lax.broadcasted_iota(jnp.int32, (SUBLANES, tl), 0)

    def group(g):
        return [sc_ref[g * N_KEYS + k * SUBLANES:g * N_KEYS + (k + 1) * SUBLANES, :] for k in range(nv)]

    t1 = [_top16_of_128(group(2 * h)) for h in range(H)]
    t2 = [_top16_of_128(group(2 * h + 1)) for h in range(H)]

    def pack(reps):
        out = reps[0]
        for h in range(1, H):
            out = jnp.where(sub == h, reps[h], out)
        return out

    def unpack(x, h):
        return jnp.broadcast_to(x[h:h + 1, :], x.shape)

    t1p = [pack([t1[h][a] for h in range(H)]) for a in range(K)]
    t2p = [pack([t2[h][b] for h in range(H)]) for b in range(K)]
    cand = [[t1p[a] + t2p[b] for b in range(K // (a + 1))] for a in range(K)]
    best = list(cand[0])
    for a in range(1, SUBLANES):
        best = _merge_top16(best, cand[a])
    best = _merge_top16(best, [cand[a][0] for a in range(SUBLANES, K)])
    thr = best[K - 1]
    z = jnp.ones_like(thr)
    for r in range(1, K):
        z = z + jnp.exp(best[r] - best[0])
    inv_z = 1.0 / z
    phi = []
    for b in range(K):
        f = jnp.full_like(thr, float("inf"))
        for a in range(K // (b + 1)):
            f = jnp.where(cand[a][b] >= thr, t1p[a], f)
        phi.append(f)

    for h in range(H):
        phi_h = [unpack(f, h) for f in phi]
        inv_z_h = unpack(inv_z, h)
        th, e1, e2 = [], [], []
        for x in group(2 * h):
            t = jnp.full_like(x, float("inf"))
            for b in range(K):
                t = jnp.where(x >= phi_h[b], t2[h][b], t)
            th.append(t)
            e1.append(jnp.exp(x - t1[h][0]))
        for x in group(2 * h + 1):
            e2.append(jnp.exp(x - t2[h][0]) * inv_z_h)
        rows = slice(h * N_KEYS, (h + 1) * N_KEYS)
        th_ref[rows, :] = jnp.concatenate(th, axis=0)
        e1_ref[rows, :] = jnp.concatenate(e1, axis=0)
        e2_ref[rows, :] = jnp.concatenate(e2, axis=0).astype(BF16)


def _select(sc):
    R, T = sc.shape
    tl = TL_SEL
    half = R // 2
    blk = pl.BlockSpec((half, tl), lambda t: (0, t))
    return pl.pallas_call(
        _select_body,
        grid=(T // tl,),
        in_specs=[pl.BlockSpec((R, tl), lambda t: (0, t))],
        out_specs=[blk, blk, blk],
        out_shape=[jax.ShapeDtypeStruct((half, T), BF16),
                   jax.ShapeDtypeStruct((half, T), F32), jax.ShapeDtypeStruct((half, T), F32)],
        compiler_params=_cparams(("parallel",)),
        name="select",
    )(sc)


def _peer_body(h2t_ref, u0_ref, un_ref, vt_ref, sc_ref, e2_ref, th_ref, e1_ref, x1_ref, gate2_ref,
               out_ref, acc_ref, at_ref, w_ref):
    e = pl.program_id(1)
    tm = h2t_ref.shape[1]
    n_i = un_ref.shape[0] // N_KEYS

    @pl.when(e == 0)
    def _():
        acc_ref[...] = jnp.zeros_like(acc_ref)
        at_ref[...] = (_dot(u0_ref[...], h2t_ref[...]) * (1.0 / PEER_U_SCALE)).astype(BF16)

    assert n_i == SUBLANES
    i0 = pl.multiple_of(e * n_i, SUBLANES)

    def row_f32(ref, h, ii, ls):
        return jnp.broadcast_to(ref[pl.ds(h * N_KEYS + i0, SUBLANES), ls][ii:ii + 1], (N_KEYS, LANES))

    def row_bf16(ref, h, ii, ls):
        one = jnp.broadcast_to(ref[pl.ds(h * N_KEYS + i0, SUBLANES), ls][ii:ii + 1], (2 * SUBLANES, LANES))
        return jnp.concatenate([one.astype(BF16)] * (N_KEYS // (2 * SUBLANES)), axis=0)

    def routing_weights(lb):
        ls = slice(lb * LANES, (lb + 1) * LANES)
        for ii in range(n_i):
            rs = slice(ii * N_KEYS, (ii + 1) * N_KEYS)
            coef = None
            for h in range(PEER_HEADS):
                hs = slice(h * N_KEYS, (h + 1) * N_KEYS)
                s2 = sc_ref[(2 * h + 1) * N_KEYS:(2 * h + 2) * N_KEYS, ls]
                e2 = e2_ref[hs, ls]
                sel = s2 >= row_f32(th_ref, h, ii, ls)
                term = jnp.where(sel, e2, jnp.zeros_like(e2)) * row_bf16(e1_ref, h, ii, ls)
                coef = term if coef is None else coef + term
            a = at_ref[rs, ls]
            cdf = 0.5 * jnp.tanh(a * (GELU_K0 + GELU_K1 * (a * a))) + 0.5
            w_ref[rs, ls] = (coef * (a * cdf)).astype(PEER_MM_DTYPE)

    half = tm // 2
    for ts in (slice(0, half), slice(half, tm)):
        for lb in range(ts.start // LANES, ts.stop // LANES):
            routing_weights(lb)
        at_ref[:, ts] = (_dot(un_ref[...], h2t_ref[:, ts]) * (1.0 / PEER_U_SCALE)).astype(BF16)
        acc_ref[:, ts] += _dot(vt_ref[...], w_ref[:, ts])

    @pl.when(e == pl.num_programs(1) - 1)
    def _():
        out_ref[...] = x1_ref[...] + gate2_ref[0] * acc_ref[...].T


def _peer(h2t, u, vt, sc, e2, th, e1, x1, gate2, seq_len):
    D, T = h2t.shape
    NE = u.shape[0]
    tm, te = TM_PEER, TE_PEER
    tiles_per_seq = seq_len // tm
    table = pl.BlockSpec((e2.shape[0], tm), lambda t, e: (0, t))
    return pl.pallas_call(
        _peer_body,
        grid=(T // tm, NE // te),
        in_specs=[pl.BlockSpec((D, tm), lambda t, e: (0, t)),
                  pl.BlockSpec((te, D), lambda t, e: (0, 0)),
                  pl.BlockSpec((te, D), lambda t, e: (jnp.minimum(e + 1, NE // te - 1), 0)),
                  pl.BlockSpec((D, te), lambda t, e: (0, e)),
                  pl.BlockSpec((sc.shape[0], tm), lambda t, e: (0, t)), table, table, table,
                  pl.BlockSpec((tm, D), lambda t, e: (t, 0)),
                  pl.BlockSpec((1, 1, D), lambda t, e: (t // tiles_per_seq, 0, 0))],
        out_specs=pl.BlockSpec((tm, D), lambda t, e: (t, 0)),
        out_shape=jax.ShapeDtypeStruct((T, D), F32),
        scratch_shapes=[pltpu.VMEM((D, tm), F32), pltpu.VMEM((te, tm), BF16), pltpu.VMEM((te, tm), PEER_MM_DTYPE)],
        compiler_params=_cparams(("parallel", "arbitrary")),
        name="peer",
    )(h2t, u, u, vt, sc, e2, th, e1, x1, gate2)


def _layer(x, cs_mod, positions, p, l):
    B, S, D = x.shape
    H = MLA_HEADS
    shift1, scale1, gate1, shift2, scale2, gate2 = [cs_mod[:, j].reshape(B, 1, D) for j in range(6)]
    w_in = p["w_in"][l]
    c0, c1, c2 = Q_LORA, Q_LORA + KV_LORA, Q_LORA + KV_LORA + QK_ROPE
    half = QK_ROPE // 2
    gqn = jnp.pad(p["q_norm_g"][l] * (QK_HEAD ** -0.5), (0, LANES - QK_HEAD))
    gkn = jnp.pad(p["k_norm_g"][l], (0, LANES - QK_HEAD))

    def rotary_partner(w, gain):
        wg = w * gain
        t1, t2 = wg[..., QK_NOPE:QK_NOPE + half], wg[..., QK_NOPE + half:QK_HEAD]
        lo = jnp.zeros(w.shape[:-1] + (QK_NOPE,), w.dtype)
        hi = jnp.zeros(w.shape[:-1] + (LANES - QK_HEAD,), w.dtype)
        return jnp.concatenate([lo, -t2, t1, hi], axis=-1)

    w_rope = jnp.pad(w_in[:, c1:c2], ((0, 0), (QK_NOPE, LANES - QK_HEAD)))
    wqkr = jnp.concatenate([w_in[:, :c1], w_rope, rotary_partner(w_rope, gkn)], axis=1).astype(BF16)
    wlru = w_in[:, c2:c2 + 2 * D].astype(BF16)
    wmg = w_in[:, c2 + 2 * D:].astype(BF16)
    wuq3 = jnp.pad(p["w_uq"][l].reshape(Q_LORA, H, QK_HEAD), ((0, 0), (0, 0), (0, LANES - QK_HEAD)))
    wuq = jnp.concatenate([wuq3.reshape(Q_LORA, H * LANES),
                           rotary_partner(wuq3, gqn).reshape(Q_LORA, H * LANES)], axis=1).astype(BF16)
    wukv = p["w_ukv"][l].reshape(KV_LORA, H, QK_NOPE + V_HEAD)
    wuk = jnp.pad(wukv[:, :, :QK_NOPE], ((0, 0), (0, 0), (0, LANES - QK_NOPE))).reshape(KV_LORA, H * LANES).astype(BF16)
    wuv = jnp.pad(wukv[:, :, QK_NOPE:], ((0, 0), (0, 0), (0, LANES - V_HEAD))).reshape(KV_LORA, H * LANES).astype(BF16)
    inv_freq = 1.0 / (ROPE_THETA ** (jnp.arange(0, QK_ROPE, 2, dtype=F32) / QK_ROPE))
    zeros_lo = jnp.zeros((QK_NOPE,), F32)
    zeros_hi = jnp.zeros((LANES - QK_HEAD,), F32)
    freq = jnp.concatenate([zeros_lo, inv_freq, inv_freq, zeros_hi]).reshape(1, LANES)

    q, k, v, lx, lg, mg = _in_proj(
        x, positions.reshape(B, S, 1), shift1, scale1, p["norm1_g"][l].reshape(1, D), wqkr, wlru, wmg,
        p["q_a_norm_g"][l].reshape(1, Q_LORA), wuq, p["kv_a_norm_g"][l].reshape(1, KV_LORA), wuk, wuv,
        gqn.reshape(1, LANES), gkn.reshape(1, LANES), freq)
    o = _attention(q, k, v)

    C = p["conv_b"].shape[-1]
    wg = jnp.concatenate([p["w_rg"][l], p["w_ig"][l]], axis=-1).astype(BF16)
    rec = _lru(lx, lg, p["conv_w"][l], p["conv_b"][l].reshape(1, C), wg, p["b_rg"][l].reshape(1, C),
               p["b_ig"][l].reshape(1, C), p["lru_lambda"][l].reshape(1, C))

    woa = jnp.pad(p["w_o_attn"][l].reshape(H, V_HEAD, D), ((0, 0), (0, LANES - V_HEAD), (0, 0)))
    woa = woa.reshape(H * LANES, D).astype(BF16)
    keys = p["sub_keys"][l].reshape(PEER_HEADS * 2, N_KEYS, -1).astype(BF16)
    x1, h2t, sc = _merge(o, rec, mg, x, gate1, shift2, scale2, p["norm2_g"][l].reshape(1, D), woa,
                        p["w_o_lru"][l].astype(BF16), p["w_out"][l].astype(BF16), p["w_query"][l].astype(BF16), keys)
    e2, th, e1 = _select(sc)
    out = _peer(h2t, (p["expert_u"][l] * PEER_U_SCALE).astype(PEER_MM_DTYPE), p["expert_v"][l].T.astype(PEER_MM_DTYPE),
                sc, e2, th, e1, x1.reshape(B * S, D), gate2, S)
    return out.reshape(B, S, D)


def kernel(x, c, positions, w_ada, b_ada, norm1_g, w_in, q_a_norm_g, w_uq, kv_a_norm_g, w_ukv, q_norm_g, k_norm_g, w_o_attn, conv_w, conv_b, w_rg, b_rg, w_ig, b_ig, lru_lambda, w_o_lru, w_out, norm2_g, w_query, sub_keys, expert_u, expert_v):
    p = dict(w_in=w_in, norm1_g=norm1_g, q_a_norm_g=q_a_norm_g, w_uq=w_uq, kv_a_norm_g=kv_a_norm_g, w_ukv=w_ukv,
             q_norm_g=q_norm_g, k_norm_g=k_norm_g, w_o_attn=w_o_attn, conv_w=conv_w, conv_b=conv_b, w_rg=w_rg,
             b_rg=b_rg, w_ig=w_ig, b_ig=b_ig, lru_lambda=lru_lambda, w_o_lru=w_o_lru, w_out=w_out,
             norm2_g=norm2_g, w_query=w_query, sub_keys=sub_keys, expert_u=expert_u, expert_v=expert_v)
    B, D = c.shape
    for l in range(w_ada.shape[0]):
        mod = _mod(c, w_ada[l], b_ada[l]).reshape(B, 6, D)
        x = _layer(x, mod, positions, p, l)
    return x
```

```python
import functools

import jax
import jax.numpy as jnp
from jax import lax
from jax.experimental import pallas as pl
from jax.experimental.pallas import tpu as pltpu

F32 = jnp.float32
BF16 = jnp.bfloat16

EPS = 1e-6
LANES = 128
SUBLANES = 8
VMEM_LIMIT = 56 * 1024 * 1024

CHUNK = 64
MLA_HEADS = 8
QK_NOPE = 64
QK_ROPE = 32
QK_HEAD = QK_NOPE + QK_ROPE
V_HEAD = 64
Q_LORA = 256
KV_LORA = 128
ROPE_THETA = 10000.0
LRU_BLOCKS = 8
CONV_WIDTH = 4
LRU_C = 8.0
PEER_HEADS = 8
N_KEYS = 128
PEER_TOPK = 16
GELU_K0 = 0.7978845608028654
GELU_K1 = 0.044715 * GELU_K0

TM_PROJ = 256
TQ = 256
ATT_HEADS_PER_STEP = 4
TS_LRU = 256
TL_SEL = 128
TM_PEER = 512
TE_PEER = 1024
PEER_MM_DTYPE = jnp.float8_e4m3fn
PEER_U_SCALE = 32.0


def _cparams(sem):
    return pltpu.CompilerParams(dimension_semantics=sem, vmem_limit_bytes=VMEM_LIMIT)


def _dot(a, b):
    return jnp.dot(a, b, preferred_element_type=F32)


def _dot_nt(a, b):
    return lax.dot_general(a, b, (((1,), (1,)), ((), ())), preferred_element_type=F32)


def _mod_body(c_ref, w_ref, b_ref, o_ref):
    c = c_ref[...]
    cs = c * jax.nn.sigmoid(c)
    o_ref[...] = _dot(cs.astype(BF16), w_ref[...].astype(BF16)) + b_ref[...]


def _mod(c, w_ada, b_ada):
    B, D = c.shape
    N = w_ada.shape[1]
    tn = 1024
    return pl.pallas_call(
        _mod_body,
        grid=(N // tn,),
        in_specs=[pl.BlockSpec((B, D), lambda n: (0, 0)),
                  pl.BlockSpec((D, tn), lambda n: (0, n)),
                  pl.BlockSpec((1, tn), lambda n: (0, n))],
        out_specs=pl.BlockSpec((B, tn), lambda n: (0, n)),
        out_shape=jax.ShapeDtypeStruct((B, N), F32),
        compiler_params=_cparams(("parallel",)),
        name="mod",
    )(c, w_ada, b_ada.reshape(1, N))


def _in_proj_body(x_ref, pos_ref, shift_ref, scale_ref, g1_ref, wqkr_ref, wlru_ref, wmg_ref,
                  gqa_ref, wuq_ref, gkva_ref, wuk_ref, wuv_ref, gqn_ref, gkn_ref, freq_ref,
                  q_ref, k_ref, v_ref, lx_ref, lg_ref, mg_ref):
    HP = MLA_HEADS * LANES
    x = x_ref[0]
    D = x.shape[-1]
    y = x * lax.rsqrt(jnp.mean(x * x, axis=-1, keepdims=True) + EPS) * g1_ref[...]
    h = y * (1.0 + scale_ref[0]) + shift_ref[0]
    hb = h.astype(BF16)

    lru = _dot(hb, wlru_ref[...])
    lx_ref[0] = lru[:, :D]
    lg_ref[0] = lru[:, D:]
    mg_ref[0] = _dot(hb, wmg_ref[...])

    qkr = _dot(hb, wqkr_ref[...])
    cq = qkr[:, :Q_LORA]
    ckv = qkr[:, Q_LORA:Q_LORA + KV_LORA]
    kr = qkr[:, Q_LORA + KV_LORA:Q_LORA + KV_LORA + LANES]
    kr_partner = qkr[:, Q_LORA + KV_LORA + LANES:]
    qn = cq * lax.rsqrt(jnp.mean(cq * cq, axis=-1, keepdims=True) + EPS) * gqa_ref[...]
    kvn = ckv * lax.rsqrt(jnp.mean(ckv * ckv, axis=-1, keepdims=True) + EPS) * gkva_ref[...]
    kvb = kvn.astype(BF16)
    q_both = _dot(qn.astype(BF16), wuq_ref[...])
    k_all = _dot(kvb, wuk_ref[...])
    v_ref[0] = _dot(kvb, wuv_ref[...]).astype(BF16)

    ang = pos_ref[0].astype(F32) * freq_ref[...]
    cos_l = jnp.cos(ang)
    sin_l = jnp.sin(ang)
    q_cos = cos_l * gqn_ref[...]
    k_cos = cos_l * gkn_ref[...]
    k_sin = kr_partner * sin_l
    inv_n = 1.0 / QK_HEAD
    for hh in range(MLA_HEADS):
        sl = slice(hh * LANES, (hh + 1) * LANES)
        qh = q_both[:, sl]
        rq = lax.rsqrt(jnp.sum(qh * qh, axis=-1, keepdims=True) * inv_n + EPS)
        q_ref[0, :, sl] = ((qh * q_cos + q_both[:, HP + hh * LANES:HP + (hh + 1) * LANES] * sin_l) * rq).astype(BF16)
        kh = k_all[:, sl] + kr
        rk = lax.rsqrt(jnp.sum(kh * kh, axis=-1, keepdims=True) * inv_n + EPS)
        k_ref[0, :, sl] = ((kh * k_cos + k_sin) * rk).astype(BF16)


def _in_proj(x, pos3, shift1, scale1, g1, wqkr, wlru, wmg, gqa, wuq, gkva, wuk, wuv, gqn, gkn, freq):
    B, S, D = x.shape
    tm = TM_PROJ
    HP = MLA_HEADS * LANES
    tok = lambda w: pl.BlockSpec((1, tm, w), lambda b, s: (b, s, 0))
    per_b = pl.BlockSpec((1, 1, D), lambda b, s: (b, 0, 0))
    full = lambda a: pl.BlockSpec(a.shape, lambda b, s: (0,) * a.ndim)
    return pl.pallas_call(
        _in_proj_body,
        grid=(B, S // tm),
        in_specs=[tok(D), tok(1), per_b, per_b, full(g1), full(wqkr), full(wlru), full(wmg),
                  full(gqa), full(wuq), full(gkva), full(wuk), full(wuv), full(gqn), full(gkn),
                  full(freq)],
        out_specs=[tok(HP), tok(HP), tok(HP), tok(D), tok(D), tok(2 * D)],
        out_shape=[jax.ShapeDtypeStruct((B, S, HP), BF16)] * 3
        + [jax.ShapeDtypeStruct((B, S, D), F32)] * 2 + [jax.ShapeDtypeStruct((B, S, 2 * D), F32)],
        compiler_params=_cparams(("parallel", "parallel")),
        name="in_proj",
    )(x, pos3, shift1, scale1, g1, wqkr, wlru, wmg, gqa, wuq, gkva, wuk, wuv, gqn, gkn, freq)


def _attn_body(q_ref, k_ref, v_ref, o_ref):
    qi = pl.program_id(2)
    neg = float(jnp.finfo(jnp.float32).min)
    shift = CHUNK.bit_length() - 1
    qc = jnp.right_shift(lax.broadcasted_iota(jnp.int32, (TQ, TQ), 0), shift)
    kc = jnp.right_shift(lax.broadcasted_iota(jnp.int32, (TQ, TQ), 1), shift)
    allowed = kc <= qc

    for nk in range(1, k_ref.shape[1] // TQ + 1):
        @pl.when(qi == nk - 1)
        def _(nk=nk):
            n_keys = nk * TQ
            for hh in range(ATT_HEADS_PER_STEP):
                hl = slice(hh * LANES, (hh + 1) * LANES)
                q = q_ref[0, :, hl]
                s = _dot_nt(q, k_ref[0, :n_keys, hl])
                diag = jnp.where(allowed, s[:, n_keys - TQ:], neg)
                s = diag if nk == 1 else jnp.concatenate([s[:, :n_keys - TQ], diag], axis=1)
                p = jnp.exp(s - jnp.max(s, axis=-1, keepdims=True))
                l = jnp.sum(p, axis=-1, keepdims=True)
                o_ref[0, :, hl] = (_dot(p.astype(BF16), v_ref[0, :n_keys, hl]) / l).astype(BF16)


def _attention(q, k, v):
    B, S, HP = q.shape
    hw = ATT_HEADS_PER_STEP * LANES
    H = HP // hw
    return pl.pallas_call(
        _attn_body,
        grid=(B, H, S // TQ),
        in_specs=[pl.BlockSpec((1, TQ, hw), lambda b, h, i: (b, i, h)),
                  pl.BlockSpec((1, S, hw), lambda b, h, i: (b, 0, h)),
                  pl.BlockSpec((1, S, hw), lambda b, h, i: (b, 0, h))],
        out_specs=pl.BlockSpec((1, TQ, hw), lambda b, h, i: (b, i, h)),
        out_shape=jax.ShapeDtypeStruct((B, S, HP), BF16),
        compiler_params=_cparams(("parallel", "parallel", "arbitrary")),
        name="attn",
    )(q, k, v)


def _lru_body(lx_ref, lg_ref, cw_ref, cb_ref, wg_ref, brg_ref, big_ref, lam_ref, rec_ref, prev_ref, h_ref):
    si = pl.program_id(1)

    @pl.when(si == 0)
    def _():
        prev_ref[...] = jnp.zeros_like(prev_ref)
        h_ref[...] = jnp.zeros_like(h_ref)

    x = lx_ref[0]
    ts, C = x.shape
    prev8 = prev_ref[...]
    row8 = lax.broadcasted_iota(jnp.int32, prev8.shape, 0)
    xc = x * cw_ref[CONV_WIDTH - 1:CONV_WIDTH, :] + cb_ref[...]
    for d in range(1, CONV_WIDTH):
        xr = pltpu.roll(x, d, 0)
        first = jnp.where(row8 < d, pltpu.roll(prev8, d, 0), xr[:SUBLANES])
        xs = jnp.concatenate([first, xr[SUBLANES:]], axis=0)
        xc = xc + xs * cw_ref[CONV_WIDTH - 1 - d:CONV_WIDTH - d, :]
    prev_ref[...] = x[ts - SUBLANES:]

    xcb = xc.astype(BF16)
    rowmod = jnp.bitwise_and(lax.broadcasted_iota(jnp.int32, (ts, LANES), 0), SUBLANES - 1)
    for blk in range(LRU_BLOCKS):
        sl = slice(blk * LANES, (blk + 1) * LANES)
        xb = xc[:, sl]
        g = _dot(xcb[:, sl], wg_ref[blk])
        r = jax.nn.sigmoid(g[:, :LANES] + brg_ref[:, sl])
        i = jax.nn.sigmoid(g[:, LANES:] + big_ref[:, sl])
        nl = -lam_ref[:, sl]
        softplus = jnp.maximum(nl, 0.0) + jnp.log(1.0 + jnp.exp(-jnp.abs(nl)))
        log_a = (-LRU_C * r) * softplus
        a = jnp.exp(log_a)
        b = jnp.sqrt(1.0 - a * a) * (i * xb)
        for d in (1, 2, 4):
            valid = rowmod >= d
            b = jnp.where(valid, a * pltpu.roll(b, d, 0) + b, b)
            a = jnp.where(valid, a * pltpu.roll(a, d, 0), a)
        carry = h_ref[:, sl]
        hs = []
        for gi in range(ts // SUBLANES):
            rs = slice(gi * SUBLANES, (gi + 1) * SUBLANES)
            hg = a[rs] * carry + b[rs]
            carry = hg[SUBLANES - 1:SUBLANES]
            hs.append(hg)
        h_ref[:, sl] = carry
        hfull = jnp.concatenate(hs, axis=0)
        rec_ref[0, :, sl] = (hfull * jax.nn.gelu(lg_ref[0, :, sl])).astype(BF16)


def _lru(lx, lg, cw, cb, wg, brg, big, lam):
    B, S, C = lx.shape
    ts = TS_LRU
    tok = pl.BlockSpec((1, ts, C), lambda b, s: (b, s, 0))
    full = lambda a: pl.BlockSpec(a.shape, lambda b, s: (0,) * a.ndim)
    return pl.pallas_call(
        _lru_body,
        grid=(B, S // ts),
        in_specs=[tok, tok, full(cw), full(cb), full(wg), full(brg), full(big), full(lam)],
        out_specs=tok,
        out_shape=jax.ShapeDtypeStruct((B, S, C), BF16),
        scratch_shapes=[pltpu.VMEM((SUBLANES, C), F32), pltpu.VMEM((1, C), F32)],
        compiler_params=_cparams(("parallel", "arbitrary")),
        name="lru",
    )(lx, lg, cw, cb, wg, brg, big, lam)


def _merge_body(o_ref, rec_ref, mg_ref, x_ref, gate1_ref, shift2_ref, scale2_ref, g2_ref,
                woa_ref, wob_ref, wout_ref, wq_ref, keys_ref, x1_ref, h2_ref, sc_ref):
    D = x_ref.shape[-1]
    ya = _dot(o_ref[0], woa_ref[...])
    yb = _dot(rec_ref[0], wob_ref[...])
    mg = mg_ref[0]
    y = jax.nn.sigmoid(mg[:, :D]) * ya + jax.nn.sigmoid(mg[:, D:]) * yb
    x1 = x_ref[0] + gate1_ref[0] * _dot(y.astype(BF16), wout_ref[...])
    x1_ref[0] = x1
    h2 = x1 * lax.rsqrt(jnp.mean(x1 * x1, axis=-1, keepdims=True) + EPS) * g2_ref[...]
    h2 = h2 * (1.0 + scale2_ref[0]) + shift2_ref[0]
    h2b = h2.astype(BF16)
    h2_ref[...] = h2.T.astype(PEER_MM_DTYPE)
    qp = _dot(h2b, wq_ref[...]).astype(BF16)
    for g in range(keys_ref.shape[0]):
        sc_ref[g * N_KEYS:(g + 1) * N_KEYS, :] = _dot_nt(keys_ref[g], qp[:, g * LANES:(g + 1) * LANES])


def _merge(o, rec, mg, x, gate1, shift2, scale2, g2, woa, wob, wout, wq, keys):
    B, S, D = x.shape
    tm = TM_PROJ
    ns = S // tm
    NG = keys.shape[0]
    tok = lambda w: pl.BlockSpec((1, tm, w), lambda b, s: (b, s, 0))
    per_b = pl.BlockSpec((1, 1, D), lambda b, s: (b, 0, 0))
    full = lambda a: pl.BlockSpec(a.shape, lambda b, s: (0,) * a.ndim)
    return pl.pallas_call(
        _merge_body,
        grid=(B, ns),
        in_specs=[tok(o.shape[-1]), tok(D), tok(2 * D), tok(D), per_b, per_b, per_b, full(g2),
                  full(woa), full(wob), full(wout), full(wq), full(keys)],
        out_specs=[tok(D), pl.BlockSpec((D, tm), lambda b, s: (0, b * ns + s)),
                   pl.BlockSpec((NG * N_KEYS, tm), lambda b, s: (0, b * ns + s))],
        out_shape=[jax.ShapeDtypeStruct((B, S, D), F32), jax.ShapeDtypeStruct((D, B * S), PEER_MM_DTYPE),
                   jax.ShapeDtypeStruct((NG * N_KEYS, B * S), F32)],
        compiler_params=_cparams(("parallel", "parallel")),
        name="merge",
    )(o, rec, mg, x, gate1, shift2, scale2, g2, woa, wob, wout, wq, keys)


def _sort16_network():
    def merge(lo, hi, r):
        step = r * 2
        if step < hi - lo:
            yield from merge(lo, hi, step)
            yield from merge(lo + r, hi, step)
            yield from [(i, i + r) for i in range(lo + r, hi - r, step)]
        else:
            yield (lo, lo + r)

    def sort(lo, hi):
        if hi - lo >= 1:
            mid = lo + (hi - lo) // 2
            yield from sort(lo, mid)
            yield from sort(mid + 1, hi)
            yield from merge(lo, hi, 1)

    return tuple(sort(0, PEER_TOPK - 1))


_SORT16 = _sort16_network()


def _compare_exchange(v, i, j):
    v[i], v[j] = jnp.maximum(v[i], v[j]), jnp.minimum(v[i], v[j])


def _bitonic_merge16(v):
    d = PEER_TOPK // 2
    while d:
        for i in range(PEER_TOPK):
            if not i & d:
                _compare_exchange(v, i, i + d)
        d //= 2


def _merge_top16(best, other):
    z = [jnp.maximum(best[i], other[PEER_TOPK - 1 - i]) if PEER_TOPK - 1 - i < len(other) else best[i]
         for i in range(PEER_TOPK)]
    _bitonic_merge16(z)
    return z


def _top16_of_128(v):
    v = list(v)
    for i, j in _SORT16:
        _compare_exchange(v, i, j)
    shift = SUBLANES // 2
    while shift:
        v = _merge_top16(v, [pltpu.roll(x, shift, 0) for x in v])
        shift //= 2
    return v


def _select_body(sc_ref, r2_ref, e2_ref, c1_ref, e1_ref):
    K = PEER_TOPK
    H = PEER_HEADS
    nv = N_KEYS // SUBLANES
    tl = sc_ref.shape[1]
    sub = lax.broadcasted_iota(jnp.int32, (SUBLANES, tl), 0)

    def group(g):
        return [sc_ref[g * N_KEYS + k * SUBLANES:g * N_KEYS + (k + 1) * SUBLANES, :] for k in range(nv)]

    t1 = [_top16_of_128(group(2 * h)) for h in range(H)]
    t2 = [_top16_of_128(group(2 * h + 1)) for h in range(H)]

    def pack(reps):
        out = reps[0]
        for h in range(1, H):
            out = jnp.where(sub == h, reps[h], out)
        return out

    def unpack(x, h):
        return jnp.broadcast_to(x[h:h + 1, :], x.shape)

    t1p = [pack([t1[h][a] for h in range(H)]) for a in range(K)]
    t2p = [pack([t2[h][b] for h in range(H)]) for b in range(K)]
    cand = [[t1p[a] + t2p[b] for b in range(K // (a + 1))] for a in range(K)]
    best = list(cand[0])
    for a in range(1, SUBLANES):
        best = _merge_top16(best, cand[a])
    best = _merge_top16(best, [cand[a][0] for a in range(SUBLANES, K)])
    thr = best[K - 1]
    z = jnp.ones_like(thr)
    for r in range(1, K):
        z = z + jnp.exp(best[r] - best[0])
    inv_z = 1.0 / z
    phi = []
    for b in range(K):
        f = jnp.full_like(thr, float("inf"))
        for a in range(K // (b + 1)):
            f = jnp.where(cand[a][b] >= thr, t1p[a], f)
        phi.append(f)

    for h in range(H):
        phi_h = [unpack(f, h) for f in phi]
        inv_z_h = unpack(inv_z, h)
        c1, e1, r2, e2 = [], [], [], []
        for x in group(2 * h):
            cnt = jnp.zeros_like(x)
            for b in range(K):
                cnt = jnp.where(x >= phi_h[b], float(b + 1), cnt)
            c1.append(cnt)
            e1.append(jnp.exp(x - t1[h][0]))
        for x in group(2 * h + 1):
            rank = jnp.full_like(x, float(K))
            for r in reversed(range(K)):
                rank = jnp.where(x >= t2[h][r], float(r), rank)
            r2.append(rank)
            e2.append(jnp.exp(x - t2[h][0]) * inv_z_h)
        rows = slice(h * N_KEYS, (h + 1) * N_KEYS)
        c1_ref[rows, :] = jnp.concatenate(c1, axis=0)
        e1_ref[rows, :] = jnp.concatenate(e1, axis=0)
        r2_ref[rows, :] = jnp.concatenate(r2, axis=0).astype(BF16)
        e2_ref[rows, :] = jnp.concatenate(e2, axis=0).astype(BF16)


def _select(sc):
    R, T = sc.shape
    tl = TL_SEL
    half = R // 2
    blk = pl.BlockSpec((half, tl), lambda t: (0, t))
    return pl.pallas_call(
        _select_body,
        grid=(T // tl,),
        in_specs=[pl.BlockSpec((R, tl), lambda t: (0, t))],
        out_specs=[blk, blk, blk, blk],
        out_shape=[jax.ShapeDtypeStruct((half, T), BF16), jax.ShapeDtypeStruct((half, T), BF16),
                   jax.ShapeDtypeStruct((half, T), F32), jax.ShapeDtypeStruct((half, T), F32)],
        compiler_params=_cparams(("parallel",)),
        name="select",
    )(sc)


def _peer_body(h2t_ref, u0_ref, un_ref, vt_ref, r2_ref, e2_ref, c1_ref, e1_ref, x1_ref, gate2_ref,
               out_ref, acc_ref, at_ref, w_ref):
    e = pl.program_id(1)
    tm = h2t_ref.shape[1]
    n_i = un_ref.shape[0] // N_KEYS

    @pl.when(e == 0)
    def _():
        acc_ref[...] = jnp.zeros_like(acc_ref)
        at_ref[...] = (_dot(u0_ref[...], h2t_ref[...]) * (1.0 / PEER_U_SCALE)).astype(BF16)

    assert n_i == SUBLANES
    i0 = pl.multiple_of(e * n_i, SUBLANES)

    def row_bf16(ref, h, ii, ls):
        row = ref[pl.ds(h * N_KEYS + i0, SUBLANES), ls][ii:ii + 1]
        one = jnp.broadcast_to(row, (2 * SUBLANES, LANES)).astype(BF16)
        return jnp.concatenate([one] * (N_KEYS // (2 * SUBLANES)), axis=0)

    def routing_weights(lb):
        ls = slice(lb * LANES, (lb + 1) * LANES)
        for ii in range(n_i):
            rs = slice(ii * N_KEYS, (ii + 1) * N_KEYS)
            coef = None
            for h in range(PEER_HEADS):
                hs = slice(h * N_KEYS, (h + 1) * N_KEYS)
                e2 = e2_ref[hs, ls]
                sel = r2_ref[hs, ls] < row_bf16(c1_ref, h, ii, ls)
                term = jnp.where(sel, e2, jnp.zeros_like(e2)) * row_bf16(e1_ref, h, ii, ls)
                coef = term if coef is None else coef + term
            a = at_ref[rs, ls]
            cdf = 0.5 * jnp.tanh(a * (GELU_K0 + GELU_K1 * (a * a))) + 0.5
            w_ref[rs, ls] = (coef * (a * cdf)).astype(PEER_MM_DTYPE)

    half = tm // 2
    for ts in (slice(0, half), slice(half, tm)):
        for lb in range(ts.start // LANES, ts.stop // LANES):
            routing_weights(lb)
        at_ref[:, ts] = (_dot(un_ref[...], h2t_ref[:, ts]) * (1.0 / PEER_U_SCALE)).astype(BF16)
        acc_ref[:, ts] += _dot(vt_ref[...], w_ref[:, ts])

    @pl.when(e == pl.num_programs(1) - 1)
    def _():
        out_ref[...] = x1_ref[...] + gate2_ref[0] * acc_ref[...].T


def _peer(h2t, u, vt, r2, e2, c1, e1, x1, gate2, seq_len):
    D, T = h2t.shape
    NE = u.shape[0]
    tm, te = TM_PEER, TE_PEER
    tiles_per_seq = seq_len // tm
    table = pl.BlockSpec((r2.shape[0], tm), lambda t, e: (0, t))
    return pl.pallas_call(
        _peer_body,
        grid=(T // tm, NE // te),
        in_specs=[pl.BlockSpec((D, tm), lambda t, e: (0, t)),
                  pl.BlockSpec((te, D), lambda t, e: (0, 0)),
                  pl.BlockSpec((te, D), lambda t, e: (jnp.minimum(e + 1, NE // te - 1), 0)),
                  pl.BlockSpec((D, te), lambda t, e: (0, e)),
                  table, table, table, table,
                  pl.BlockSpec((tm, D), lambda t, e: (t, 0)),
                  pl.BlockSpec((1, 1, D), lambda t, e: (t // tiles_per_seq, 0, 0))],
        out_specs=pl.BlockSpec((tm, D), lambda t, e: (t, 0)),
        out_shape=jax.ShapeDtypeStruct((T, D), F32),
        scratch_shapes=[pltpu.VMEM((D, tm), F32), pltpu.VMEM((te, tm), BF16), pltpu.VMEM((te, tm), PEER_MM_DTYPE)],
        compiler_params=_cparams(("parallel", "arbitrary")),
        name="peer",
    )(h2t, u, u, vt, r2, e2, c1, e1, x1, gate2)


def _layer(x, cs_mod, positions, p, l):
    B, S, D = x.shape
    H = MLA_HEADS
    shift1, scale1, gate1, shift2, scale2, gate2 = [cs_mod[:, j].reshape(B, 1, D) for j in range(6)]
    w_in = p["w_in"][l]
    c0, c1, c2 = Q_LORA, Q_LORA + KV_LORA, Q_LORA + KV_LORA + QK_ROPE
    half = QK_ROPE // 2
    gqn = jnp.pad(p["q_norm_g"][l] * (QK_HEAD ** -0.5), (0, LANES - QK_HEAD))
    gkn = jnp.pad(p["k_norm_g"][l], (0, LANES - QK_HEAD))

    def rotary_partner(w, gain):
        wg = w * gain
        t1, t2 = wg[..., QK_NOPE:QK_NOPE + half], wg[..., QK_NOPE + half:QK_HEAD]
        lo = jnp.zeros(w.shape[:-1] + (QK_NOPE,), w.dtype)
        hi = jnp.zeros(w.shape[:-1] + (LANES - QK_HEAD,), w.dtype)
        return jnp.concatenate([lo, -t2, t1, hi], axis=-1)

    w_rope = jnp.pad(w_in[:, c1:c2], ((0, 0), (QK_NOPE, LANES - QK_HEAD)))
    wqkr = jnp.concatenate([w_in[:, :c1], w_rope, rotary_partner(w_rope, gkn)], axis=1).astype(BF16)
    wlru = w_in[:, c2:c2 + 2 * D].astype(BF16)
    wmg = w_in[:, c2 + 2 * D:].astype(BF16)
    wuq3 = jnp.pad(p["w_uq"][l].reshape(Q_LORA, H, QK_HEAD), ((0, 0), (0, 0), (0, LANES - QK_HEAD)))
    wuq = jnp.concatenate([wuq3.reshape(Q_LORA, H * LANES),
                           rotary_partner(wuq3, gqn).reshape(Q_LORA, H * LANES)], axis=1).astype(BF16)
    wukv = p["w_ukv"][l].reshape(KV_LORA, H, QK_NOPE + V_HEAD)
    wuk = jnp.pad(wukv[:, :, :QK_NOPE], ((0, 0), (0, 0), (0, LANES - QK_NOPE))).reshape(KV_LORA, H * LANES).astype(BF16)
    wuv = jnp.pad(wukv[:, :, QK_NOPE:], ((0, 0), (0, 0), (0, LANES - V_HEAD))).reshape(KV_LORA, H * LANES).astype(BF16)
    inv_freq = 1.0 / (ROPE_THETA ** (jnp.arange(0, QK_ROPE, 2, dtype=F32) / QK_ROPE))
    zeros_lo = jnp.zeros((QK_NOPE,), F32)
    zeros_hi = jnp.zeros((LANES - QK_HEAD,), F32)
    freq = jnp.concatenate([zeros_lo, inv_freq, inv_freq, zeros_hi]).reshape(1, LANES)

    q, k, v, lx, lg, mg = _in_proj(
        x, positions.reshape(B, S, 1), shift1, scale1, p["norm1_g"][l].reshape(1, D), wqkr, wlru, wmg,
        p["q_a_norm_g"][l].reshape(1, Q_LORA), wuq, p["kv_a_norm_g"][l].reshape(1, KV_LORA), wuk, wuv,
        gqn.reshape(1, LANES), gkn.reshape(1, LANES), freq)
    o = _attention(q, k, v)

    C = p["conv_b"].shape[-1]
    wg = jnp.concatenate([p["w_rg"][l], p["w_ig"][l]], axis=-1).astype(BF16)
    rec = _lru(lx, lg, p["conv_w"][l], p["conv_b"][l].reshape(1, C), wg, p["b_rg"][l].reshape(1, C),
               p["b_ig"][l].reshape(1, C), p["lru_lambda"][l].reshape(1, C))

    woa = jnp.pad(p["w_o_attn"][l].reshape(H, V_HEAD, D), ((0, 0), (0, LANES - V_HEAD), (0, 0)))
    woa = woa.reshape(H * LANES, D).astype(BF16)
    keys = p["sub_keys"][l].reshape(PEER_HEADS * 2, N_KEYS, -1).astype(BF16)
    x1, h2t, sc = _merge(o, rec, mg, x, gate1, shift2, scale2, p["norm2_g"][l].reshape(1, D), woa,
                        p["w_o_lru"][l].astype(BF16), p["w_out"][l].astype(BF16), p["w_query"][l].astype(BF16), keys)
    r2, e2, c1, e1 = _select(sc)
    out = _peer(h2t, (p["expert_u"][l] * PEER_U_SCALE).astype(PEER_MM_DTYPE), p["expert_v"][l].T.astype(PEER_MM_DTYPE),
                r2, e2, c1, e1, x1.reshape(B * S, D), gate2, S)
    return out.reshape(B, S, D)


def kernel(x, c, positions, w_ada, b_ada, norm1_g, w_in, q_a_norm_g, w_uq, kv_a_norm_g, w_ukv, q_norm_g, k_norm_g, w_o_attn, conv_w, conv_b, w_rg, b_rg, w_ig, b_ig, lru_lambda, w_o_lru, w_out, norm2_g, w_query, sub_keys, expert_u, expert_v):
    p = dict(w_in=w_in, norm1_g=norm1_g, q_a_norm_g=q_a_norm_g, w_uq=w_uq, kv_a_norm_g=kv_a_norm_g, w_ukv=w_ukv,
             q_norm_g=q_norm_g, k_norm_g=k_norm_g, w_o_attn=w_o_attn, conv_w=conv_w, conv_b=conv_b, w_rg=w_rg,
             b_rg=b_rg, w_ig=w_ig, b_ig=b_ig, lru_lambda=lru_lambda, w_o_lru=w_o_lru, w_out=w_out,
             norm2_g=norm2_g, w_query=w_query, sub_keys=sub_keys, expert_u=expert_u, expert_v=expert_v)
    B, D = c.shape
    for l in range(w_ada.shape[0]):
        mod = _mod(c, w_ada[l], b_ada[l]).reshape(B, 6, D)
        x = _layer(x, mod, positions, p, l)
    return x
```

```python
import functools

import jax
import jax.numpy as jnp
from jax import lax
from jax.experimental import pallas as pl
from jax.experimental.pallas import tpu as pltpu

F32 = jnp.float32
BF16 = jnp.bfloat16

EPS = 1e-6
LANES = 128
SUBLANES = 8
VMEM_LIMIT = 56 * 1024 * 1024

CHUNK = 64
MLA_HEADS = 8
QK_NOPE = 64
QK_ROPE = 32
QK_HEAD = QK_NOPE + QK_ROPE
V_HEAD = 64
Q_LORA = 256
KV_LORA = 128
ROPE_THETA = 10000.0
LRU_BLOCKS = 8
CONV_WIDTH = 4
LRU_C = 8.0
PEER_HEADS = 8
N_KEYS = 128
PEER_TOPK = 16
GELU_K0 = 0.7978845608028654
GELU_K1 = 0.044715 * GELU_K0

TM_PROJ = 256
TQ = 256
ATT_HEADS_PER_STEP = 4
TS_LRU = 256
TL_SEL = 128
TM_PEER = 512
TE_PEER = 1024
PEER_MM_DTYPE = jnp.float8_e4m3fn
PEER_U_SCALE = 32.0


def _cparams(sem):
    return pltpu.CompilerParams(dimension_semantics=sem, vmem_limit_bytes=VMEM_LIMIT)


def _dot(a, b):
    return jnp.dot(a, b, preferred_element_type=F32)


def _dot_nt(a, b):
    return lax.dot_general(a, b, (((1,), (1,)), ((), ())), preferred_element_type=F32)


def _mod_body(c_ref, w_ref, b_ref, o_ref):
    c = c_ref[...]
    cs = c * jax.nn.sigmoid(c)
    o_ref[...] = _dot(cs.astype(BF16), w_ref[...].astype(BF16)) + b_ref[...]


def _mod(c, w_ada, b_ada):
    B, D = c.shape
    N = w_ada.shape[1]
    tn = 1024
    return pl.pallas_call(
        _mod_body,
        grid=(N // tn,),
        in_specs=[pl.BlockSpec((B, D), lambda n: (0, 0)),
                  pl.BlockSpec((D, tn), lambda n: (0, n)),
                  pl.BlockSpec((1, tn), lambda n: (0, n))],
        out_specs=pl.BlockSpec((B, tn), lambda n: (0, n)),
        out_shape=jax.ShapeDtypeStruct((B, N), F32),
        compiler_params=_cparams(("parallel",)),
        name="mod",
    )(c, w_ada, b_ada.reshape(1, N))


def _in_proj_body(x_ref, pos_ref, shift_ref, scale_ref, g1_ref, wqkr_ref, wlru_ref, wmg_ref,
                  gqa_ref, wuq_ref, gkva_ref, wuk_ref, wuv_ref, gqn_ref, gkn_ref, freq_ref,
                  q_ref, k_ref, v_ref, lx_ref, lg_ref, mg_ref):
    HP = MLA_HEADS * LANES
    x = x_ref[0]
    D = x.shape[-1]
    y = x * lax.rsqrt(jnp.mean(x * x, axis=-1, keepdims=True) + EPS) * g1_ref[...]
    h = y * (1.0 + scale_ref[0]) + shift_ref[0]
    hb = h.astype(BF16)

    lru = _dot(hb, wlru_ref[...])
    lx_ref[0] = lru[:, :D]
    lg_ref[0] = lru[:, D:]
    mg_ref[0] = _dot(hb, wmg_ref[...])

    qkr = _dot(hb, wqkr_ref[...])
    cq = qkr[:, :Q_LORA]
    ckv = qkr[:, Q_LORA:Q_LORA + KV_LORA]
    kr = qkr[:, Q_LORA + KV_LORA:Q_LORA + KV_LORA + LANES]
    kr_partner = qkr[:, Q_LORA + KV_LORA + LANES:]
    qn = cq * lax.rsqrt(jnp.mean(cq * cq, axis=-1, keepdims=True) + EPS) * gqa_ref[...]
    kvn = ckv * lax.rsqrt(jnp.mean(ckv * ckv, axis=-1, keepdims=True) + EPS) * gkva_ref[...]
    kvb = kvn.astype(BF16)
    q_both = _dot(qn.astype(BF16), wuq_ref[...])
    k_all = _dot(kvb, wuk_ref[...])
    v_ref[0] = _dot(kvb, wuv_ref[...]).astype(BF16)

    ang = pos_ref[0].astype(F32) * freq_ref[...]
    cos_l = jnp.cos(ang)
    sin_l = jnp.sin(ang)
    q_cos = cos_l * gqn_ref[...]
    k_cos = cos_l * gkn_ref[...]
    k_sin = kr_partner * sin_l
    inv_n = 1.0 / QK_HEAD
    for hh in range(MLA_HEADS):
        sl = slice(hh * LANES, (hh + 1) * LANES)
        qh = q_both[:, sl]
        rq = lax.rsqrt(jnp.sum(qh * qh, axis=-1, keepdims=True) * inv_n + EPS)
        q_ref[0, :, sl] = ((qh * q_cos + q_both[:, HP + hh * LANES:HP + (hh + 1) * LANES] * sin_l) * rq).astype(BF16)
        kh = k_all[:, sl] + kr
        rk = lax.rsqrt(jnp.sum(kh * kh, axis=-1, keepdims=True) * inv_n + EPS)
        k_ref[0, :, sl] = ((kh * k_cos + k_sin) * rk).astype(BF16)


def _in_proj(x, pos3, shift1, scale1, g1, wqkr, wlru, wmg, gqa, wuq, gkva, wuk, wuv, gqn, gkn, freq):
    B, S, D = x.shape
    tm = TM_PROJ
    HP = MLA_HEADS * LANES
    tok = lambda w: pl.BlockSpec((1, tm, w), lambda b, s: (b, s, 0))
    per_b = pl.BlockSpec((1, 1, D), lambda b, s: (b, 0, 0))
    full = lambda a: pl.BlockSpec(a.shape, lambda b, s: (0,) * a.ndim)
    return pl.pallas_call(
        _in_proj_body,
        grid=(B, S // tm),
        in_specs=[tok(D), tok(1), per_b, per_b, full(g1), full(wqkr), full(wlru), full(wmg),
                  full(gqa), full(wuq), full(gkva), full(wuk), full(wuv), full(gqn), full(gkn),
                  full(freq)],
        out_specs=[tok(HP), tok(HP), tok(HP), tok(D), tok(D), tok(2 * D)],
        out_shape=[jax.ShapeDtypeStruct((B, S, HP), BF16)] * 3
        + [jax.ShapeDtypeStruct((B, S, D), F32)] * 2 + [jax.ShapeDtypeStruct((B, S, 2 * D), F32)],
        compiler_params=_cparams(("parallel", "parallel")),
        name="in_proj",
    )(x, pos3, shift1, scale1, g1, wqkr, wlru, wmg, gqa, wuq, gkva, wuk, wuv, gqn, gkn, freq)


def _attn_body(q_ref, k_ref, v_ref, o_ref):
    qi = pl.program_id(2)
    neg = float(jnp.finfo(jnp.float32).min)
    shift = CHUNK.bit_length() - 1
    qc = jnp.right_shift(lax.broadcasted_iota(jnp.int32, (TQ, TQ), 0), shift)
    kc = jnp.right_shift(lax.broadcasted_iota(jnp.int32, (TQ, TQ), 1), shift)
    allowed = kc <= qc

    for nk in range(1, k_ref.shape[1] // TQ + 1):
        @pl.when(qi == nk - 1)
        def _(nk=nk):
            n_keys = nk * TQ
            for hh in range(ATT_HEADS_PER_STEP):
                hl = slice(hh * LANES, (hh + 1) * LANES)
                q = q_ref[0, :, hl]
                s = _dot_nt(q, k_ref[0, :n_keys, hl])
                diag = jnp.where(allowed, s[:, n_keys - TQ:], neg)
                s = diag if nk == 1 else jnp.concatenate([s[:, :n_keys - TQ], diag], axis=1)
                p = jnp.exp(s - jnp.max(s, axis=-1, keepdims=True))
                l = jnp.sum(p, axis=-1, keepdims=True)
                o_ref[0, :, hl] = (_dot(p.astype(BF16), v_ref[0, :n_keys, hl]) / l).astype(BF16)


def _attention(q, k, v):
    B, S, HP = q.shape
    hw = ATT_HEADS_PER_STEP * LANES
    H = HP // hw
    return pl.pallas_call(
        _attn_body,
        grid=(B, H, S // TQ),
        in_specs=[pl.BlockSpec((1, TQ, hw), lambda b, h, i: (b, i, h)),
                  pl.BlockSpec((1, S, hw), lambda b, h, i: (b, 0, h)),
                  pl.BlockSpec((1, S, hw), lambda b, h, i: (b, 0, h))],
        out_specs=pl.BlockSpec((1, TQ, hw), lambda b, h, i: (b, i, h)),
        out_shape=jax.ShapeDtypeStruct((B, S, HP), BF16),
        compiler_params=_cparams(("parallel", "parallel", "arbitrary")),
        name="attn",
    )(q, k, v)


def _lru_body(lx_ref, lg_ref, cw_ref, cb_ref, wg_ref, brg_ref, big_ref, lam_ref, rec_ref, prev_ref, h_ref):
    si = pl.program_id(1)

    @pl.when(si == 0)
    def _():
        prev_ref[...] = jnp.zeros_like(prev_ref)
        h_ref[...] = jnp.zeros_like(h_ref)

    x = lx_ref[0]
    ts, C = x.shape
    prev8 = prev_ref[...]
    row8 = lax.broadcasted_iota(jnp.int32, prev8.shape, 0)
    xc = x * cw_ref[CONV_WIDTH - 1:CONV_WIDTH, :] + cb_ref[...]
    for d in range(1, CONV_WIDTH):
        xr = pltpu.roll(x, d, 0)
        first = jnp.where(row8 < d, pltpu.roll(prev8, d, 0), xr[:SUBLANES])
        xs = jnp.concatenate([first, xr[SUBLANES:]], axis=0)
        xc = xc + xs * cw_ref[CONV_WIDTH - 1 - d:CONV_WIDTH - d, :]
    prev_ref[...] = x[ts - SUBLANES:]

    xcb = xc.astype(BF16)
    rowmod = jnp.bitwise_and(lax.broadcasted_iota(jnp.int32, (ts, LANES), 0), SUBLANES - 1)
    for blk in range(LRU_BLOCKS):
        sl = slice(blk * LANES, (blk + 1) * LANES)
        xb = xc[:, sl]
        g = _dot(xcb[:, sl], wg_ref[blk])
        r = jax.nn.sigmoid(g[:, :LANES] + brg_ref[:, sl])
        i = jax.nn.sigmoid(g[:, LANES:] + big_ref[:, sl])
        nl = -lam_ref[:, sl]
        softplus = jnp.maximum(nl, 0.0) + jnp.log(1.0 + jnp.exp(-jnp.abs(nl)))
        log_a = (-LRU_C * r) * softplus
        a = jnp.exp(log_a)
        b = jnp.sqrt(1.0 - a * a) * (i * xb)
        for d in (1, 2, 4):
            valid = rowmod >= d
            b = jnp.where(valid, a * pltpu.roll(b, d, 0) + b, b)
            a = jnp.where(valid, a * pltpu.roll(a, d, 0), a)
        carry = h_ref[:, sl]
        hs = []
        for gi in range(ts // SUBLANES):
            rs = slice(gi * SUBLANES, (gi + 1) * SUBLANES)
            hg = a[rs] * carry + b[rs]
            carry = hg[SUBLANES - 1:SUBLANES]
            hs.append(hg)
        h_ref[:, sl] = carry
        hfull = jnp.concatenate(hs, axis=0)
        rec_ref[0, :, sl] = (hfull * jax.nn.gelu(lg_ref[0, :, sl])).astype(BF16)


def _lru(lx, lg, cw, cb, wg, brg, big, lam):
    B, S, C = lx.shape
    ts = TS_LRU
    tok = pl.BlockSpec((1, ts, C), lambda b, s: (b, s, 0))
    full = lambda a: pl.BlockSpec(a.shape, lambda b, s: (0,) * a.ndim)
    return pl.pallas_call(
        _lru_body,
        grid=(B, S // ts),
        in_specs=[tok, tok, full(cw), full(cb), full(wg), full(brg), full(big), full(lam)],
        out_specs=tok,
        out_shape=jax.ShapeDtypeStruct((B, S, C), BF16),
        scratch_shapes=[pltpu.VMEM((SUBLANES, C), F32), pltpu.VMEM((1, C), F32)],
        compiler_params=_cparams(("parallel", "arbitrary")),
        name="lru",
    )(lx, lg, cw, cb, wg, brg, big, lam)


def _merge_body(o_ref, rec_ref, mg_ref, x_ref, gate1_ref, shift2_ref, scale2_ref, g2_ref,
                woa_ref, wob_ref, wout_ref, wq_ref, keys_ref, x1_ref, h2_ref, sc_ref):
    D = x_ref.shape[-1]
    ya = _dot(o_ref[0], woa_ref[...])
    yb = _dot(rec_ref[0], wob_ref[...])
    mg = mg_ref[0]
    y = jax.nn.sigmoid(mg[:, :D]) * ya + jax.nn.sigmoid(mg[:, D:]) * yb
    x1 = x_ref[0] + gate1_ref[0] * _dot(y.astype(BF16), wout_ref[...])
    x1_ref[0] = x1
    h2 = x1 * lax.rsqrt(jnp.mean(x1 * x1, axis=-1, keepdims=True) + EPS) * g2_ref[...]
    h2 = h2 * (1.0 + scale2_ref[0]) + shift2_ref[0]
    h2b = h2.astype(BF16)
    h2_ref[...] = h2.T.astype(PEER_MM_DTYPE)
    qp = _dot(h2b, wq_ref[...]).astype(BF16)
    for g in range(keys_ref.shape[0]):
        sc_ref[g * N_KEYS:(g + 1) * N_KEYS, :] = _dot_nt(keys_ref[g], qp[:, g * LANES:(g + 1) * LANES])


def _merge(o, rec, mg, x, gate1, shift2, scale2, g2, woa, wob, wout, wq, keys):
    B, S, D = x.shape
    tm = TM_PROJ
    ns = S // tm
    NG = keys.shape[0]
    tok = lambda w: pl.BlockSpec((1, tm, w), lambda b, s: (b, s, 0))
    per_b = pl.BlockSpec((1, 1, D), lambda b, s: (b, 0, 0))
    full = lambda a: pl.BlockSpec(a.shape, lambda b, s: (0,) * a.ndim)
    return pl.pallas_call(
        _merge_body,
        grid=(B, ns),
        in_specs=[tok(o.shape[-1]), tok(D), tok(2 * D), tok(D), per_b, per_b, per_b, full(g2),
                  full(woa), full(wob), full(wout), full(wq), full(keys)],
        out_specs=[tok(D), pl.BlockSpec((D, tm), lambda b, s: (0, b * ns + s)),
                   pl.BlockSpec((NG * N_KEYS, tm), lambda b, s: (0, b * ns + s))],
        out_shape=[jax.ShapeDtypeStruct((B, S, D), F32), jax.ShapeDtypeStruct((D, B * S), PEER_MM_DTYPE),
                   jax.ShapeDtypeStruct((NG * N_KEYS, B * S), F32)],
        compiler_params=_cparams(("parallel", "parallel")),
        name="merge",
    )(o, rec, mg, x, gate1, shift2, scale2, g2, woa, wob, wout, wq, keys)


def _sort16_network():
    def merge(lo, hi, r):
        step = r * 2
        if step < hi - lo:
            yield from merge(lo, hi, step)
            yield from merge(lo + r, hi, step)
            yield from [(i, i + r) for i in range(lo + r, hi - r, step)]
        else:
            yield (lo, lo + r)

    def sort(lo, hi):
        if hi - lo >= 1:
            mid = lo + (hi - lo) // 2
            yield from sort(lo, mid)
            yield from sort(mid + 1, hi)
            yield from merge(lo, hi, 1)

    return tuple(sort(0, PEER_TOPK - 1))


_SORT16 = _sort16_network()


def _compare_exchange(v, i, j):
    v[i], v[j] = jnp.maximum(v[i], v[j]), jnp.minimum(v[i], v[j])


def _bitonic_merge16(v):
    d = PEER_TOPK // 2
    while d:
        for i in range(PEER_TOPK):
            if not i & d:
                _compare_exchange(v, i, i + d)
        d //= 2


def _merge_top16(best, other):
    z = [jnp.maximum(best[i], other[PEER_TOPK - 1 - i]) if PEER_TOPK - 1 - i < len(other) else best[i]
         for i in range(PEER_TOPK)]
    _bitonic_merge16(z)
    return z


def _top16_of_128(v):
    v = list(v)
    for i, j in _SORT16:
        _compare_exchange(v, i, j)
    shift = SUBLANES // 2
    while shift:
        v = _merge_top16(v, [pltpu.roll(x, shift, 0) for x in v])
        shift //= 2
    return v


def _select_body(sc_ref, r2_ref, e2_ref, c1_ref, e1_ref):
    K = PEER_TOPK
    H = PEER_HEADS
    nv = N_KEYS // SUBLANES
    tl = sc_ref.shape[1]
    sub = lax.broadcasted_iota(jnp.int32, (SUBLANES, tl), 0)

    def group(g):
        return [sc_ref[g * N_KEYS + k * SUBLANES:g * N_KEYS + (k + 1) * SUBLANES, :] for k in range(nv)]

    t1 = [_top16_of_128(group(2 * h)) for h in range(H)]
    t2 = [_top16_of_128(group(2 * h + 1)) for h in range(H)]

    def pack(reps):
        out = reps[0]
        for h in range(1, H):
            out = jnp.where(sub == h, reps[h], out)
        return out

    def unpack(x, h):
        return jnp.broadcast_to(x[h:h + 1, :], x.shape)

    t1p = [pack([t1[h][a] for h in range(H)]) for a in range(K)]
    t2p = [pack([t2[h][b] for h in range(H)]) for b in range(K)]
    cand = [[t1p[a] + t2p[b] for b in range(K // (a + 1))] for a in range(K)]
    best = list(cand[0])
    for a in range(1, SUBLANES):
        best = _merge_top16(best, cand[a])
    best = _merge_top16(best, [cand[a][0] for a in range(SUBLANES, K)])
    thr = best[K - 1]
    z = jnp.ones_like(thr)
    for r in range(1, K):
        z = z + jnp.exp(best[r] - best[0])
    inv_z = 1.0 / z
    phi = []
    for b in range(K):
        f = jnp.full_like(thr, float("inf"))
        for a in range(K // (b + 1)):
            f = jnp.where(cand[a][b] >= thr, t1p[a], f)
        phi.append(f)

    for h in range(H):
        phi_h = [unpack(f, h) for f in phi]
        inv_z_h = unpack(inv_z, h)
        c1, e1, r2, e2 = [], [], [], []
        for x in group(2 * h):
            cnt = jnp.zeros_like(x)
            for b in range(K):
                cnt = jnp.where(x >= phi_h[b], float(b + 1), cnt)
            c1.append(cnt)
            e1.append(jnp.exp(x - t1[h][0]))
        for x in group(2 * h + 1):
            rank = jnp.full_like(x, float(K))
            for r in reversed(range(K)):
                rank = jnp.where(x >= t2[h][r], float(r), rank)
            r2.append(rank)
            e2.append(jnp.exp(x - t2[h][0]) * inv_z_h)
        rows = slice(h * N_KEYS, (h + 1) * N_KEYS)
        c1_ref[rows, :] = jnp.concatenate(c1, axis=0)
        e1_ref[rows, :] = jnp.concatenate(e1, axis=0)
        r2_ref[rows, :] = jnp.concatenate(r2, axis=0).astype(BF16)
        e2_ref[rows, :] = jnp.concatenate(e2, axis=0).astype(BF16)


def _select(sc):
    R, T = sc.shape
    tl = TL_SEL
    half = R // 2
    blk = pl.BlockSpec((half, tl), lambda t: (0, t))
    return pl.pallas_call(
        _select_body,
        grid=(T // tl,),
        in_specs=[pl.BlockSpec((R, tl), lambda t: (0, t))],
        out_specs=[blk, blk, blk, blk],
        out_shape=[jax.ShapeDtypeStruct((half, T), BF16), jax.ShapeDtypeStruct((half, T), BF16),
                   jax.ShapeDtypeStruct((half, T), F32), jax.ShapeDtypeStruct((half, T), F32)],
        compiler_params=_cparams(("parallel",)),
        name="select",
    )(sc)


def _peer_body(h2t_ref, u0_ref, un_ref, vt_ref, r2_ref, e2_ref, c1_ref, e1_ref, x1_ref, gate2_ref,
               out_ref, acc_ref, at_ref, w_ref):
    e = pl.program_id(1)
    tm = h2t_ref.shape[1]
    n_i = un_ref.shape[0] // N_KEYS

    @pl.when(e == 0)
    def _():
        acc_ref[...] = jnp.zeros_like(acc_ref)
        at_ref[...] = _dot(u0_ref[...], h2t_ref[...]).astype(BF16)

    assert n_i == SUBLANES
    i0 = pl.multiple_of(e * n_i, SUBLANES)

    def row_bf16(ref, h, ii, ls):
        row = ref[pl.ds(h * N_KEYS + i0, SUBLANES), ls][ii:ii + 1]
        one = jnp.broadcast_to(row, (2 * SUBLANES, LANES)).astype(BF16)
        return jnp.concatenate([one] * (N_KEYS // (2 * SUBLANES)), axis=0)

    def routing_weights(lb):
        ls = slice(lb * LANES, (lb + 1) * LANES)
        for ii in range(n_i):
            rs = slice(ii * N_KEYS, (ii + 1) * N_KEYS)
            coef = None
            for h in range(PEER_HEADS):
                hs = slice(h * N_KEYS, (h + 1) * N_KEYS)
                e2 = e2_ref[hs, ls]
                sel = r2_ref[hs, ls] < row_bf16(c1_ref, h, ii, ls)
                term = jnp.where(sel, e2, jnp.zeros_like(e2)) * row_bf16(e1_ref, h, ii, ls)
                coef = term if coef is None else coef + term
            a = at_ref[rs, ls]
            k0, k1, half = GELU_K0 / PEER_U_SCALE, GELU_K1 / PEER_U_SCALE ** 3, 0.5 / PEER_U_SCALE
            cdf_over_s = half * jnp.tanh(a * (k0 + k1 * (a * a))) + half
            w_ref[rs, ls] = (coef * (a * cdf_over_s)).astype(PEER_MM_DTYPE)

    half = tm // 2
    for ts in (slice(0, half), slice(half, tm)):
        for lb in range(ts.start // LANES, ts.stop // LANES):
            routing_weights(lb)
        at_ref[:, ts] = _dot(un_ref[...], h2t_ref[:, ts]).astype(BF16)
        acc_ref[:, ts] += _dot(vt_ref[...], w_ref[:, ts])

    @pl.when(e == pl.num_programs(1) - 1)
    def _():
        out_ref[...] = x1_ref[...] + gate2_ref[0] * acc_ref[...].T


def _peer(h2t, u, vt, r2, e2, c1, e1, x1, gate2, seq_len):
    D, T = h2t.shape
    NE = u.shape[0]
    tm, te = TM_PEER, TE_PEER
    tiles_per_seq = seq_len // tm
    table = pl.BlockSpec((r2.shape[0], tm), lambda t, e: (0, t))
    return pl.pallas_call(
        _peer_body,
        grid=(T // tm, NE // te),
        in_specs=[pl.BlockSpec((D, tm), lambda t, e: (0, t)),
                  pl.BlockSpec((te, D), lambda t, e: (0, 0)),
                  pl.BlockSpec((te, D), lambda t, e: (jnp.minimum(e + 1, NE // te - 1), 0)),
                  pl.BlockSpec((D, te), lambda t, e: (0, e)),
                  table, table, table, table,
                  pl.BlockSpec((tm, D), lambda t, e: (t, 0)),
                  pl.BlockSpec((1, 1, D), lambda t, e: (t // tiles_per_seq, 0, 0))],
        out_specs=pl.BlockSpec((tm, D), lambda t, e: (t, 0)),
        out_shape=jax.ShapeDtypeStruct((T, D), F32),
        scratch_shapes=[pltpu.VMEM((D, tm), F32), pltpu.VMEM((te, tm), BF16), pltpu.VMEM((te, tm), PEER_MM_DTYPE)],
        compiler_params=_cparams(("parallel", "arbitrary")),
        name="peer",
    )(h2t, u, u, vt, r2, e2, c1, e1, x1, gate2)


def _layer(x, cs_mod, positions, p, l):
    B, S, D = x.shape
    H = MLA_HEADS
    shift1, scale1, gate1, shift2, scale2, gate2 = [cs_mod[:, j].reshape(B, 1, D) for j in range(6)]
    w_in = p["w_in"][l]
    c0, c1, c2 = Q_LORA, Q_LORA + KV_LORA, Q_LORA + KV_LORA + QK_ROPE
    half = QK_ROPE // 2
    gqn = jnp.pad(p["q_norm_g"][l] * (QK_HEAD ** -0.5), (0, LANES - QK_HEAD))
    gkn = jnp.pad(p["k_norm_g"][l], (0, LANES - QK_HEAD))

    def rotary_partner(w, gain):
        wg = w * gain
        t1, t2 = wg[..., QK_NOPE:QK_NOPE + half], wg[..., QK_NOPE + half:QK_HEAD]
        lo = jnp.zeros(w.shape[:-1] + (QK_NOPE,), w.dtype)
        hi = jnp.zeros(w.shape[:-1] + (LANES - QK_HEAD,), w.dtype)
        return jnp.concatenate([lo, -t2, t1, hi], axis=-1)

    w_rope = jnp.pad(w_in[:, c1:c2], ((0, 0), (QK_NOPE, LANES - QK_HEAD)))
    wqkr = jnp.concatenate([w_in[:, :c1], w_rope, rotary_partner(w_rope, gkn)], axis=1).astype(BF16)
    wlru = w_in[:, c2:c2 + 2 * D].astype(BF16)
    wmg = w_in[:, c2 + 2 * D:].astype(BF16)
    wuq3 = jnp.pad(p["w_uq"][l].reshape(Q_LORA, H, QK_HEAD), ((0, 0), (0, 0), (0, LANES - QK_HEAD)))
    wuq = jnp.concatenate([wuq3.reshape(Q_LORA, H * LANES),
                           rotary_partner(wuq3, gqn).reshape(Q_LORA, H * LANES)], axis=1).astype(BF16)
    wukv = p["w_ukv"][l].reshape(KV_LORA, H, QK_NOPE + V_HEAD)
    wuk = jnp.pad(wukv[:, :, :QK_NOPE], ((0, 0), (0, 0), (0, LANES - QK_NOPE))).reshape(KV_LORA, H * LANES).astype(BF16)
    wuv = jnp.pad(wukv[:, :, QK_NOPE:], ((0, 0), (0, 0), (0, LANES - V_HEAD))).reshape(KV_LORA, H * LANES).astype(BF16)
    inv_freq = 1.0 / (ROPE_THETA ** (jnp.arange(0, QK_ROPE, 2, dtype=F32) / QK_ROPE))
    zeros_lo = jnp.zeros((QK_NOPE,), F32)
    zeros_hi = jnp.zeros((LANES - QK_HEAD,), F32)
    freq = jnp.concatenate([zeros_lo, inv_freq, inv_freq, zeros_hi]).reshape(1, LANES)

    q, k, v, lx, lg, mg = _in_proj(
        x, positions.reshape(B, S, 1), shift1, scale1, p["norm1_g"][l].reshape(1, D), wqkr, wlru, wmg,
        p["q_a_norm_g"][l].reshape(1, Q_LORA), wuq, p["kv_a_norm_g"][l].reshape(1, KV_LORA), wuk, wuv,
        gqn.reshape(1, LANES), gkn.reshape(1, LANES), freq)
    o = _attention(q, k, v)

    C = p["conv_b"].shape[-1]
    wg = jnp.concatenate([p["w_rg"][l], p["w_ig"][l]], axis=-1).astype(BF16)
    rec = _lru(lx, lg, p["conv_w"][l], p["conv_b"][l].reshape(1, C), wg, p["b_rg"][l].reshape(1, C),
               p["b_ig"][l].reshape(1, C), p["lru_lambda"][l].reshape(1, C))

    woa = jnp.pad(p["w_o_attn"][l].reshape(H, V_HEAD, D), ((0, 0), (0, LANES - V_HEAD), (0, 0)))
    woa = woa.reshape(H * LANES, D).astype(BF16)
    keys = p["sub_keys"][l].reshape(PEER_HEADS * 2, N_KEYS, -1).astype(BF16)
    x1, h2t, sc = _merge(o, rec, mg, x, gate1, shift2, scale2, p["norm2_g"][l].reshape(1, D), woa,
                        p["w_o_lru"][l].astype(BF16), p["w_out"][l].astype(BF16), p["w_query"][l].astype(BF16), keys)
    r2, e2, c1, e1 = _select(sc)
    out = _peer(h2t, (p["expert_u"][l] * PEER_U_SCALE).astype(PEER_MM_DTYPE), p["expert_v"][l].T.astype(PEER_MM_DTYPE),
                r2, e2, c1, e1, x1.reshape(B * S, D), gate2, S)
    return out.reshape(B, S, D)


def kernel(x, c, positions, w_ada, b_ada, norm1_g, w_in, q_a_norm_g, w_uq, kv_a_norm_g, w_ukv, q_norm_g, k_norm_g, w_o_attn, conv_w, conv_b, w_rg, b_rg, w_ig, b_ig, lru_lambda, w_o_lru, w_out, norm2_g, w_query, sub_keys, expert_u, expert_v):
    p = dict(w_in=w_in, norm1_g=norm1_g, q_a_norm_g=q_a_norm_g, w_uq=w_uq, kv_a_norm_g=kv_a_norm_g, w_ukv=w_ukv,
             q_norm_g=q_norm_g, k_norm_g=k_norm_g, w_o_attn=w_o_attn, conv_w=conv_w, conv_b=conv_b, w_rg=w_rg,
             b_rg=b_rg, w_ig=w_ig, b_ig=b_ig, lru_lambda=lru_lambda, w_o_lru=w_o_lru, w_out=w_out,
             norm2_g=norm2_g, w_query=w_query, sub_keys=sub_keys, expert_u=expert_u, expert_v=expert_v)
    B, D = c.shape
    for l in range(w_ada.shape[0]):
        mod = _mod(c, w_ada[l], b_ada[l]).reshape(B, 6, D)
        x = _layer(x, mod, positions, p, l)
    return x
```

```python
import jax
import jax.numpy as jnp
from jax import lax
from jax.experimental import pallas as pl
from jax.experimental.pallas import tpu as pltpu

F32 = jnp.float32
BF16 = jnp.bfloat16

EPS = 1e-6
LANES = 128
SUBLANES = 8
VMEM_LIMIT = 56 * 1024 * 1024

CHUNK = 64
MLA_HEADS = 8
QK_NOPE = 64
QK_ROPE = 32
QK_HEAD = QK_NOPE + QK_ROPE
V_HEAD = 64
Q_LORA = 256
KV_LORA = 128
ROPE_THETA = 10000.0
LRU_BLOCKS = 8
CONV_WIDTH = 4
LRU_C = 8.0
PEER_HEADS = 8
N_KEYS = 128
PEER_TOPK = 16
GELU_K0 = 0.7978845608028654
GELU_K1 = 0.044715 * GELU_K0

TM_PROJ = 256
TQ = 256
ATT_HEADS_PER_STEP = 4
TS_LRU = 256
TL_SEL = 128
TM_PEER = 512
TE_PEER = 1024
PEER_MM_DTYPE = jnp.float8_e4m3fn
PEER_U_SCALE = 32.0


def _cparams(sem):
    return pltpu.CompilerParams(dimension_semantics=sem, vmem_limit_bytes=VMEM_LIMIT)


def _dot(a, b):
    return jnp.dot(a, b, preferred_element_type=F32)


def _dot_nt(a, b):
    return lax.dot_general(a, b, (((1,), (1,)), ((), ())), preferred_element_type=F32)


def _mod_body(c_ref, w_ref, b_ref, o_ref):
    c = c_ref[...]
    cs = c * jax.nn.sigmoid(c)
    o_ref[...] = _dot(cs.astype(BF16), w_ref[...].astype(BF16)) + b_ref[...]


def _mod(c, w_ada, b_ada):
    B, D = c.shape
    N = w_ada.shape[1]
    tn = 1024
    return pl.pallas_call(
        _mod_body,
        grid=(N // tn,),
        in_specs=[pl.BlockSpec((B, D), lambda n: (0, 0)),
                  pl.BlockSpec((D, tn), lambda n: (0, n)),
                  pl.BlockSpec((1, tn), lambda n: (0, n))],
        out_specs=pl.BlockSpec((B, tn), lambda n: (0, n)),
        out_shape=jax.ShapeDtypeStruct((B, N), F32),
        compiler_params=_cparams(("parallel",)),
        name="mod",
    )(c, w_ada, b_ada.reshape(1, N))


def _in_proj_body(x_ref, pos_ref, shift_ref, scale_ref, g1_ref, wqkr_ref, wlru_ref, wmg_ref,
                  gqa_ref, wuq_ref, gkva_ref, wuk_ref, wuv_ref, gqn_ref, gkn_ref, freq_ref,
                  q_ref, k_ref, v_ref, lx_ref, lg_ref, mg_ref):
    HP = MLA_HEADS * LANES
    x = x_ref[0]
    D = x.shape[-1]
    y = x * lax.rsqrt(jnp.mean(x * x, axis=-1, keepdims=True) + EPS) * g1_ref[...]
    h = y * (1.0 + scale_ref[0]) + shift_ref[0]
    hb = h.astype(BF16)

    lru = _dot(hb, wlru_ref[...])
    lx_ref[0] = lru[:, :D]
    lg_ref[0] = lru[:, D:]
    mg_ref[0] = _dot(hb, wmg_ref[...])

    qkr = _dot(hb, wqkr_ref[...])
    cq = qkr[:, :Q_LORA]
    ckv = qkr[:, Q_LORA:Q_LORA + KV_LORA]
    kr = qkr[:, Q_LORA + KV_LORA:Q_LORA + KV_LORA + LANES]
    kr_partner = qkr[:, Q_LORA + KV_LORA + LANES:]
    qn = cq * lax.rsqrt(jnp.mean(cq * cq, axis=-1, keepdims=True) + EPS) * gqa_ref[...]
    kvn = ckv * lax.rsqrt(jnp.mean(ckv * ckv, axis=-1, keepdims=True) + EPS) * gkva_ref[...]
    kvb = kvn.astype(BF16)
    q_both = _dot(qn.astype(BF16), wuq_ref[...])
    k_all = _dot(kvb, wuk_ref[...])
    v_ref[0] = _dot(kvb, wuv_ref[...]).astype(BF16)

    ang = pos_ref[0].astype(F32) * freq_ref[...]
    cos_l = jnp.cos(ang)
    sin_l = jnp.sin(ang)
    q_cos = cos_l * gqn_ref[...]
    k_cos = cos_l * gkn_ref[...]
    k_sin = kr_partner * sin_l
    inv_n = 1.0 / QK_HEAD
    for hh in range(MLA_HEADS):
        sl = slice(hh * LANES, (hh + 1) * LANES)
        qh = q_both[:, sl]
        rq = lax.rsqrt(jnp.sum(qh * qh, axis=-1, keepdims=True) * inv_n + EPS)
        q_ref[0, :, sl] = ((qh * q_cos + q_both[:, HP + hh * LANES:HP + (hh + 1) * LANES] * sin_l) * rq).astype(BF16)
        kh = k_all[:, sl] + kr
        rk = lax.rsqrt(jnp.sum(kh * kh, axis=-1, keepdims=True) * inv_n + EPS)
        k_ref[0, :, sl] = ((kh * k_cos + k_sin) * rk).astype(BF16)


def _in_proj(x, pos3, shift1, scale1, g1, wqkr, wlru, wmg, gqa, wuq, gkva, wuk, wuv, gqn, gkn, freq):
    B, S, D = x.shape
    tm = TM_PROJ
    HP = MLA_HEADS * LANES
    tok = lambda w: pl.BlockSpec((1, tm, w), lambda b, s: (b, s, 0))
    per_b = pl.BlockSpec((1, 1, D), lambda b, s: (b, 0, 0))
    full = lambda a: pl.BlockSpec(a.shape, lambda b, s: (0,) * a.ndim)
    return pl.pallas_call(
        _in_proj_body,
        grid=(B, S // tm),
        in_specs=[tok(D), tok(1), per_b, per_b, full(g1), full(wqkr), full(wlru), full(wmg),
                  full(gqa), full(wuq), full(gkva), full(wuk), full(wuv), full(gqn), full(gkn),
                  full(freq)],
        out_specs=[tok(HP), tok(HP), tok(HP), tok(D), tok(D), tok(2 * D)],
        out_shape=[jax.ShapeDtypeStruct((B, S, HP), BF16)] * 3
        + [jax.ShapeDtypeStruct((B, S, D), F32)] * 2 + [jax.ShapeDtypeStruct((B, S, 2 * D), F32)],
        compiler_params=_cparams(("parallel", "parallel")),
        name="in_proj",
    )(x, pos3, shift1, scale1, g1, wqkr, wlru, wmg, gqa, wuq, gkva, wuk, wuv, gqn, gkn, freq)


def _attn_body(q_ref, k_ref, v_ref, o_ref):
    qi = pl.program_id(2)
    neg = float(jnp.finfo(jnp.float32).min)
    shift = CHUNK.bit_length() - 1
    qc = jnp.right_shift(lax.broadcasted_iota(jnp.int32, (TQ, TQ), 0), shift)
    kc = jnp.right_shift(lax.broadcasted_iota(jnp.int32, (TQ, TQ), 1), shift)
    allowed = kc <= qc

    for nk in range(1, k_ref.shape[1] // TQ + 1):
        @pl.when(qi == nk - 1)
        def _(nk=nk):
            n_keys = nk * TQ
            for hh in range(ATT_HEADS_PER_STEP):
                hl = slice(hh * LANES, (hh + 1) * LANES)
                q = q_ref[0, :, hl]
                s = _dot_nt(q, k_ref[0, :n_keys, hl])
                diag = jnp.where(allowed, s[:, n_keys - TQ:], neg)
                s = diag if nk == 1 else jnp.concatenate([s[:, :n_keys - TQ], diag], axis=1)
                p = jnp.exp(s - jnp.max(s, axis=-1, keepdims=True))
                l = jnp.sum(p, axis=-1, keepdims=True)
                o_ref[0, :, hl] = (_dot(p.astype(BF16), v_ref[0, :n_keys, hl]) / l).astype(BF16)


def _attention(q, k, v):
    B, S, HP = q.shape
    hw = ATT_HEADS_PER_STEP * LANES
    H = HP // hw
    return pl.pallas_call(
        _attn_body,
        grid=(B, H, S // TQ),
        in_specs=[pl.BlockSpec((1, TQ, hw), lambda b, h, i: (b, i, h)),
                  pl.BlockSpec((1, S, hw), lambda b, h, i: (b, 0, h)),
                  pl.BlockSpec((1, S, hw), lambda b, h, i: (b, 0, h))],
        out_specs=pl.BlockSpec((1, TQ, hw), lambda b, h, i: (b, i, h)),
        out_shape=jax.ShapeDtypeStruct((B, S, HP), BF16),
        compiler_params=_cparams(("parallel", "parallel", "arbitrary")),
        name="attn",
    )(q, k, v)


def _lru_body(lx_ref, lg_ref, cw_ref, cb_ref, wg_ref, brg_ref, big_ref, lam_ref, rec_ref, prev_ref, h_ref):
    si = pl.program_id(1)

    @pl.when(si == 0)
    def _():
        prev_ref[...] = jnp.zeros_like(prev_ref)
        h_ref[...] = jnp.zeros_like(h_ref)

    x = lx_ref[0]
    ts, C = x.shape
    prev8 = prev_ref[...]
    row8 = lax.broadcasted_iota(jnp.int32, prev8.shape, 0)
    xc = x * cw_ref[CONV_WIDTH - 1:CONV_WIDTH, :] + cb_ref[...]
    for d in range(1, CONV_WIDTH):
        xr = pltpu.roll(x, d, 0)
        first = jnp.where(row8 < d, pltpu.roll(prev8, d, 0), xr[:SUBLANES])
        xs = jnp.concatenate([first, xr[SUBLANES:]], axis=0)
        xc = xc + xs * cw_ref[CONV_WIDTH - 1 - d:CONV_WIDTH - d, :]
    prev_ref[...] = x[ts - SUBLANES:]

    xcb = xc.astype(BF16)
    rowmod = jnp.bitwise_and(lax.broadcasted_iota(jnp.int32, (ts, LANES), 0), SUBLANES - 1)
    for blk in range(LRU_BLOCKS):
        sl = slice(blk * LANES, (blk + 1) * LANES)
        xb = xc[:, sl]
        g = _dot(xcb[:, sl], wg_ref[blk])
        r = jax.nn.sigmoid(g[:, :LANES] + brg_ref[:, sl])
        i = jax.nn.sigmoid(g[:, LANES:] + big_ref[:, sl])
        nl = -lam_ref[:, sl]
        softplus = jnp.maximum(nl, 0.0) + jnp.log(1.0 + jnp.exp(-jnp.abs(nl)))
        log_a = (-LRU_C * r) * softplus
        a = jnp.exp(log_a)
        b = jnp.sqrt(1.0 - a * a) * (i * xb)
        for d in (1, 2, 4):
            valid = rowmod >= d
            b = jnp.where(valid, a * pltpu.roll(b, d, 0) + b, b)
            a = jnp.where(valid, a * pltpu.roll(a, d, 0), a)
        carry = h_ref[:, sl]
        hs = []
        for gi in range(ts // SUBLANES):
            rs = slice(gi * SUBLANES, (gi + 1) * SUBLANES)
            hg = a[rs] * carry + b[rs]
            carry = hg[SUBLANES - 1:SUBLANES]
            hs.append(hg)
        h_ref[:, sl] = carry
        hfull = jnp.concatenate(hs, axis=0)
        rec_ref[0, :, sl] = (hfull * jax.nn.gelu(lg_ref[0, :, sl])).astype(BF16)


def _lru(lx, lg, cw, cb, wg, brg, big, lam):
    B, S, C = lx.shape
    ts = TS_LRU
    tok = pl.BlockSpec((1, ts, C), lambda b, s: (b, s, 0))
    full = lambda a: pl.BlockSpec(a.shape, lambda b, s: (0,) * a.ndim)
    return pl.pallas_call(
        _lru_body,
        grid=(B, S // ts),
        in_specs=[tok, tok, full(cw), full(cb), full(wg), full(brg), full(big), full(lam)],
        out_specs=tok,
        out_shape=jax.ShapeDtypeStruct((B, S, C), BF16),
        scratch_shapes=[pltpu.VMEM((SUBLANES, C), F32), pltpu.VMEM((1, C), F32)],
        compiler_params=_cparams(("parallel", "arbitrary")),
        name="lru",
    )(lx, lg, cw, cb, wg, brg, big, lam)


def _merge_body(o_ref, rec_ref, mg_ref, x_ref, gate1_ref, shift2_ref, scale2_ref, g2_ref,
                woa_ref, wob_ref, wout_ref, wq_ref, keys_ref, x1_ref, h2_ref, sc_ref):
    D = x_ref.shape[-1]
    ya = _dot(o_ref[0], woa_ref[...])
    yb = _dot(rec_ref[0], wob_ref[...])
    mg = mg_ref[0]
    y = jax.nn.sigmoid(mg[:, :D]) * ya + jax.nn.sigmoid(mg[:, D:]) * yb
    x1 = x_ref[0] + gate1_ref[0] * _dot(y.astype(BF16), wout_ref[...])
    x1_ref[0] = x1
    h2 = x1 * lax.rsqrt(jnp.mean(x1 * x1, axis=-1, keepdims=True) + EPS) * g2_ref[...]
    h2 = h2 * (1.0 + scale2_ref[0]) + shift2_ref[0]
    h2b = h2.astype(BF16)
    h2_ref[...] = h2.T.astype(PEER_MM_DTYPE)
    qp = _dot(h2b, wq_ref[...]).astype(BF16)
    for g in range(keys_ref.shape[0]):
        sc_ref[g * N_KEYS:(g + 1) * N_KEYS, :] = _dot_nt(keys_ref[g], qp[:, g * LANES:(g + 1) * LANES])


def _merge(o, rec, mg, x, gate1, shift2, scale2, g2, woa, wob, wout, wq, keys):
    B, S, D = x.shape
    tm = TM_PROJ
    ns = S // tm
    NG = keys.shape[0]
    tok = lambda w: pl.BlockSpec((1, tm, w), lambda b, s: (b, s, 0))
    per_b = pl.BlockSpec((1, 1, D), lambda b, s: (b, 0, 0))
    full = lambda a: pl.BlockSpec(a.shape, lambda b, s: (0,) * a.ndim)
    return pl.pallas_call(
        _merge_body,
        grid=(B, ns),
        in_specs=[tok(o.shape[-1]), tok(D), tok(2 * D), tok(D), per_b, per_b, per_b, full(g2),
                  full(woa), full(wob), full(wout), full(wq), full(keys)],
        out_specs=[tok(D), pl.BlockSpec((D, tm), lambda b, s: (0, b * ns + s)),
                   pl.BlockSpec((NG * N_KEYS, tm), lambda b, s: (0, b * ns + s))],
        out_shape=[jax.ShapeDtypeStruct((B, S, D), F32), jax.ShapeDtypeStruct((D, B * S), PEER_MM_DTYPE),
                   jax.ShapeDtypeStruct((NG * N_KEYS, B * S), F32)],
        compiler_params=_cparams(("parallel", "parallel")),
        name="merge",
    )(o, rec, mg, x, gate1, shift2, scale2, g2, woa, wob, wout, wq, keys)


def _sort16_network():
    def merge(lo, hi, r):
        step = r * 2
        if step < hi - lo:
            yield from merge(lo, hi, step)
            yield from merge(lo + r, hi, step)
            yield from [(i, i + r) for i in range(lo + r, hi - r, step)]
        else:
            yield (lo, lo + r)

    def sort(lo, hi):
        if hi - lo >= 1:
            mid = lo + (hi - lo) // 2
            yield from sort(lo, mid)
            yield from sort(mid + 1, hi)
            yield from merge(lo, hi, 1)

    return tuple(sort(0, PEER_TOPK - 1))


_SORT16 = _sort16_network()


def _compare_exchange(v, i, j):
    v[i], v[j] = jnp.maximum(v[i], v[j]), jnp.minimum(v[i], v[j])


def _bitonic_merge16(v):
    d = PEER_TOPK // 2
    while d:
        for i in range(PEER_TOPK):
            if not i & d:
                _compare_exchange(v, i, i + d)
        d //= 2


def _merge_top16(best, other):
    z = [jnp.maximum(best[i], other[PEER_TOPK - 1 - i]) if PEER_TOPK - 1 - i < len(other) else best[i]
         for i in range(PEER_TOPK)]
    _bitonic_merge16(z)
    return z


def _top16_of_128(v):
    v = list(v)
    for i, j in _SORT16:
        _compare_exchange(v, i, j)
    shift = SUBLANES // 2
    while shift:
        v = _merge_top16(v, [pltpu.roll(x, shift, 0) for x in v])
        shift //= 2
    return v


def _select_body(sc_ref, r2_ref, e2_ref, c1_ref, e1_ref):
    K = PEER_TOPK
    H = PEER_HEADS
    nv = N_KEYS // SUBLANES
    tl = sc_ref.shape[1]
    sub = lax.broadcasted_iota(jnp.int32, (SUBLANES, tl), 0)

    def group(g):
        return [sc_ref[g * N_KEYS + k * SUBLANES:g * N_KEYS + (k + 1) * SUBLANES, :] for k in range(nv)]

    t1 = [_top16_of_128(group(2 * h)) for h in range(H)]
    t2 = [_top16_of_128(group(2 * h + 1)) for h in range(H)]

    def pack(reps):
        out = reps[0]
        for h in range(1, H):
            out = jnp.where(sub == h, reps[h], out)
        return out

    def unpack(x, h):
        return jnp.broadcast_to(x[h:h + 1, :], x.shape)

    t1p = [pack([t1[h][a] for h in range(H)]) for a in range(K)]
    t2p = [pack([t2[h][b] for h in range(H)]) for b in range(K)]
    cand = [[t1p[a] + t2p[b] for b in range(K // (a + 1))] for a in range(K)]
    best = list(cand[0])
    for a in range(1, SUBLANES):
        best = _merge_top16(best, cand[a])
    best = _merge_top16(best, [cand[a][0] for a in range(SUBLANES, K)])
    thr = best[K - 1]
    z = jnp.ones_like(thr)
    for r in range(1, K):
        z = z + jnp.exp(best[r] - best[0])
    inv_z = 1.0 / z
    phi = []
    for b in range(SUBLANES):
        f = jnp.full_like(thr, float("inf"))
        for a in range(K // (b + 1)):
            f = jnp.where(cand[a][b] >= thr, t1p[a], f)
        phi.append(f)
    top_count = jnp.zeros_like(thr)
    for b in range(K):
        top_count = jnp.where(cand[0][b] >= thr, float(b + 1), top_count)

    for h in range(H):
        phi_h = [unpack(f, h) for f in phi]
        inv_z_h = unpack(inv_z, h)
        top_count_h = unpack(top_count, h)
        c1, e1, r2, e2 = [], [], [], []
        for x in group(2 * h):
            cnt = jnp.zeros_like(x)
            for b in range(SUBLANES):
                cnt = jnp.where(x >= phi_h[b], float(b + 1), cnt)
            c1.append(jnp.where(x >= t1[h][0], top_count_h, cnt))
            e1.append(jnp.exp(x - t1[h][0]))
        for x in group(2 * h + 1):
            rank = jnp.full_like(x, float(K))
            for r in reversed(range(K)):
                rank = jnp.where(x >= t2[h][r], float(r), rank)
            r2.append(rank)
            e2.append(jnp.exp(x - t2[h][0]) * inv_z_h)
        rows = slice(h * N_KEYS, (h + 1) * N_KEYS)
        c1_ref[rows, :] = jnp.concatenate(c1, axis=0)
        e1_ref[rows, :] = jnp.concatenate(e1, axis=0)
        r2_ref[rows, :] = jnp.concatenate(r2, axis=0).astype(BF16)
        e2_ref[rows, :] = jnp.concatenate(e2, axis=0).astype(BF16)


def _select(sc):
    R, T = sc.shape
    tl = TL_SEL
    half = R // 2
    blk = pl.BlockSpec((half, tl), lambda t: (0, t))
    return pl.pallas_call(
        _select_body,
        grid=(T // tl,),
        in_specs=[pl.BlockSpec((R, tl), lambda t: (0, t))],
        out_specs=[blk, blk, blk, blk],
        out_shape=[jax.ShapeDtypeStruct((half, T), BF16), jax.ShapeDtypeStruct((half, T), BF16),
                   jax.ShapeDtypeStruct((half, T), F32), jax.ShapeDtypeStruct((half, T), F32)],
        compiler_params=_cparams(("parallel",)),
        name="select",
    )(sc)


def _peer_body(h2t_ref, u0_ref, un_ref, vt_ref, r2_ref, e2_ref, c1_ref, e1_ref, x1_ref, gate2_ref,
               out_ref, acc_ref, at_ref, w_ref):
    e = pl.program_id(1)
    tm = h2t_ref.shape[1]
    n_i = un_ref.shape[0] // N_KEYS

    @pl.when(e == 0)
    def _():
        acc_ref[...] = jnp.zeros_like(acc_ref)
        at_ref[...] = _dot(u0_ref[...], h2t_ref[...]).astype(BF16)

    assert n_i == SUBLANES
    i0 = pl.multiple_of(e * n_i, SUBLANES)

    def row_bf16(ref, h, ii, ls):
        row = ref[pl.ds(h * N_KEYS + i0, SUBLANES), ls][ii:ii + 1]
        one = jnp.broadcast_to(row, (2 * SUBLANES, LANES)).astype(BF16)
        return jnp.concatenate([one] * (N_KEYS // (2 * SUBLANES)), axis=0)

    def routing_weights(lb):
        ls = slice(lb * LANES, (lb + 1) * LANES)
        for ii in range(n_i):
            rs = slice(ii * N_KEYS, (ii + 1) * N_KEYS)
            coef = None
            for h in range(PEER_HEADS):
                hs = slice(h * N_KEYS, (h + 1) * N_KEYS)
                e2 = e2_ref[hs, ls]
                sel = r2_ref[hs, ls] < row_bf16(c1_ref, h, ii, ls)
                term = jnp.where(sel, e2, jnp.zeros_like(e2)) * row_bf16(e1_ref, h, ii, ls)
                coef = term if coef is None else coef + term
            a = at_ref[rs, ls]
            k0, k1, half = GELU_K0 / PEER_U_SCALE, GELU_K1 / PEER_U_SCALE ** 3, 0.5 / PEER_U_SCALE
            cdf_over_s = half * jnp.tanh(a * (k0 + k1 * (a * a))) + half
            w_ref[rs, ls] = (coef * (a * cdf_over_s)).astype(PEER_MM_DTYPE)

    half = tm // 2
    for ts in (slice(0, half), slice(half, tm)):
        for lb in range(ts.start // LANES, ts.stop // LANES):
            routing_weights(lb)
        at_ref[:, ts] = _dot(un_ref[...], h2t_ref[:, ts]).astype(BF16)
        acc_ref[:, ts] += _dot(vt_ref[...], w_ref[:, ts])

    @pl.when(e == pl.num_programs(1) - 1)
    def _():
        out_ref[...] = x1_ref[...] + gate2_ref[0] * acc_ref[...].T


def _peer(h2t, u, vt, r2, e2, c1, e1, x1, gate2, seq_len):
    D, T = h2t.shape
    NE = u.shape[0]
    tm, te = TM_PEER, TE_PEER
    tiles_per_seq = seq_len // tm
    table = pl.BlockSpec((r2.shape[0], tm), lambda t, e: (0, t))
    return pl.pallas_call(
        _peer_body,
        grid=(T // tm, NE // te),
        in_specs=[pl.BlockSpec((D, tm), lambda t, e: (0, t)),
                  pl.BlockSpec((te, D), lambda t, e: (0, 0)),
                  pl.BlockSpec((te, D), lambda t, e: (jnp.minimum(e + 1, NE // te - 1), 0)),
                  pl.BlockSpec((D, te), lambda t, e: (0, e)),
                  table, table, table, table,
                  pl.BlockSpec((tm, D), lambda t, e: (t, 0)),
                  pl.BlockSpec((1, 1, D), lambda t, e: (t // tiles_per_seq, 0, 0))],
        out_specs=pl.BlockSpec((tm, D), lambda t, e: (t, 0)),
        out_shape=jax.ShapeDtypeStruct((T, D), F32),
        scratch_shapes=[pltpu.VMEM((D, tm), F32), pltpu.VMEM((te, tm), BF16), pltpu.VMEM((te, tm), PEER_MM_DTYPE)],
        compiler_params=_cparams(("parallel", "arbitrary")),
        name="peer",
    )(h2t, u, u, vt, r2, e2, c1, e1, x1, gate2)


def _layer(x, cs_mod, positions, p, l):
    B, S, D = x.shape
    H = MLA_HEADS
    shift1, scale1, gate1, shift2, scale2, gate2 = [cs_mod[:, j].reshape(B, 1, D) for j in range(6)]
    w_in = p["w_in"][l]
    c0, c1, c2 = Q_LORA, Q_LORA + KV_LORA, Q_LORA + KV_LORA + QK_ROPE
    half = QK_ROPE // 2
    gqn = jnp.pad(p["q_norm_g"][l] * (QK_HEAD ** -0.5), (0, LANES - QK_HEAD))
    gkn = jnp.pad(p["k_norm_g"][l], (0, LANES - QK_HEAD))

    def rotary_partner(w, gain):
        wg = w * gain
        t1, t2 = wg[..., QK_NOPE:QK_NOPE + half], wg[..., QK_NOPE + half:QK_HEAD]
        lo = jnp.zeros(w.shape[:-1] + (QK_NOPE,), w.dtype)
        hi = jnp.zeros(w.shape[:-1] + (LANES - QK_HEAD,), w.dtype)
        return jnp.concatenate([lo, -t2, t1, hi], axis=-1)

    w_rope = jnp.pad(w_in[:, c1:c2], ((0, 0), (QK_NOPE, LANES - QK_HEAD)))
    wqkr = jnp.concatenate([w_in[:, :c1], w_rope, rotary_partner(w_rope, gkn)], axis=1).astype(BF16)
    wlru = w_in[:, c2:c2 + 2 * D].astype(BF16)
    wmg = w_in[:, c2 + 2 * D:].astype(BF16)
    wuq3 = jnp.pad(p["w_uq"][l].reshape(Q_LORA, H, QK_HEAD), ((0, 0), (0, 0), (0, LANES - QK_HEAD)))
    wuq = jnp.concatenate([wuq3.reshape(Q_LORA, H * LANES),
                           rotary_partner(wuq3, gqn).reshape(Q_LORA, H * LANES)], axis=1).astype(BF16)
    wukv = p["w_ukv"][l].reshape(KV_LORA, H, QK_NOPE + V_HEAD)
    wuk = jnp.pad(wukv[:, :, :QK_NOPE], ((0, 0), (0, 0), (0, LANES - QK_NOPE))).reshape(KV_LORA, H * LANES).astype(BF16)
    wuv = jnp.pad(wukv[:, :, QK_NOPE:], ((0, 0), (0, 0), (0, LANES - V_HEAD))).reshape(KV_LORA, H * LANES).astype(BF16)
    inv_freq = 1.0 / (ROPE_THETA ** (jnp.arange(0, QK_ROPE, 2, dtype=F32) / QK_ROPE))
    zeros_lo = jnp.zeros((QK_NOPE,), F32)
    zeros_hi = jnp.zeros((LANES - QK_HEAD,), F32)
    freq = jnp.concatenate([zeros_lo, inv_freq, inv_freq, zeros_hi]).reshape(1, LANES)

    q, k, v, lx, lg, mg = _in_proj(
        x, positions.reshape(B, S, 1), shift1, scale1, p["norm1_g"][l].reshape(1, D), wqkr, wlru, wmg,
        p["q_a_norm_g"][l].reshape(1, Q_LORA), wuq, p["kv_a_norm_g"][l].reshape(1, KV_LORA), wuk, wuv,
        gqn.reshape(1, LANES), gkn.reshape(1, LANES), freq)
    o = _attention(q, k, v)

    C = p["conv_b"].shape[-1]
    wg = jnp.concatenate([p["w_rg"][l], p["w_ig"][l]], axis=-1).astype(BF16)
    rec = _lru(lx, lg, p["conv_w"][l], p["conv_b"][l].reshape(1, C), wg, p["b_rg"][l].reshape(1, C),
               p["b_ig"][l].reshape(1, C), p["lru_lambda"][l].reshape(1, C))

    woa = jnp.pad(p["w_o_attn"][l].reshape(H, V_HEAD, D), ((0, 0), (0, LANES - V_HEAD), (0, 0)))
    woa = woa.reshape(H * LANES, D).astype(BF16)
    keys = p["sub_keys"][l].reshape(PEER_HEADS * 2, N_KEYS, -1).astype(BF16)
    x1, h2t, sc = _merge(o, rec, mg, x, gate1, shift2, scale2, p["norm2_g"][l].reshape(1, D), woa,
                        p["w_o_lru"][l].astype(BF16), p["w_out"][l].astype(BF16), p["w_query"][l].astype(BF16), keys)
    r2, e2, c1, e1 = _select(sc)
    out = _peer(h2t, (p["expert_u"][l] * PEER_U_SCALE).astype(PEER_MM_DTYPE), p["expert_v"][l].T.astype(PEER_MM_DTYPE),
                r2, e2, c1, e1, x1.reshape(B * S, D), gate2, S)
    return out.reshape(B, S, D)


def kernel(x, c, positions, w_ada, b_ada, norm1_g, w_in, q_a_norm_g, w_uq, kv_a_norm_g, w_ukv, q_norm_g, k_norm_g, w_o_attn, conv_w, conv_b, w_rg, b_rg, w_ig, b_ig, lru_lambda, w_o_lru, w_out, norm2_g, w_query, sub_keys, expert_u, expert_v):
    p = dict(w_in=w_in, norm1_g=norm1_g, q_a_norm_g=q_a_norm_g, w_uq=w_uq, kv_a_norm_g=kv_a_norm_g, w_ukv=w_ukv,
             q_norm_g=q_norm_g, k_norm_g=k_norm_g, w_o_attn=w_o_attn, conv_w=conv_w, conv_b=conv_b, w_rg=w_rg,
             b_rg=b_rg, w_ig=w_ig, b_ig=b_ig, lru_lambda=lru_lambda, w_o_lru=w_o_lru, w_out=w_out,
             norm2_g=norm2_g, w_query=w_query, sub_keys=sub_keys, expert_u=expert_u, expert_v=expert_v)
    B, D = c.shape
    for l in range(w_ada.shape[0]):
        mod = _mod(c, w_ada[l], b_ada[l]).reshape(B, 6, D)
        x = _layer(x, mod, positions, p, l)
    return x
```

```python
import jax
import jax.numpy as jnp
from jax import lax
from jax.experimental import pallas as pl
from jax.experimental.pallas import tpu as pltpu

F32 = jnp.float32
BF16 = jnp.bfloat16

EPS = 1e-6
LANES = 128
SUBLANES = 8
VMEM_LIMIT = 56 * 1024 * 1024

CHUNK = 64
MLA_HEADS = 8
QK_NOPE = 64
QK_ROPE = 32
QK_HEAD = QK_NOPE + QK_ROPE
V_HEAD = 64
Q_LORA = 256
KV_LORA = 128
ROPE_THETA = 10000.0
LRU_BLOCKS = 8
CONV_WIDTH = 4
LRU_C = 8.0
PEER_HEADS = 8
N_KEYS = 128
PEER_TOPK = 16
GELU_K0 = 0.7978845608028654
GELU_K1 = 0.044715 * GELU_K0

TM_PROJ = 256
TQ = 256
ATT_HEADS_PER_STEP = 8
TS_LRU = 256
TL_SEL = 256
TM_PEER = 512
TE_PEER = 1024
PEER_MM_DTYPE = jnp.float8_e4m3fn
PEER_U_SCALE = 32.0


def _cparams(sem):
    return pltpu.CompilerParams(dimension_semantics=sem, vmem_limit_bytes=VMEM_LIMIT)


def _dot(a, b):
    return jnp.dot(a, b, preferred_element_type=F32)


def _dot_nt(a, b):
    return lax.dot_general(a, b, (((1,), (1,)), ((), ())), preferred_element_type=F32)


def _mod_body(c_ref, w_ref, b_ref, o_ref):
    c = c_ref[...]
    cs = c * jax.nn.sigmoid(c)
    o_ref[...] = _dot(cs.astype(BF16), w_ref[...].astype(BF16)) + b_ref[...]


def _mod(c, w_ada, b_ada):
    B, D = c.shape
    N = w_ada.shape[1]
    tn = 1024
    return pl.pallas_call(
        _mod_body,
        grid=(N // tn,),
        in_specs=[pl.BlockSpec((B, D), lambda n: (0, 0)),
                  pl.BlockSpec((D, tn), lambda n: (0, n)),
                  pl.BlockSpec((1, tn), lambda n: (0, n))],
        out_specs=pl.BlockSpec((B, tn), lambda n: (0, n)),
        out_shape=jax.ShapeDtypeStruct((B, N), F32),
        compiler_params=_cparams(("parallel",)),
        name="mod",
    )(c, w_ada, b_ada.reshape(1, N))


def _in_proj_body(x_ref, pos_ref, shift_ref, scale_ref, g1_ref, wqkr_ref, wlru_ref, wmg_ref,
                  gqa_ref, wuq_ref, gkva_ref, wuk_ref, wuv_ref, gqn_ref, gkn_ref, freq_ref,
                  q_ref, k_ref, v_ref, lx_ref, lg_ref, mg_ref):
    HP = MLA_HEADS * LANES
    x = x_ref[0]
    D = x.shape[-1]
    y = x * lax.rsqrt(jnp.mean(x * x, axis=-1, keepdims=True) + EPS) * g1_ref[...]
    h = y * (1.0 + scale_ref[0]) + shift_ref[0]
    hb = h.astype(BF16)

    lru = _dot(hb, wlru_ref[...])
    lx_ref[0] = lru[:, :D]
    lg_ref[0] = lru[:, D:]
    mg_ref[0] = _dot(hb, wmg_ref[...])

    qkr = _dot(hb, wqkr_ref[...])
    cq = qkr[:, :Q_LORA]
    ckv = qkr[:, Q_LORA:Q_LORA + KV_LORA]
    kr = qkr[:, Q_LORA + KV_LORA:Q_LORA + KV_LORA + LANES]
    kr_partner = qkr[:, Q_LORA + KV_LORA + LANES:]
    qn = cq * lax.rsqrt(jnp.mean(cq * cq, axis=-1, keepdims=True) + EPS) * gqa_ref[...]
    kvn = ckv * lax.rsqrt(jnp.mean(ckv * ckv, axis=-1, keepdims=True) + EPS) * gkva_ref[...]
    kvb = kvn.astype(BF16)
    q_both = _dot(qn.astype(BF16), wuq_ref[...])
    k_all = _dot(kvb, wuk_ref[...])
    v_ref[0] = _dot(kvb, wuv_ref[...]).astype(BF16)

    ang = pos_ref[0].astype(F32) * freq_ref[...]
    cos_l = jnp.cos(ang)
    sin_l = jnp.sin(ang)
    q_cos = cos_l * gqn_ref[...]
    k_cos = cos_l * gkn_ref[...]
    k_sin = kr_partner * sin_l
    inv_n = 1.0 / QK_HEAD
    for hh in range(MLA_HEADS):
        sl = slice(hh * LANES, (hh + 1) * LANES)
        qh = q_both[:, sl]
        rq = lax.rsqrt(jnp.sum(qh * qh, axis=-1, keepdims=True) * inv_n + EPS)
        q_ref[0, :, sl] = ((qh * q_cos + q_both[:, HP + hh * LANES:HP + (hh + 1) * LANES] * sin_l) * rq).astype(BF16)
        kh = k_all[:, sl] + kr
        rk = lax.rsqrt(jnp.sum(kh * kh, axis=-1, keepdims=True) * inv_n + EPS)
        k_ref[0, :, sl] = ((kh * k_cos + k_sin) * rk).astype(BF16)


def _in_proj(x, pos3, shift1, scale1, g1, wqkr, wlru, wmg, gqa, wuq, gkva, wuk, wuv, gqn, gkn, freq):
    B, S, D = x.shape
    tm = TM_PROJ
    HP = MLA_HEADS * LANES
    tok = lambda w: pl.BlockSpec((1, tm, w), lambda b, s: (b, s, 0))
    per_b = pl.BlockSpec((1, 1, D), lambda b, s: (b, 0, 0))
    full = lambda a: pl.BlockSpec(a.shape, lambda b, s: (0,) * a.ndim)
    return pl.pallas_call(
        _in_proj_body,
        grid=(B, S // tm),
        in_specs=[tok(D), tok(1), per_b, per_b, full(g1), full(wqkr), full(wlru), full(wmg),
                  full(gqa), full(wuq), full(gkva), full(wuk), full(wuv), full(gqn), full(gkn),
                  full(freq)],
        out_specs=[tok(HP), tok(HP), tok(HP), tok(D), tok(D), tok(2 * D)],
        out_shape=[jax.ShapeDtypeStruct((B, S, HP), BF16)] * 3
        + [jax.ShapeDtypeStruct((B, S, D), F32)] * 2 + [jax.ShapeDtypeStruct((B, S, 2 * D), F32)],
        compiler_params=_cparams(("parallel", "parallel")),
        name="in_proj",
    )(x, pos3, shift1, scale1, g1, wqkr, wlru, wmg, gqa, wuq, gkva, wuk, wuv, gqn, gkn, freq)


def _attn_body(q_ref, k_ref, v_ref, o_ref):
    qi = pl.program_id(2)
    neg = float(jnp.finfo(jnp.float32).min)
    shift = CHUNK.bit_length() - 1
    qc = jnp.right_shift(lax.broadcasted_iota(jnp.int32, (TQ, TQ), 0), shift)
    kc = jnp.right_shift(lax.broadcasted_iota(jnp.int32, (TQ, TQ), 1), shift)
    allowed = kc <= qc

    for nk in range(1, k_ref.shape[1] // TQ + 1):
        @pl.when(qi == nk - 1)
        def _(nk=nk):
            n_keys = nk * TQ
            for hh in range(ATT_HEADS_PER_STEP):
                hl = slice(hh * LANES, (hh + 1) * LANES)
                q = q_ref[0, :, hl]
                s = _dot_nt(q, k_ref[0, :n_keys, hl])
                diag = jnp.where(allowed, s[:, n_keys - TQ:], neg)
                s = diag if nk == 1 else jnp.concatenate([s[:, :n_keys - TQ], diag], axis=1)
                p = jnp.exp(s - jnp.max(s, axis=-1, keepdims=True))
                l = jnp.sum(p, axis=-1, keepdims=True)
                o_ref[0, :, hl] = (_dot(p.astype(BF16), v_ref[0, :n_keys, hl]) / l).astype(BF16)


def _attention(q, k, v):
    B, S, HP = q.shape
    hw = ATT_HEADS_PER_STEP * LANES
    H = HP // hw
    return pl.pallas_call(
        _attn_body,
        grid=(B, H, S // TQ),
        in_specs=[pl.BlockSpec((1, TQ, hw), lambda b, h, i: (b, i, h)),
                  pl.BlockSpec((1, S, hw), lambda b, h, i: (b, 0, h)),
                  pl.BlockSpec((1, S, hw), lambda b, h, i: (b, 0, h))],
        out_specs=pl.BlockSpec((1, TQ, hw), lambda b, h, i: (b, i, h)),
        out_shape=jax.ShapeDtypeStruct((B, S, HP), BF16),
        compiler_params=_cparams(("parallel", "parallel", "arbitrary")),
        name="attn",
    )(q, k, v)


def _lru_body(lx_ref, lg_ref, cw_ref, cb_ref, wg_ref, brg_ref, big_ref, lam_ref, rec_ref, prev_ref, h_ref):
    si = pl.program_id(1)

    @pl.when(si == 0)
    def _():
        prev_ref[...] = jnp.zeros_like(prev_ref)
        h_ref[...] = jnp.zeros_like(h_ref)

    x = lx_ref[0]
    ts, C = x.shape
    prev8 = prev_ref[...]
    row8 = lax.broadcasted_iota(jnp.int32, prev8.shape, 0)
    xc = x * cw_ref[CONV_WIDTH - 1:CONV_WIDTH, :] + cb_ref[...]
    for d in range(1, CONV_WIDTH):
        xr = pltpu.roll(x, d, 0)
        first = jnp.where(row8 < d, pltpu.roll(prev8, d, 0), xr[:SUBLANES])
        xs = jnp.concatenate([first, xr[SUBLANES:]], axis=0)
        xc = xc + xs * cw_ref[CONV_WIDTH - 1 - d:CONV_WIDTH - d, :]
    prev_ref[...] = x[ts - SUBLANES:]

    xcb = xc.astype(BF16)
    rowmod = jnp.bitwise_and(lax.broadcasted_iota(jnp.int32, (ts, LANES), 0), SUBLANES - 1)
    for blk in range(LRU_BLOCKS):
        sl = slice(blk * LANES, (blk + 1) * LANES)
        xb = xc[:, sl]
        g = _dot(xcb[:, sl], wg_ref[blk])
        r = jax.nn.sigmoid(g[:, :LANES] + brg_ref[:, sl])
        i = jax.nn.sigmoid(g[:, LANES:] + big_ref[:, sl])
        nl = -lam_ref[:, sl]
        softplus = jnp.maximum(nl, 0.0) + jnp.log(1.0 + jnp.exp(-jnp.abs(nl)))
        log_a = (-LRU_C * r) * softplus
        a = jnp.exp(log_a)
        b = jnp.sqrt(1.0 - a * a) * (i * xb)
        for d in (1, 2, 4):
            valid = rowmod >= d
            b = jnp.where(valid, a * pltpu.roll(b, d, 0) + b, b)
            a = jnp.where(valid, a * pltpu.roll(a, d, 0), a)
        carry = h_ref[:, sl]
        hs = []
        for gi in range(ts // SUBLANES):
            rs = slice(gi * SUBLANES, (gi + 1) * SUBLANES)
            hg = a[rs] * carry + b[rs]
            carry = hg[SUBLANES - 1:SUBLANES]
            hs.append(hg)
        h_ref[:, sl] = carry
        hfull = jnp.concatenate(hs, axis=0)
        rec_ref[0, :, sl] = (hfull * jax.nn.gelu(lg_ref[0, :, sl])).astype(BF16)


def _lru(lx, lg, cw, cb, wg, brg, big, lam):
    B, S, C = lx.shape
    ts = TS_LRU
    tok = pl.BlockSpec((1, ts, C), lambda b, s: (b, s, 0))
    full = lambda a: pl.BlockSpec(a.shape, lambda b, s: (0,) * a.ndim)
    return pl.pallas_call(
        _lru_body,
        grid=(B, S // ts),
        in_specs=[tok, tok, full(cw), full(cb), full(wg), full(brg), full(big), full(lam)],
        out_specs=tok,
        out_shape=jax.ShapeDtypeStruct((B, S, C), BF16),
        scratch_shapes=[pltpu.VMEM((SUBLANES, C), F32), pltpu.VMEM((1, C), F32)],
        compiler_params=_cparams(("parallel", "arbitrary")),
        name="lru",
    )(lx, lg, cw, cb, wg, brg, big, lam)


def _merge_body(o_ref, rec_ref, mg_ref, x_ref, gate1_ref, shift2_ref, scale2_ref, g2_ref,
                woa_ref, wob_ref, wout_ref, wq_ref, keys_ref, x1_ref, h2_ref, sc_ref):
    D = x_ref.shape[-1]
    ya = _dot(o_ref[0], woa_ref[...])
    yb = _dot(rec_ref[0], wob_ref[...])
    mg = mg_ref[0]
    y = jax.nn.sigmoid(mg[:, :D]) * ya + jax.nn.sigmoid(mg[:, D:]) * yb
    x1 = x_ref[0] + gate1_ref[0] * _dot(y.astype(BF16), wout_ref[...])
    x1_ref[0] = x1
    h2 = x1 * lax.rsqrt(jnp.mean(x1 * x1, axis=-1, keepdims=True) + EPS) * g2_ref[...]
    h2 = h2 * (1.0 + scale2_ref[0]) + shift2_ref[0]
    h2b = h2.astype(BF16)
    h2_ref[...] = h2.T.astype(PEER_MM_DTYPE)
    qp = _dot(h2b, wq_ref[...]).astype(BF16)
    for g in range(keys_ref.shape[0]):
        sc_ref[g * N_KEYS:(g + 1) * N_KEYS, :] = _dot_nt(keys_ref[g], qp[:, g * LANES:(g + 1) * LANES])


def _merge(o, rec, mg, x, gate1, shift2, scale2, g2, woa, wob, wout, wq, keys):
    B, S, D = x.shape
    tm = TM_PROJ
    ns = S // tm
    NG = keys.shape[0]
    tok = lambda w: pl.BlockSpec((1, tm, w), lambda b, s: (b, s, 0))
    per_b = pl.BlockSpec((1, 1, D), lambda b, s: (b, 0, 0))
    full = lambda a: pl.BlockSpec(a.shape, lambda b, s: (0,) * a.ndim)
    return pl.pallas_call(
        _merge_body,
        grid=(B, ns),
        in_specs=[tok(o.shape[-1]), tok(D), tok(2 * D), tok(D), per_b, per_b, per_b, full(g2),
                  full(woa), full(wob), full(wout), full(wq), full(keys)],
        out_specs=[tok(D), pl.BlockSpec((D, tm), lambda b, s: (0, b * ns + s)),
                   pl.BlockSpec((NG * N_KEYS, tm), lambda b, s: (0, b * ns + s))],
        out_shape=[jax.ShapeDtypeStruct((B, S, D), F32), jax.ShapeDtypeStruct((D, B * S), PEER_MM_DTYPE),
                   jax.ShapeDtypeStruct((NG * N_KEYS, B * S), F32)],
        compiler_params=_cparams(("parallel", "parallel")),
        name="merge",
    )(o, rec, mg, x, gate1, shift2, scale2, g2, woa, wob, wout, wq, keys)


def _sort16_network():
    def merge(lo, hi, r):
        step = r * 2
        if step < hi - lo:
            yield from merge(lo, hi, step)
            yield from merge(lo + r, hi, step)
            yield from [(i, i + r) for i in range(lo + r, hi - r, step)]
        else:
            yield (lo, lo + r)

    def sort(lo, hi):
        if hi - lo >= 1:
            mid = lo + (hi - lo) // 2
            yield from sort(lo, mid)
            yield from sort(mid + 1, hi)
            yield from merge(lo, hi, 1)

    return tuple(sort(0, PEER_TOPK - 1))


_SORT16 = _sort16_network()


def _compare_exchange(v, i, j):
    v[i], v[j] = jnp.maximum(v[i], v[j]), jnp.minimum(v[i], v[j])


def _bitonic_merge16(v):
    d = PEER_TOPK // 2
    while d:
        for i in range(PEER_TOPK):
            if not i & d:
                _compare_exchange(v, i, i + d)
        d //= 2


def _merge_top16(best, other):
    z = [jnp.maximum(best[i], other[PEER_TOPK - 1 - i]) if PEER_TOPK - 1 - i < len(other) else best[i]
         for i in range(PEER_TOPK)]
    _bitonic_merge16(z)
    return z


def _top16_of_128(v):
    v = list(v)
    for i, j in _SORT16:
        _compare_exchange(v, i, j)
    shift = SUBLANES // 2
    while shift:
        v = _merge_top16(v, [pltpu.roll(x, shift, 0) for x in v])
        shift //= 2
    return v


def _select_body(sc_ref, r2_ref, e2_ref, c1_ref, e1_ref):
    K = PEER_TOPK
    H = PEER_HEADS
    nv = N_KEYS // SUBLANES
    tl = sc_ref.shape[1]
    sub = lax.broadcasted_iota(jnp.int32, (SUBLANES, tl), 0)

    def group(g):
        return [sc_ref[g * N_KEYS + k * SUBLANES:g * N_KEYS + (k + 1) * SUBLANES, :] for k in range(nv)]

    t1 = [_top16_of_128(group(2 * h)) for h in range(H)]
    t2 = [_top16_of_128(group(2 * h + 1)) for h in range(H)]

    def pack(reps):
        out = reps[0]
        for h in range(1, H):
            out = jnp.where(sub == h, reps[h], out)
        return out

    def unpack(x, h):
        return jnp.broadcast_to(x[h:h + 1, :], x.shape)

    t1p = [pack([t1[h][a] for h in range(H)]) for a in range(K)]
    t2p = [pack([t2[h][b] for h in range(H)]) for b in range(K)]
    cand = [[t1p[a] + t2p[b] for b in range(K // (a + 1))] for a in range(K)]
    best = list(cand[0])
    for a in range(1, SUBLANES):
        best = _merge_top16(best, cand[a])
    best = _merge_top16(best, [cand[a][0] for a in range(SUBLANES, K)])
    thr = best[K - 1]
    z = jnp.ones_like(thr)
    for r in range(1, K):
        z = z + jnp.exp(best[r] - best[0])
    inv_z = 1.0 / z
    phi = []
    for b in range(SUBLANES):
        f = jnp.full_like(thr, float("inf"))
        for a in range(K // (b + 1)):
            f = jnp.where(cand[a][b] >= thr, t1p[a], f)
        phi.append(f)
    top_count = jnp.zeros_like(thr)
    for b in range(K):
        top_count = jnp.where(cand[0][b] >= thr, float(b + 1), top_count)

    for h in range(H):
        phi_h = [unpack(f, h) for f in phi]
        inv_z_h = unpack(inv_z, h)
        top_count_h = unpack(top_count, h)
        c1, e1, r2, e2 = [], [], [], []
        for x in group(2 * h):
            cnt = jnp.zeros_like(x)
            for b in range(SUBLANES):
                cnt = jnp.where(x >= phi_h[b], float(b + 1), cnt)
            c1.append(jnp.where(x >= t1[h][0], top_count_h, cnt))
            e1.append(jnp.exp(x - t1[h][0]))
        for x in group(2 * h + 1):
            rank = jnp.full_like(x, float(K))
            for r in reversed(range(K)):
                rank = jnp.where(x >= t2[h][r], float(r), rank)
            r2.append(rank)
            e2.append(jnp.exp(x - t2[h][0]) * inv_z_h)
        rows = slice(h * N_KEYS, (h + 1) * N_KEYS)
        c1_ref[rows, :] = jnp.concatenate(c1, axis=0)
        e1_ref[rows, :] = jnp.concatenate(e1, axis=0)
        r2_ref[rows, :] = jnp.concatenate(r2, axis=0).astype(BF16)
        e2_ref[rows, :] = jnp.concatenate(e2, axis=0).astype(BF16)


def _select(sc):
    R, T = sc.shape
    tl = TL_SEL
    half = R // 2
    blk = pl.BlockSpec((half, tl), lambda t: (0, t))
    return pl.pallas_call(
        _select_body,
        grid=(T // tl,),
        in_specs=[pl.BlockSpec((R, tl), lambda t: (0, t))],
        out_specs=[blk, blk, blk, blk],
        out_shape=[jax.ShapeDtypeStruct((half, T), BF16), jax.ShapeDtypeStruct((half, T), BF16),
                   jax.ShapeDtypeStruct((half, T), F32), jax.ShapeDtypeStruct((half, T), F32)],
        compiler_params=_cparams(("parallel",)),
        name="select",
    )(sc)


def _peer_body(h2t_ref, u0_ref, un_ref, vt_ref, r2_ref, e2_ref, c1_ref, e1_ref, x1_ref, gate2_ref,
               out_ref, acc_ref, at_ref, w_ref):
    e = pl.program_id(1)
    tm = h2t_ref.shape[1]
    n_i = un_ref.shape[0] // N_KEYS

    @pl.when(e == 0)
    def _():
        acc_ref[...] = jnp.zeros_like(acc_ref)
        at_ref[...] = _dot(u0_ref[...], h2t_ref[...]).astype(BF16)

    assert n_i == SUBLANES
    i0 = pl.multiple_of(e * n_i, SUBLANES)

    def row_bf16(ref, h, ii, ls):
        row = ref[pl.ds(h * N_KEYS + i0, SUBLANES), ls][ii:ii + 1]
        one = jnp.broadcast_to(row, (2 * SUBLANES, LANES)).astype(BF16)
        return jnp.concatenate([one] * (N_KEYS // (2 * SUBLANES)), axis=0)

    def routing_weights(lb):
        ls = slice(lb * LANES, (lb + 1) * LANES)
        for ii in range(n_i):
            rs = slice(ii * N_KEYS, (ii + 1) * N_KEYS)
            coef = None
            for h in range(PEER_HEADS):
                hs = slice(h * N_KEYS, (h + 1) * N_KEYS)
                e2 = e2_ref[hs, ls]
                sel = r2_ref[hs, ls] < row_bf16(c1_ref, h, ii, ls)
                term = jnp.where(sel, e2, jnp.zeros_like(e2)) * row_bf16(e1_ref, h, ii, ls)
                coef = term if coef is None else coef + term
            a = at_ref[rs, ls]
            k0, k1, half = GELU_K0 / PEER_U_SCALE, GELU_K1 / PEER_U_SCALE ** 3, 0.5 / PEER_U_SCALE
            cdf_over_s = half * jnp.tanh(a * (k0 + k1 * (a * a))) + half
            w_ref[rs, ls] = (coef * (a * cdf_over_s)).astype(PEER_MM_DTYPE)

    half = tm // 2
    for ts in (slice(0, half), slice(half, tm)):
        for lb in range(ts.start // LANES, ts.stop // LANES):
            routing_weights(lb)
        at_ref[:, ts] = _dot(un_ref[...], h2t_ref[:, ts]).astype(BF16)
        acc_ref[:, ts] += _dot(vt_ref[...], w_ref[:, ts])

    @pl.when(e == pl.num_programs(1) - 1)
    def _():
        out_ref[...] = x1_ref[...] + gate2_ref[0] * acc_ref[...].T


def _peer(h2t, u, vt, r2, e2, c1, e1, x1, gate2, seq_len):
    D, T = h2t.shape
    NE = u.shape[0]
    tm, te = TM_PEER, TE_PEER
    tiles_per_seq = seq_len // tm
    table = pl.BlockSpec((r2.shape[0], tm), lambda t, e: (0, t))
    return pl.pallas_call(
        _peer_body,
        grid=(T // tm, NE // te),
        in_specs=[pl.BlockSpec((D, tm), lambda t, e: (0, t)),
                  pl.BlockSpec((te, D), lambda t, e: (0, 0)),
                  pl.BlockSpec((te, D), lambda t, e: (jnp.minimum(e + 1, NE // te - 1), 0)),
                  pl.BlockSpec((D, te), lambda t, e: (0, e)),
                  table, table, table, table,
                  pl.BlockSpec((tm, D), lambda t, e: (t, 0)),
                  pl.BlockSpec((1, 1, D), lambda t, e: (t // tiles_per_seq, 0, 0))],
        out_specs=pl.BlockSpec((tm, D), lambda t, e: (t, 0)),
        out_shape=jax.ShapeDtypeStruct((T, D), F32),
        scratch_shapes=[pltpu.VMEM((D, tm), F32), pltpu.VMEM((te, tm), BF16), pltpu.VMEM((te, tm), PEER_MM_DTYPE)],
        compiler_params=_cparams(("parallel", "arbitrary")),
        name="peer",
    )(h2t, u, u, vt, r2, e2, c1, e1, x1, gate2)


def _layer(x, cs_mod, positions, p, l):
    B, S, D = x.shape
    H = MLA_HEADS
    shift1, scale1, gate1, shift2, scale2, gate2 = [cs_mod[:, j].reshape(B, 1, D) for j in range(6)]
    w_in = p["w_in"][l]
    c0, c1, c2 = Q_LORA, Q_LORA + KV_LORA, Q_LORA + KV_LORA + QK_ROPE
    half = QK_ROPE // 2
    gqn = jnp.pad(p["q_norm_g"][l] * (QK_HEAD ** -0.5), (0, LANES - QK_HEAD))
    gkn = jnp.pad(p["k_norm_g"][l], (0, LANES - QK_HEAD))

    def rotary_partner(w, gain):
        wg = w * gain
        t1, t2 = wg[..., QK_NOPE:QK_NOPE + half], wg[..., QK_NOPE + half:QK_HEAD]
        lo = jnp.zeros(w.shape[:-1] + (QK_NOPE,), w.dtype)
        hi = jnp.zeros(w.shape[:-1] + (LANES - QK_HEAD,), w.dtype)
        return jnp.concatenate([lo, -t2, t1, hi], axis=-1)

    w_rope = jnp.pad(w_in[:, c1:c2], ((0, 0), (QK_NOPE, LANES - QK_HEAD)))
    wqkr = jnp.concatenate([w_in[:, :c1], w_rope, rotary_partner(w_rope, gkn)], axis=1).astype(BF16)
    wlru = w_in[:, c2:c2 + 2 * D].astype(BF16)
    wmg = w_in[:, c2 + 2 * D:].astype(BF16)
    wuq3 = jnp.pad(p["w_uq"][l].reshape(Q_LORA, H, QK_HEAD), ((0, 0), (0, 0), (0, LANES - QK_HEAD)))
    wuq = jnp.concatenate([wuq3.reshape(Q_LORA, H * LANES),
                           rotary_partner(wuq3, gqn).reshape(Q_LORA, H * LANES)], axis=1).astype(BF16)
    wukv = p["w_ukv"][l].reshape(KV_LORA, H, QK_NOPE + V_HEAD)
    wuk = jnp.pad(wukv[:, :, :QK_NOPE], ((0, 0), (0, 0), (0, LANES - QK_NOPE))).reshape(KV_LORA, H * LANES).astype(BF16)
    wuv = jnp.pad(wukv[:, :, QK_NOPE:], ((0, 0), (0, 0), (0, LANES - V_HEAD))).reshape(KV_LORA, H * LANES).astype(BF16)
    inv_freq = 1.0 / (ROPE_THETA ** (jnp.arange(0, QK_ROPE, 2, dtype=F32) / QK_ROPE))
    zeros_lo = jnp.zeros((QK_NOPE,), F32)
    zeros_hi = jnp.zeros((LANES - QK_HEAD,), F32)
    freq = jnp.concatenate([zeros_lo, inv_freq, inv_freq, zeros_hi]).reshape(1, LANES)

    q, k, v, lx, lg, mg = _in_proj(
        x, positions.reshape(B, S, 1), shift1, scale1, p["norm1_g"][l].reshape(1, D), wqkr, wlru, wmg,
        p["q_a_norm_g"][l].reshape(1, Q_LORA), wuq, p["kv_a_norm_g"][l].reshape(1, KV_LORA), wuk, wuv,
        gqn.reshape(1, LANES), gkn.reshape(1, LANES), freq)
    o = _attention(q, k, v)

    C = p["conv_b"].shape[-1]
    wg = jnp.concatenate([p["w_rg"][l], p["w_ig"][l]], axis=-1).astype(BF16)
    rec = _lru(lx, lg, p["conv_w"][l], p["conv_b"][l].reshape(1, C), wg, p["b_rg"][l].reshape(1, C),
               p["b_ig"][l].reshape(1, C), p["lru_lambda"][l].reshape(1, C))

    woa = jnp.pad(p["w_o_attn"][l].reshape(H, V_HEAD, D), ((0, 0), (0, LANES - V_HEAD), (0, 0)))
    woa = woa.reshape(H * LANES, D).astype(BF16)
    keys = p["sub_keys"][l].reshape(PEER_HEADS * 2, N_KEYS, -1).astype(BF16)
    x1, h2t, sc = _merge(o, rec, mg, x, gate1, shift2, scale2, p["norm2_g"][l].reshape(1, D), woa,
                        p["w_o_lru"][l].astype(BF16), p["w_out"][l].astype(BF16), p["w_query"][l].astype(BF16), keys)
    r2, e2, c1, e1 = _select(sc)
    out = _peer(h2t, (p["expert_u"][l] * PEER_U_SCALE).astype(PEER_MM_DTYPE), p["expert_v"][l].T.astype(PEER_MM_DTYPE),
                r2, e2, c1, e1, x1.reshape(B * S, D), gate2, S)
    return out.reshape(B, S, D)


def kernel(x, c, positions, w_ada, b_ada, norm1_g, w_in, q_a_norm_g, w_uq, kv_a_norm_g, w_ukv, q_norm_g, k_norm_g, w_o_attn, conv_w, conv_b, w_rg, b_rg, w_ig, b_ig, lru_lambda, w_o_lru, w_out, norm2_g, w_query, sub_keys, expert_u, expert_v):
    p = dict(w_in=w_in, norm1_g=norm1_g, q_a_norm_g=q_a_norm_g, w_uq=w_uq, kv_a_norm_g=kv_a_norm_g, w_ukv=w_ukv,
             q_norm_g=q_norm_g, k_norm_g=k_norm_g, w_o_attn=w_o_attn, conv_w=conv_w, conv_b=conv_b, w_rg=w_rg,
             b_rg=b_rg, w_ig=w_ig, b_ig=b_ig, lru_lambda=lru_lambda, w_o_lru=w_o_lru, w_out=w_out,
             norm2_g=norm2_g, w_query=w_query, sub_keys=sub_keys, expert_u=expert_u, expert_v=expert_v)
    B, D = c.shape
    for l in range(w_ada.shape[0]):
        mod = _mod(c, w_ada[l], b_ada[l]).reshape(B, 6, D)
        x = _layer(x, mod, positions, p, l)
    return x
```

```python
import jax
import jax.numpy as jnp
from jax import lax
from jax.experimental import pallas as pl
from jax.experimental.pallas import tpu as pltpu

F32 = jnp.float32
BF16 = jnp.bfloat16

EPS = 1e-6
LANES = 128
SUBLANES = 8
VMEM_LIMIT = 56 * 1024 * 1024

CHUNK = 64
MLA_HEADS = 8
QK_NOPE = 64
QK_ROPE = 32
QK_HEAD = QK_NOPE + QK_ROPE
V_HEAD = 64
Q_LORA = 256
KV_LORA = 128
ROPE_THETA = 10000.0
LRU_BLOCKS = 8
CONV_WIDTH = 4
LRU_C = 8.0
PEER_HEADS = 8
N_KEYS = 128
PEER_TOPK = 16
GELU_K0 = 0.7978845608028654
GELU_K1 = 0.044715 * GELU_K0

TM_PROJ = 512
TQ = 256
ATT_HEADS_PER_STEP = 8
TS_LRU = 256
TL_SEL = 256
TM_PEER = 512
TE_PEER = 1024
PEER_MM_DTYPE = jnp.float8_e4m3fn
PEER_U_SCALE = 32.0


def _cparams(sem):
    return pltpu.CompilerParams(dimension_semantics=sem, vmem_limit_bytes=VMEM_LIMIT)


def _dot(a, b):
    return jnp.dot(a, b, preferred_element_type=F32)


def _dot_nt(a, b):
    return lax.dot_general(a, b, (((1,), (1,)), ((), ())), preferred_element_type=F32)


def _mod_body(c_ref, w_ref, b_ref, o_ref):
    c = c_ref[...]
    cs = c * jax.nn.sigmoid(c)
    o_ref[...] = _dot(cs.astype(BF16), w_ref[...].astype(BF16)) + b_ref[...]


def _mod(c, w_ada, b_ada):
    B, D = c.shape
    N = w_ada.shape[1]
    tn = 1024
    return pl.pallas_call(
        _mod_body,
        grid=(N // tn,),
        in_specs=[pl.BlockSpec((B, D), lambda n: (0, 0)),
                  pl.BlockSpec((D, tn), lambda n: (0, n)),
                  pl.BlockSpec((1, tn), lambda n: (0, n))],
        out_specs=pl.BlockSpec((B, tn), lambda n: (0, n)),
        out_shape=jax.ShapeDtypeStruct((B, N), F32),
        compiler_params=_cparams(("parallel",)),
        name="mod",
    )(c, w_ada, b_ada.reshape(1, N))


def _in_proj_body(x_ref, pos_ref, shift_ref, scale_ref, g1_ref, wqkr_ref, wlru_ref, wmg_ref,
                  gqa_ref, wuq_ref, gkva_ref, wuk_ref, wuv_ref, gqn_ref, gkn_ref, freq_ref,
                  q_ref, k_ref, v_ref, lx_ref, lg_ref, mg_ref):
    HP = MLA_HEADS * LANES
    x = x_ref[0]
    D = x.shape[-1]
    y = x * lax.rsqrt(jnp.mean(x * x, axis=-1, keepdims=True) + EPS) * g1_ref[...]
    h = y * (1.0 + scale_ref[0]) + shift_ref[0]
    hb = h.astype(BF16)

    lru = _dot(hb, wlru_ref[...])
    lx_ref[0] = lru[:, :D]
    lg_ref[0] = lru[:, D:]
    mg_ref[0] = _dot(hb, wmg_ref[...])

    qkr = _dot(hb, wqkr_ref[...])
    cq = qkr[:, :Q_LORA]
    ckv = qkr[:, Q_LORA:Q_LORA + KV_LORA]
    kr = qkr[:, Q_LORA + KV_LORA:Q_LORA + KV_LORA + LANES]
    kr_partner = qkr[:, Q_LORA + KV_LORA + LANES:]
    qn = cq * lax.rsqrt(jnp.mean(cq * cq, axis=-1, keepdims=True) + EPS) * gqa_ref[...]
    kvn = ckv * lax.rsqrt(jnp.mean(ckv * ckv, axis=-1, keepdims=True) + EPS) * gkva_ref[...]
    kvb = kvn.astype(BF16)
    q_both = _dot(qn.astype(BF16), wuq_ref[...])
    k_all = _dot(kvb, wuk_ref[...])
    v_ref[0] = _dot(kvb, wuv_ref[...]).astype(BF16)

    ang = pos_ref[0].astype(F32) * freq_ref[...]
    cos_l = jnp.cos(ang)
    sin_l = jnp.sin(ang)
    q_cos = cos_l * gqn_ref[...]
    k_cos = cos_l * gkn_ref[...]
    k_sin = kr_partner * sin_l
    inv_n = 1.0 / QK_HEAD
    for hh in range(MLA_HEADS):
        sl = slice(hh * LANES, (hh + 1) * LANES)
        qh = q_both[:, sl]
        rq = lax.rsqrt(jnp.sum(qh * qh, axis=-1, keepdims=True) * inv_n + EPS)
        q_ref[0, :, sl] = ((qh * q_cos + q_both[:, HP + hh * LANES:HP + (hh + 1) * LANES] * sin_l) * rq).astype(BF16)
        kh = k_all[:, sl] + kr
        rk = lax.rsqrt(jnp.sum(kh * kh, axis=-1, keepdims=True) * inv_n + EPS)
        k_ref[0, :, sl] = ((kh * k_cos + k_sin) * rk).astype(BF16)


def _in_proj(x, pos3, shift1, scale1, g1, wqkr, wlru, wmg, gqa, wuq, gkva, wuk, wuv, gqn, gkn, freq):
    B, S, D = x.shape
    tm = TM_PROJ
    HP = MLA_HEADS * LANES
    tok = lambda w: pl.BlockSpec((1, tm, w), lambda b, s: (b, s, 0))
    per_b = pl.BlockSpec((1, 1, D), lambda b, s: (b, 0, 0))
    full = lambda a: pl.BlockSpec(a.shape, lambda b, s: (0,) * a.ndim)
    return pl.pallas_call(
        _in_proj_body,
        grid=(B, S // tm),
        in_specs=[tok(D), tok(1), per_b, per_b, full(g1), full(wqkr), full(wlru), full(wmg),
                  full(gqa), full(wuq), full(gkva), full(wuk), full(wuv), full(gqn), full(gkn),
                  full(freq)],
        out_specs=[tok(HP), tok(HP), tok(HP), tok(D), tok(D), tok(2 * D)],
        out_shape=[jax.ShapeDtypeStruct((B, S, HP), BF16)] * 3
        + [jax.ShapeDtypeStruct((B, S, D), F32)] * 2 + [jax.ShapeDtypeStruct((B, S, 2 * D), F32)],
        compiler_params=_cparams(("parallel", "parallel")),
        name="in_proj",
    )(x, pos3, shift1, scale1, g1, wqkr, wlru, wmg, gqa, wuq, gkva, wuk, wuv, gqn, gkn, freq)


def _attn_body(q_ref, k_ref, v_ref, o_ref):
    qi = pl.program_id(2)
    neg = float(jnp.finfo(jnp.float32).min)
    shift = CHUNK.bit_length() - 1
    qc = jnp.right_shift(lax.broadcasted_iota(jnp.int32, (TQ, TQ), 0), shift)
    kc = jnp.right_shift(lax.broadcasted_iota(jnp.int32, (TQ, TQ), 1), shift)
    allowed = kc <= qc

    for nk in range(1, k_ref.shape[1] // TQ + 1):
        @pl.when(qi == nk - 1)
        def _(nk=nk):
            n_keys = nk * TQ
            for hh in range(ATT_HEADS_PER_STEP):
                hl = slice(hh * LANES, (hh + 1) * LANES)
                q = q_ref[0, :, hl]
                s = _dot_nt(q, k_ref[0, :n_keys, hl])
                diag = jnp.where(allowed, s[:, n_keys - TQ:], neg)
                s = diag if nk == 1 else jnp.concatenate([s[:, :n_keys - TQ], diag], axis=1)
                p = jnp.exp(s - jnp.max(s, axis=-1, keepdims=True))
                l = jnp.sum(p, axis=-1, keepdims=True)
                o_ref[0, :, hl] = (_dot(p.astype(BF16), v_ref[0, :n_keys, hl]) / l).astype(BF16)


def _attention(q, k, v):
    B, S, HP = q.shape
    hw = ATT_HEADS_PER_STEP * LANES
    H = HP // hw
    return pl.pallas_call(
        _attn_body,
        grid=(B, H, S // TQ),
        in_specs=[pl.BlockSpec((1, TQ, hw), lambda b, h, i: (b, i, h)),
                  pl.BlockSpec((1, S, hw), lambda b, h, i: (b, 0, h)),
                  pl.BlockSpec((1, S, hw), lambda b, h, i: (b, 0, h))],
        out_specs=pl.BlockSpec((1, TQ, hw), lambda b, h, i: (b, i, h)),
        out_shape=jax.ShapeDtypeStruct((B, S, HP), BF16),
        compiler_params=_cparams(("parallel", "parallel", "arbitrary")),
        name="attn",
    )(q, k, v)


def _lru_body(lx_ref, lg_ref, cw_ref, cb_ref, wg_ref, brg_ref, big_ref, lam_ref, rec_ref, prev_ref, h_ref):
    si = pl.program_id(1)

    @pl.when(si == 0)
    def _():
        prev_ref[...] = jnp.zeros_like(prev_ref)
        h_ref[...] = jnp.zeros_like(h_ref)

    x = lx_ref[0]
    ts, C = x.shape
    prev8 = prev_ref[...]
    row8 = lax.broadcasted_iota(jnp.int32, prev8.shape, 0)
    xc = x * cw_ref[CONV_WIDTH - 1:CONV_WIDTH, :] + cb_ref[...]
    for d in range(1, CONV_WIDTH):
        xr = pltpu.roll(x, d, 0)
        first = jnp.where(row8 < d, pltpu.roll(prev8, d, 0), xr[:SUBLANES])
        xs = jnp.concatenate([first, xr[SUBLANES:]], axis=0)
        xc = xc + xs * cw_ref[CONV_WIDTH - 1 - d:CONV_WIDTH - d, :]
    prev_ref[...] = x[ts - SUBLANES:]

    xcb = xc.astype(BF16)
    rowmod = jnp.bitwise_and(lax.broadcasted_iota(jnp.int32, (ts, LANES), 0), SUBLANES - 1)
    for blk in range(LRU_BLOCKS):
        sl = slice(blk * LANES, (blk + 1) * LANES)
        xb = xc[:, sl]
        g = _dot(xcb[:, sl], wg_ref[blk])
        r = jax.nn.sigmoid(g[:, :LANES] + brg_ref[:, sl])
        i = jax.nn.sigmoid(g[:, LANES:] + big_ref[:, sl])
        nl = -lam_ref[:, sl]
        softplus = jnp.maximum(nl, 0.0) + jnp.log(1.0 + jnp.exp(-jnp.abs(nl)))
        log_a = (-LRU_C * r) * softplus
        a = jnp.exp(log_a)
        b = jnp.sqrt(1.0 - a * a) * (i * xb)
        for d in (1, 2, 4):
            valid = rowmod >= d
            b = jnp.where(valid, a * pltpu.roll(b, d, 0) + b, b)
            a = jnp.where(valid, a * pltpu.roll(a, d, 0), a)
        carry = h_ref[:, sl]
        hs = []
        for gi in range(ts // SUBLANES):
            rs = slice(gi * SUBLANES, (gi + 1) * SUBLANES)
            hg = a[rs] * carry + b[rs]
            carry = hg[SUBLANES - 1:SUBLANES]
            hs.append(hg)
        h_ref[:, sl] = carry
        hfull = jnp.concatenate(hs, axis=0)
        rec_ref[0, :, sl] = (hfull * jax.nn.gelu(lg_ref[0, :, sl])).astype(BF16)


def _lru(lx, lg, cw, cb, wg, brg, big, lam):
    B, S, C = lx.shape
    ts = TS_LRU
    tok = pl.BlockSpec((1, ts, C), lambda b, s: (b, s, 0))
    full = lambda a: pl.BlockSpec(a.shape, lambda b, s: (0,) * a.ndim)
    return pl.pallas_call(
        _lru_body,
        grid=(B, S // ts),
        in_specs=[tok, tok, full(cw), full(cb), full(wg), full(brg), full(big), full(lam)],
        out_specs=tok,
        out_shape=jax.ShapeDtypeStruct((B, S, C), BF16),
        scratch_shapes=[pltpu.VMEM((SUBLANES, C), F32), pltpu.VMEM((1, C), F32)],
        compiler_params=_cparams(("parallel", "arbitrary")),
        name="lru",
    )(lx, lg, cw, cb, wg, brg, big, lam)


def _merge_body(o_ref, rec_ref, mg_ref, x_ref, gate1_ref, shift2_ref, scale2_ref, g2_ref,
                woa_ref, wob_ref, wout_ref, wq_ref, keys_ref, x1_ref, h2_ref, sc_ref):
    D = x_ref.shape[-1]
    ya = _dot(o_ref[0], woa_ref[...])
    yb = _dot(rec_ref[0], wob_ref[...])
    mg = mg_ref[0]
    y = jax.nn.sigmoid(mg[:, :D]) * ya + jax.nn.sigmoid(mg[:, D:]) * yb
    x1 = x_ref[0] + gate1_ref[0] * _dot(y.astype(BF16), wout_ref[...])
    x1_ref[0] = x1
    h2 = x1 * lax.rsqrt(jnp.mean(x1 * x1, axis=-1, keepdims=True) + EPS) * g2_ref[...]
    h2 = h2 * (1.0 + scale2_ref[0]) + shift2_ref[0]
    h2b = h2.astype(BF16)
    h2_ref[...] = h2.T.astype(PEER_MM_DTYPE)
    qp = _dot(h2b, wq_ref[...]).astype(BF16)
    for g in range(keys_ref.shape[0]):
        sc_ref[g * N_KEYS:(g + 1) * N_KEYS, :] = _dot_nt(keys_ref[g], qp[:, g * LANES:(g + 1) * LANES])


def _merge(o, rec, mg, x, gate1, shift2, scale2, g2, woa, wob, wout, wq, keys):
    B, S, D = x.shape
    tm = TM_PROJ
    ns = S // tm
    NG = keys.shape[0]
    tok = lambda w: pl.BlockSpec((1, tm, w), lambda b, s: (b, s, 0))
    per_b = pl.BlockSpec((1, 1, D), lambda b, s: (b, 0, 0))
    full = lambda a: pl.BlockSpec(a.shape, lambda b, s: (0,) * a.ndim)
    return pl.pallas_call(
        _merge_body,
        grid=(B, ns),
        in_specs=[tok(o.shape[-1]), tok(D), tok(2 * D), tok(D), per_b, per_b, per_b, full(g2),
                  full(woa), full(wob), full(wout), full(wq), full(keys)],
        out_specs=[tok(D), pl.BlockSpec((D, tm), lambda b, s: (0, b * ns + s)),
                   pl.BlockSpec((NG * N_KEYS, tm), lambda b, s: (0, b * ns + s))],
        out_shape=[jax.ShapeDtypeStruct((B, S, D), F32), jax.ShapeDtypeStruct((D, B * S), PEER_MM_DTYPE),
                   jax.ShapeDtypeStruct((NG * N_KEYS, B * S), F32)],
        compiler_params=_cparams(("parallel", "parallel")),
        name="merge",
    )(o, rec, mg, x, gate1, shift2, scale2, g2, woa, wob, wout, wq, keys)


def _sort16_network():
    def merge(lo, hi, r):
        step = r * 2
        if step < hi - lo:
            yield from merge(lo, hi, step)
            yield from merge(lo + r, hi, step)
            yield from [(i, i + r) for i in range(lo + r, hi - r, step)]
        else:
            yield (lo, lo + r)

    def sort(lo, hi):
        if hi - lo >= 1:
            mid = lo + (hi - lo) // 2
            yield from sort(lo, mid)
            yield from sort(mid + 1, hi)
            yield from merge(lo, hi, 1)

    return tuple(sort(0, PEER_TOPK - 1))


_SORT16 = _sort16_network()


def _compare_exchange(v, i, j):
    v[i], v[j] = jnp.maximum(v[i], v[j]), jnp.minimum(v[i], v[j])


def _bitonic_merge16(v):
    d = PEER_TOPK // 2
    while d:
        for i in range(PEER_TOPK):
            if not i & d:
                _compare_exchange(v, i, i + d)
        d //= 2


def _merge_top16(best, other):
    z = [jnp.maximum(best[i], other[PEER_TOPK - 1 - i]) if PEER_TOPK - 1 - i < len(other) else best[i]
         for i in range(PEER_TOPK)]
    _bitonic_merge16(z)
    return z


def _top16_of_128(v):
    v = list(v)
    for i, j in _SORT16:
        _compare_exchange(v, i, j)
    shift = SUBLANES // 2
    while shift:
        v = _merge_top16(v, [pltpu.roll(x, shift, 0) for x in v])
        shift //= 2
    return v


def _select_body(sc_ref, r2_ref, e2_ref, c1_ref, e1_ref):
    K = PEER_TOPK
    H = PEER_HEADS
    nv = N_KEYS // SUBLANES
    tl = sc_ref.shape[1]
    sub = lax.broadcasted_iota(jnp.int32, (SUBLANES, tl), 0)

    def group(g):
        return [sc_ref[g * N_KEYS + k * SUBLANES:g * N_KEYS + (k + 1) * SUBLANES, :] for k in range(nv)]

    t1 = [_top16_of_128(group(2 * h)) for h in range(H)]
    t2 = [_top16_of_128(group(2 * h + 1)) for h in range(H)]

    def pack(reps):
        out = reps[0]
        for h in range(1, H):
            out = jnp.where(sub == h, reps[h], out)
        return out

    def unpack(x, h):
        return jnp.broadcast_to(x[h:h + 1, :], x.shape)

    t1p = [pack([t1[h][a] for h in range(H)]) for a in range(K)]
    t2p = [pack([t2[h][b] for h in range(H)]) for b in range(K)]
    cand = [[t1p[a] + t2p[b] for b in range(K // (a + 1))] for a in range(K)]
    best = list(cand[0])
    for a in range(1, SUBLANES):
        best = _merge_top16(best, cand[a])
    best = _merge_top16(best, [cand[a][0] for a in range(SUBLANES, K)])
    thr = best[K - 1]
    z = jnp.ones_like(thr)
    for r in range(1, K):
        z = z + jnp.exp(best[r] - best[0])
    inv_z = 1.0 / z
    phi = []
    for b in range(SUBLANES):
        f = jnp.full_like(thr, float("inf"))
        for a in range(K // (b + 1)):
            f = jnp.where(cand[a][b] >= thr, t1p[a], f)
        phi.append(f)
    top_count = jnp.zeros_like(thr)
    for b in range(K):
        top_count = jnp.where(cand[0][b] >= thr, float(b + 1), top_count)

    for h in range(H):
        phi_h = [unpack(f, h) for f in phi]
        inv_z_h = unpack(inv_z, h)
        top_count_h = unpack(top_count, h)
        c1, e1, r2, e2 = [], [], [], []
        for x in group(2 * h):
            cnt = jnp.zeros_like(x)
            for b in range(SUBLANES):
                cnt = jnp.where(x >= phi_h[b], float(b + 1), cnt)
            c1.append(jnp.where(x >= t1[h][0], top_count_h, cnt))
            e1.append(jnp.exp(x - t1[h][0]))
        for x in group(2 * h + 1):
            rank = jnp.full_like(x, float(K))
            for r in reversed(range(K)):
                rank = jnp.where(x >= t2[h][r], float(r), rank)
            r2.append(rank)
            e2.append(jnp.exp(x - t2[h][0]) * inv_z_h)
        rows = slice(h * N_KEYS, (h + 1) * N_KEYS)
        c1_ref[rows, :] = jnp.concatenate(c1, axis=0)
        e1_ref[rows, :] = jnp.concatenate(e1, axis=0)
        r2_ref[rows, :] = jnp.concatenate(r2, axis=0).astype(BF16)
        e2_ref[rows, :] = jnp.concatenate(e2, axis=0).astype(BF16)


def _select(sc):
    R, T = sc.shape
    tl = TL_SEL
    half = R // 2
    blk = pl.BlockSpec((half, tl), lambda t: (0, t))
    return pl.pallas_call(
        _select_body,
        grid=(T // tl,),
        in_specs=[pl.BlockSpec((R, tl), lambda t: (0, t))],
        out_specs=[blk, blk, blk, blk],
        out_shape=[jax.ShapeDtypeStruct((half, T), BF16), jax.ShapeDtypeStruct((half, T), BF16),
                   jax.ShapeDtypeStruct((half, T), F32), jax.ShapeDtypeStruct((half, T), F32)],
        compiler_params=_cparams(("parallel",)),
        name="select",
    )(sc)


def _peer_body(h2t_ref, u0_ref, un_ref, vt_ref, r2_ref, e2_ref, c1_ref, e1_ref, x1_ref, gate2_ref,
               out_ref, acc_ref, at_ref, w_ref):
    e = pl.program_id(1)
    tm = h2t_ref.shape[1]
    n_i = un_ref.shape[0] // N_KEYS

    @pl.when(e == 0)
    def _():
        acc_ref[...] = jnp.zeros_like(acc_ref)
        at_ref[...] = _dot(u0_ref[...], h2t_ref[...]).astype(BF16)

    assert n_i == SUBLANES
    i0 = pl.multiple_of(e * n_i, SUBLANES)

    def row_bf16(ref, h, ii, ls):
        row = ref[pl.ds(h * N_KEYS + i0, SUBLANES), ls][ii:ii + 1]
        one = jnp.broadcast_to(row, (2 * SUBLANES, LANES)).astype(BF16)
        return jnp.concatenate([one] * (N_KEYS // (2 * SUBLANES)), axis=0)

    def routing_weights(lb):
        ls = slice(lb * LANES, (lb + 1) * LANES)
        for ii in range(n_i):
            rs = slice(ii * N_KEYS, (ii + 1) * N_KEYS)
            coef = None
            for h in range(PEER_HEADS):
                hs = slice(h * N_KEYS, (h + 1) * N_KEYS)
                e2 = e2_ref[hs, ls]
                sel = r2_ref[hs, ls] < row_bf16(c1_ref, h, ii, ls)
                term = jnp.where(sel, e2, jnp.zeros_like(e2)) * row_bf16(e1_ref, h, ii, ls)
                coef = term if coef is None else coef + term
            a = at_ref[rs, ls]
            k0, k1, half = GELU_K0 / PEER_U_SCALE, GELU_K1 / PEER_U_SCALE ** 3, 0.5 / PEER_U_SCALE
            cdf_over_s = half * jnp.tanh(a * (k0 + k1 * (a * a))) + half
            w_ref[rs, ls] = (coef * (a * cdf_over_s)).astype(PEER_MM_DTYPE)

    half = tm // 2
    for ts in (slice(0, half), slice(half, tm)):
        for lb in range(ts.start // LANES, ts.stop // LANES):
            routing_weights(lb)
        at_ref[:, ts] = _dot(un_ref[...], h2t_ref[:, ts]).astype(BF16)
        acc_ref[:, ts] += _dot(vt_ref[...], w_ref[:, ts])

    @pl.when(e == pl.num_programs(1) - 1)
    def _():
        out_ref[...] = x1_ref[...] + gate2_ref[0] * acc_ref[...].T


def _peer(h2t, u, vt, r2, e2, c1, e1, x1, gate2, seq_len):
    D, T = h2t.shape
    NE = u.shape[0]
    tm, te = TM_PEER, TE_PEER
    tiles_per_seq = seq_len // tm
    table = pl.BlockSpec((r2.shape[0], tm), lambda t, e: (0, t))
    return pl.pallas_call(
        _peer_body,
        grid=(T // tm, NE // te),
        in_specs=[pl.BlockSpec((D, tm), lambda t, e: (0, t)),
                  pl.BlockSpec((te, D), lambda t, e: (0, 0)),
                  pl.BlockSpec((te, D), lambda t, e: (jnp.minimum(e + 1, NE // te - 1), 0)),
                  pl.BlockSpec((D, te), lambda t, e: (0, e)),
                  table, table, table, table,
                  pl.BlockSpec((tm, D), lambda t, e: (t, 0)),
                  pl.BlockSpec((1, 1, D), lambda t, e: (t // tiles_per_seq, 0, 0))],
        out_specs=pl.BlockSpec((tm, D), lambda t, e: (t, 0)),
        out_shape=jax.ShapeDtypeStruct((T, D), F32),
        scratch_shapes=[pltpu.VMEM((D, tm), F32), pltpu.VMEM((te, tm), BF16), pltpu.VMEM((te, tm), PEER_MM_DTYPE)],
        compiler_params=_cparams(("parallel", "arbitrary")),
        name="peer",
    )(h2t, u, u, vt, r2, e2, c1, e1, x1, gate2)


def _layer(x, cs_mod, positions, p, l):
    B, S, D = x.shape
    H = MLA_HEADS
    shift1, scale1, gate1, shift2, scale2, gate2 = [cs_mod[:, j].reshape(B, 1, D) for j in range(6)]
    w_in = p["w_in"][l]
    c0, c1, c2 = Q_LORA, Q_LORA + KV_LORA, Q_LORA + KV_LORA + QK_ROPE
    half = QK_ROPE // 2
    gqn = jnp.pad(p["q_norm_g"][l] * (QK_HEAD ** -0.5), (0, LANES - QK_HEAD))
    gkn = jnp.pad(p["k_norm_g"][l], (0, LANES - QK_HEAD))

    def rotary_partner(w, gain):
        wg = w * gain
        t1, t2 = wg[..., QK_NOPE:QK_NOPE + half], wg[..., QK_NOPE + half:QK_HEAD]
        lo = jnp.zeros(w.shape[:-1] + (QK_NOPE,), w.dtype)
        hi = jnp.zeros(w.shape[:-1] + (LANES - QK_HEAD,), w.dtype)
        return jnp.concatenate([lo, -t2, t1, hi], axis=-1)

    w_rope = jnp.pad(w_in[:, c1:c2], ((0, 0), (QK_NOPE, LANES - QK_HEAD)))
    wqkr = jnp.concatenate([w_in[:, :c1], w_rope, rotary_partner(w_rope, gkn)], axis=1).astype(BF16)
    wlru = w_in[:, c2:c2 + 2 * D].astype(BF16)
    wmg = w_in[:, c2 + 2 * D:].astype(BF16)
    wuq3 = jnp.pad(p["w_uq"][l].reshape(Q_LORA, H, QK_HEAD), ((0, 0), (0, 0), (0, LANES - QK_HEAD)))
    wuq = jnp.concatenate([wuq3.reshape(Q_LORA, H * LANES),
                           rotary_partner(wuq3, gqn).reshape(Q_LORA, H * LANES)], axis=1).astype(BF16)
    wukv = p["w_ukv"][l].reshape(KV_LORA, H, QK_NOPE + V_HEAD)
    wuk = jnp.pad(wukv[:, :, :QK_NOPE], ((0, 0), (0, 0), (0, LANES - QK_NOPE))).reshape(KV_LORA, H * LANES).astype(BF16)
    wuv = jnp.pad(wukv[:, :, QK_NOPE:], ((0, 0), (0, 0), (0, LANES - V_HEAD))).reshape(KV_LORA, H * LANES).astype(BF16)
    inv_freq = 1.0 / (ROPE_THETA ** (jnp.arange(0, QK_ROPE, 2, dtype=F32) / QK_ROPE))
    zeros_lo = jnp.zeros((QK_NOPE,), F32)
    zeros_hi = jnp.zeros((LANES - QK_HEAD,), F32)
    freq = jnp.concatenate([zeros_lo, inv_freq, inv_freq, zeros_hi]).reshape(1, LANES)

    q, k, v, lx, lg, mg = _in_proj(
        x, positions.reshape(B, S, 1), shift1, scale1, p["norm1_g"][l].reshape(1, D), wqkr, wlru, wmg,
        p["q_a_norm_g"][l].reshape(1, Q_LORA), wuq, p["kv_a_norm_g"][l].reshape(1, KV_LORA), wuk, wuv,
        gqn.reshape(1, LANES), gkn.reshape(1, LANES), freq)
    o = _attention(q, k, v)

    C = p["conv_b"].shape[-1]
    wg = jnp.concatenate([p["w_rg"][l], p["w_ig"][l]], axis=-1).astype(BF16)
    rec = _lru(lx, lg, p["conv_w"][l], p["conv_b"][l].reshape(1, C), wg, p["b_rg"][l].reshape(1, C),
               p["b_ig"][l].reshape(1, C), p["lru_lambda"][l].reshape(1, C))

    woa = jnp.pad(p["w_o_attn"][l].reshape(H, V_HEAD, D), ((0, 0), (0, LANES - V_HEAD), (0, 0)))
    woa = woa.reshape(H * LANES, D).astype(BF16)
    keys = p["sub_keys"][l].reshape(PEER_HEADS * 2, N_KEYS, -1).astype(BF16)
    x1, h2t, sc = _merge(o, rec, mg, x, gate1, shift2, scale2, p["norm2_g"][l].reshape(1, D), woa,
                        p["w_o_lru"][l].astype(BF16), p["w_out"][l].astype(BF16), p["w_query"][l].astype(BF16), keys)
    r2, e2, c1, e1 = _select(sc)
    out = _peer(h2t, (p["expert_u"][l] * PEER_U_SCALE).astype(PEER_MM_DTYPE), p["expert_v"][l].T.astype(PEER_MM_DTYPE),
                r2, e2, c1, e1, x1.reshape(B * S, D), gate2, S)
    return out.reshape(B, S, D)


def kernel(x, c, positions, w_ada, b_ada, norm1_g, w_in, q_a_norm_g, w_uq, kv_a_norm_g, w_ukv, q_norm_g, k_norm_g, w_o_attn, conv_w, conv_b, w_rg, b_rg, w_ig, b_ig, lru_lambda, w_o_lru, w_out, norm2_g, w_query, sub_keys, expert_u, expert_v):
    p = dict(w_in=w_in, norm1_g=norm1_g, q_a_norm_g=q_a_norm_g, w_uq=w_uq, kv_a_norm_g=kv_a_norm_g, w_ukv=w_ukv,
             q_norm_g=q_norm_g, k_norm_g=k_norm_g, w_o_attn=w_o_attn, conv_w=conv_w, conv_b=conv_b, w_rg=w_rg,
             b_rg=b_rg, w_ig=w_ig, b_ig=b_ig, lru_lambda=lru_lambda, w_o_lru=w_o_lru, w_out=w_out,
             norm2_g=norm2_g, w_query=w_query, sub_keys=sub_keys, expert_u=expert_u, expert_v=expert_v)
    B, D = c.shape
    for l in range(w_ada.shape[0]):
        mod = _mod(c, w_ada[l], b_ada[l]).reshape(B, 6, D)
        x = _layer(x, mod, positions, p, l)
    return x
```
